```python
import math
import jax, jax.numpy as jnp
from jax import lax
import numpy as np

D_MODEL = 2048
BATCH = 8
SEQ = 2048
DEPTH = 2

N_META = 16
GRID_W = 64
BLOCK = 128
HEAD_DIM = 128
D_FF = 4 * D_MODEL
NORM_EPS = 1e-6
ROPE_THETA = 10000.0
MIX_WIDTH = D_MODEL
ATT_WIDTH = 3 * MIX_WIDTH // 4
ATT_HEADS = ATT_WIDTH // HEAD_DIM
ATT_KV_HEADS = ATT_HEADS // 3
ATT_KV_WIDTH = ATT_KV_HEADS * HEAD_DIM
S5_WIDTH = MIX_WIDTH - ATT_WIDTH
S5_GROUP = 16
S5_GROUPS = S5_WIDTH // S5_GROUP
S5_STATE = 64
EVEN_IN = ATT_WIDTH + 2 * ATT_KV_WIDTH + S5_WIDTH
RET_WIDTH = MIX_WIDTH // 2
RET_HEADS = RET_WIDTH // HEAD_DIM
ML_WIDTH = MIX_WIDTH - RET_WIDTH
ML_HEADS = ML_WIDTH // HEAD_DIM
CONV_W = 5
NEG_GATE = -1e4
ODD_IN = 4 * RET_WIDTH + 2 * ML_WIDTH + 4 * ML_HEADS
N_EVEN = (DEPTH + 1) // 2
N_ODD = DEPTH // 2

kernel_name = 'hybrid_bidir_attn_s5_retnet_mlstm'


def rms_norm(x, g):
    xf = x.astype(jnp.float32)
    y = xf * lax.rsqrt(jnp.mean(xf * xf, axis=-1, keepdims=True) + NORM_EPS)
    return (y * g.astype(jnp.float32)).astype(x.dtype)


def head_layer_norm(x, g):
    b, l, h, d = x.shape
    xf = x.astype(jnp.float32)
    xc = xf - jnp.mean(xf, axis=-1, keepdims=True)
    y = xc * lax.rsqrt(jnp.mean(xc * xc, axis=-1, keepdims=True) + NORM_EPS)
    return y.reshape(b, l, h * d) * g.astype(jnp.float32)


def rope_freqs(dim):
    return ROPE_THETA ** (-jnp.arange(dim // 2, dtype=jnp.float32) / (dim // 2))


def rope(x, ang):
    c = jnp.cos(ang)[None, :, None, :]
    s = jnp.sin(ang)[None, :, None, :]
    x1, x2 = jnp.split(x.astype(jnp.float32), 2, axis=-1)
    return jnp.concatenate([x1 * c - x2 * s, x1 * s + x2 * c], axis=-1).astype(x.dtype)


def axial_rope(x, ang_row, ang_col):
    half = x.shape[-1] // 2
    return jnp.concatenate([rope(x[..., :half], ang_row), rope(x[..., half:], ang_col)], axis=-1)


def grid_positions(n_tok):
    rows = n_tok // GRID_W
    row = jnp.concatenate([-jnp.ones((N_META,), jnp.float32),
                           jnp.repeat(jnp.arange(rows, dtype=jnp.float32), GRID_W)])
    col = jnp.concatenate([jnp.arange(N_META, dtype=jnp.float32),
                           jnp.tile(jnp.arange(GRID_W, dtype=jnp.float32), rows)])
    return row, col


def pad_front(a, n):
    return jnp.pad(a, [(0, 0), (n, 0)] + [(0, 0)] * (a.ndim - 2))


def flip_t(a):
    return jnp.flip(a, axis=1)


def sq_relu_mlp(x, w1, w2):
    return jnp.square(jax.nn.relu(x @ w1)) @ w2


def grid_attention(q, k, v):
    b, l, h, d = q.shape
    kvh = k.shape[2]
    grp = h // kvh
    pad = (-l) % BLOCK
    nb = (l + pad) // BLOCK
    qb = pad_front(q, pad).reshape(b, nb, BLOCK, kvh, grp, d).transpose(1, 0, 2, 3, 4, 5)
    scale = d ** -0.5

    def attend(q_blk):
        s = jnp.einsum('bqkgd,bskd->bkgqs', q_blk, k).astype(jnp.float32) * scale
        p = jax.nn.softmax(s, axis=-1).astype(v.dtype)
        return jnp.einsum('bkgqs,bskd->bqkgd', p, v)

    out = lax.map(attend, qb)
    return out.transpose(1, 0, 2, 3, 4, 5).reshape(b, nb * BLOCK, h * d)[:, pad:]


def _complex_scan_combine(left, right):
    a1r, a1i, b1r, b1i = left
    a2r, a2i, b2r, b2i = right
    return (a1r * a2r - a1i * a2i,
            a1r * a2i + a1i * a2r,
            a2r * b1r - a2i * b1i + b2r,
            a2r * b1i + a2i * b1r + b2i)


def s5_direction(u, lam_re, lam_im, log_dt, b_re, b_im, c_re, c_im):
    lr = jnp.minimum(lam_re, -1e-4)
    li = lam_im
    dt = jnp.exp(log_dt)[:, None]
    er = jnp.exp(lr * dt)
    abar_re = er * jnp.cos(li * dt)
    abar_im = er * jnp.sin(li * dt)
    nr = abar_re - 1.0
    den = lr * lr + li * li
    coef_re = (nr * lr + abar_im * li) / den
    coef_im = (abar_im * lr - nr * li) / den
    bbar_re = coef_re[..., None] * b_re - coef_im[..., None] * b_im
    bbar_im = coef_re[..., None] * b_im + coef_im[..., None] * b_re
    bu_re = jnp.einsum('blgh,gph->blgp', u, bbar_re)
    bu_im = jnp.einsum('blgh,gph->blgp', u, bbar_im)
    n_pos = u.shape[1]
    a_re = jnp.broadcast_to(abar_re, (1, n_pos) + abar_re.shape)
    a_im = jnp.broadcast_to(abar_im, (1, n_pos) + abar_im.shape)
    _, _, x_re, x_im = lax.associative_scan(_complex_scan_combine, (a_re, a_im, bu_re, bu_im), axis=1)
    return jnp.einsum('blgp,ghp->blgh', x_re, c_re) - jnp.einsum('blgp,ghp->blgh', x_im, c_im)


def s5_mixer(u, lam_re, lam_im, log_dt, b_re, b_im, c_re, c_im, d_skip, glu_w, glu_b):
    b, l, _ = u.shape
    f32 = jnp.float32
    uf = u.astype(f32).reshape(b, l, S5_GROUPS, S5_GROUP)
    lam_re, lam_im, log_dt = lam_re.astype(f32), lam_im.astype(f32), log_dt.astype(f32)
    b_re, b_im, c_re, c_im = b_re.astype(f32), b_im.astype(f32), c_re.astype(f32), c_im.astype(f32)
    y_fw = s5_direction(uf, lam_re[0], lam_im[0], log_dt[0], b_re[0], b_im[0], c_re[0], c_im[0])
    y_bw = flip_t(s5_direction(flip_t(uf), lam_re[1], lam_im[1], log_dt[1], b_re[1], b_im[1], c_re[1], c_im[1]))
    y = y_fw + y_bw + d_skip.astype(f32) * uf
    y = jax.nn.gelu(y.reshape(b, l, S5_WIDTH))
    return y * jax.nn.sigmoid(y @ glu_w.astype(f32) + glu_b.astype(f32))


def retention_direction(q, k, v, log_gamma, strict):
    b, lp, h, dk = q.shape
    dv = v.shape[-1]
    nc = lp // BLOCK
    qc = q.reshape(b, nc, BLOCK, h, dk)
    kc = k.reshape(b, nc, BLOCK, h, dk)
    vc = v.reshape(b, nc, BLOCK, h, dv)
    idx = jnp.arange(BLOCK, dtype=jnp.float32)
    diff = idx[:, None] - idx[None, :]
    mask = (diff > 0) if strict else (diff >= 0)
    decay = jnp.where(mask, jnp.exp(jnp.where(mask, diff, 0.0)[None] * log_gamma[:, None, None]), 0.0)
    scores = jnp.einsum('bnqhd,bnshd->bnhqs', qc, kc) * decay
    intra = jnp.einsum('bnhqs,bnshe->bnqhe', scores, vc)
    zeta = jnp.exp((BLOCK - 1 - idx)[:, None] * log_gamma[None, :])
    kv = jnp.einsum('bnshd,bnshe->bnhde', kc * zeta[:, :, None], vc)
    g_chunk = jnp.exp(BLOCK * log_gamma)[None, :, None, None]

    def step(r, kv_c):
        return g_chunk * r + kv_c, r

    _, r_prev = lax.scan(step, jnp.zeros((b, h, dk, dv), kv.dtype), kv.transpose(1, 0, 2, 3, 4))
    xi = jnp.exp((idx + 1.0)[:, None] * log_gamma[None, :])
    inter = jnp.einsum('bnqhd,nbhde->bnqhe', qc, r_prev) * xi[None, None, :, :, None]
    return (intra + inter).reshape(b, lp, h, dv)


def mlstm_direction(q, k, v, log_i, log_f):
    b, lp, h, d = q.shape
    nc = lp // BLOCK
    f32 = jnp.float32
    qc = q.reshape(b, nc, BLOCK, h, d)
    kc = k.reshape(b, nc, BLOCK, h, d)
    vc = v.reshape(b, nc, BLOCK, h, d)
    li = log_i.reshape(b, nc, BLOCK, h).transpose(0, 1, 3, 2)
    bt = jnp.cumsum(log_f.reshape(b, nc, BLOCK, h).transpose(0, 1, 3, 2), axis=-1)
    lower = jnp.tril(jnp.ones((BLOCK, BLOCK), bool))
    dlog = jnp.where(lower, bt[..., :, None] - bt[..., None, :] + li[..., None, :], -jnp.inf)
    a = bt[..., -1:] - bt + li
    m_loc = jnp.max(a, axis=-1)
    w = jnp.exp(a - m_loc[..., None])
    c_loc = jnp.einsum('bnhs,bnshd,bnshe->bnhde', w, kc, vc)
    n_loc = jnp.einsum('bnhs,bnshd->bnhd', w, kc)

    def step(carry, inp):
        c_s, n_s, m_s = carry
        b_last, m_c, c_c, n_c = inp
        m_new = jnp.maximum(b_last + m_s, m_c)
        f_prev = jnp.exp(b_last + m_s - m_new)
        f_loc = jnp.exp(m_c - m_new)
        c_new = f_prev[..., None, None] * c_s + f_loc[..., None, None] * c_c
        n_new = f_prev[..., None] * n_s + f_loc[..., None] * n_c
        return (c_new, n_new, m_new), (c_s, n_s, m_s)

    init = (jnp.zeros((b, h, d, d), c_loc.dtype), jnp.zeros((b, h, d), n_loc.dtype), jnp.zeros((b, h), f32))
    xs = (bt[..., -1].transpose(1, 0, 2), m_loc.transpose(1, 0, 2),
          c_loc.transpose(1, 0, 2, 3, 4), n_loc.transpose(1, 0, 2, 3))
    _, (c_prev, n_prev, m_prev) = lax.scan(step, init, xs)
    c_prev = c_prev.transpose(1, 0, 2, 3, 4)
    n_prev = n_prev.transpose(1, 0, 2, 3)
    m_prev = m_prev.transpose(1, 0, 2)
    g = bt + m_prev[..., None]
    m_t = jnp.maximum(g, jnp.max(dlog, axis=-1))
    s = jnp.einsum('bnqhd,bnshd->bnhqs', qc, kc) * jnp.exp(dlog - m_t[..., None])
    w_inter = jnp.exp(g - m_t)
    num = (jnp.einsum('bnhqs,bnshe->bnhqe', s, vc)
           + w_inter[..., None] * jnp.einsum('bnqhd,bnhde->bnhqe', qc, c_prev))
    den = jnp.sum(s, axis=-1) + w_inter * jnp.einsum('bnqhd,bnhd->bnhq', qc, n_prev)
    out = num / jnp.maximum(jnp.abs(den), jnp.exp(-m_t))[..., None]
    return out.transpose(0, 1, 3, 2, 4).reshape(b, lp, h, d)


def centred_dwconv(x, w, bias):
    out = lax.conv_general_dilated(x, w[:, None, :].astype(x.dtype), window_strides=(1,),
                                   padding=[(CONV_W // 2, CONV_W // 2)],
                                   dimension_numbers=('NWC', 'WIO', 'NWC'),
                                   feature_group_count=x.shape[-1])
    return out + bias.astype(x.dtype)


def even_mixer(h, w_in, w_out, q_norm, k_norm, lam_re, lam_im, log_dt, b_re, b_im, c_re, c_im,
               d_skip, glu_w, glu_b, ang_row, ang_col):
    b, l, _ = h.shape
    q, k, v, u = jnp.split(h @ w_in, [ATT_WIDTH, ATT_WIDTH + ATT_KV_WIDTH, ATT_WIDTH + 2 * ATT_KV_WIDTH], axis=-1)
    q = axial_rope(rms_norm(q.reshape(b, l, ATT_HEADS, HEAD_DIM), q_norm), ang_row, ang_col)
    k = axial_rope(rms_norm(k.reshape(b, l, ATT_KV_HEADS, HEAD_DIM), k_norm), ang_row, ang_col)
    att = grid_attention(q, k, v.reshape(b, l, ATT_KV_HEADS, HEAD_DIM))
    ssm = s5_mixer(u, lam_re, lam_im, log_dt, b_re, b_im, c_re, c_im, d_skip, glu_w, glu_b)
    return jnp.concatenate([att.astype(jnp.float32), ssm], axis=-1).astype(h.dtype) @ w_out


def odd_mixer(h, w_in, w_out, ret_log_decay, ret_norm, conv_w, conv_b, wq, wk, wv, gate_b, ml_norm, ang_lin):
    b, l, _ = h.shape
    f32 = jnp.float32
    pad = (-l) % BLOCK
    splits = [RET_WIDTH, 2 * RET_WIDTH, 3 * RET_WIDTH, 4 * RET_WIDTH,
              4 * RET_WIDTH + ML_WIDTH, 4 * RET_WIDTH + 2 * ML_WIDTH]
    rq, rk, rv, rg, mu, mo, gates = jnp.split(h @ w_in, splits, axis=-1)

    rq = pad_front(rope(rq.reshape(b, l, RET_HEADS, HEAD_DIM), ang_lin) * HEAD_DIM ** -0.5, pad)
    rk = pad_front(rope(rk.reshape(b, l, RET_HEADS, HEAD_DIM), ang_lin), pad)
    rv = pad_front(rv.reshape(b, l, RET_HEADS, HEAD_DIM), pad)
    log_gamma = -jnp.abs(ret_log_decay.astype(f32))
    ret = (retention_direction(rq, rk, rv, log_gamma[0], False)
           + flip_t(retention_direction(flip_t(rq), flip_t(rk), flip_t(rv), log_gamma[1], True)))
    ret = head_layer_norm(ret[:, pad:], ret_norm) * jax.nn.silu(rg.astype(f32))

    uc = jax.nn.silu(centred_dwconv(mu, conv_w, conv_b))
    mq = pad_front(jnp.einsum('blhd,hde->blhe', uc.reshape(b, l, ML_HEADS, HEAD_DIM), wq), pad)
    mk = pad_front(jnp.einsum('blhd,hde->blhe', uc.reshape(b, l, ML_HEADS, HEAD_DIM), wk) * HEAD_DIM ** -0.5, pad)
    mv = pad_front(jnp.einsum('blhd,hde->blhe', mu.reshape(b, l, ML_HEADS, HEAD_DIM), wv), pad)
    g = gates.astype(f32).reshape(b, l, 4, ML_HEADS) + gate_b.astype(f32)
    valid = (jnp.arange(l + pad) >= pad)[None, :, None, None]
    log_i = jnp.where(valid, pad_front(g[:, :, 0::2], pad), NEG_GATE)
    log_f = jnp.where(valid, pad_front(jax.nn.log_sigmoid(g[:, :, 1::2]), pad), 0.0)
    hm = (mlstm_direction(mq, mk, mv, log_i[:, :, 0], log_f[:, :, 0])
          + flip_t(mlstm_direction(flip_t(mq), flip_t(mk), flip_t(mv),
                                   flip_t(log_i[:, :, 1]), flip_t(log_f[:, :, 1]))))
    hm = head_layer_norm(hm[:, pad:], ml_norm) * jax.nn.sigmoid(mo.astype(f32))
    return jnp.concatenate([ret, hm], axis=-1).astype(h.dtype) @ w_out


def setup_inputs(seed: int = 0) -> dict:
    key = jax.random.key(seed)
    ks = iter(jax.random.split(key, 32))
    f32 = jnp.float32

    def nrm(shape, scale):
        return scale * jax.random.normal(next(ks), shape, f32)

    lam_im_base = jnp.pi * jnp.arange(S5_STATE, dtype=f32)
    ret_base = jnp.log(1.0 - 2.0 ** (-5.0 - jnp.arange(RET_HEADS, dtype=f32)))
    forget_base = jnp.linspace(3.0, 6.0, ML_HEADS, dtype=f32)
    gate_rows = jnp.array([0.0, 1.0, 0.0, 1.0], f32)
    return {
        'x': nrm((BATCH, SEQ, D_MODEL), 1.0),
        'meta_tokens': nrm((N_META, D_MODEL), 1.0),
        'norm_gains': 1.0 + nrm((DEPTH, 4, D_MODEL), 0.02),
        'mlp_w1': nrm((DEPTH, D_MODEL, D_FF), D_MODEL ** -0.5),
        'mlp_w2': nrm((DEPTH, D_FF, D_MODEL), D_FF ** -0.5),
        'even_w_in': nrm((N_EVEN, D_MODEL, EVEN_IN), D_MODEL ** -0.5),
        'even_w_out': nrm((N_EVEN, MIX_WIDTH, D_MODEL), MIX_WIDTH ** -0.5),
        'att_q_norm': 1.0 + nrm((N_EVEN, HEAD_DIM), 0.02),
        'att_k_norm': 1.0 + nrm((N_EVEN, HEAD_DIM), 0.02),
        's5_lam_re': -0.5 + nrm((N_EVEN, 2, S5_GROUPS, S5_STATE), 0.01),
        's5_lam_im': lam_im_base + nrm((N_EVEN, 2, S5_GROUPS, S5_STATE), 0.01),
        's5_log_dt': jax.random.uniform(next(ks), (N_EVEN, 2, S5_GROUPS), f32,
                                        minval=math.log(1e-3), maxval=math.log(1e-1)),
        's5_b_re': nrm((N_EVEN, 2, S5_GROUPS, S5_STATE, S5_GROUP), (2 * S5_GROUP) ** -0.5),
        's5_b_im': nrm((N_EVEN, 2, S5_GROUPS, S5_STATE, S5_GROUP), (2 * S5_GROUP) ** -0.5),
        's5_c_re': nrm((N_EVEN, 2, S5_GROUPS, S5_GROUP, S5_STATE), S5_STATE ** -0.5),
        's5_c_im': nrm((N_EVEN, 2, S5_GROUPS, S5_GROUP, S5_STATE), S5_STATE ** -0.5),
        's5_d': nrm((N_EVEN, S5_GROUPS, S5_GROUP), 1.0),
        's5_glu_w': nrm((N_EVEN, S5_WIDTH, S5_WIDTH), S5_WIDTH ** -0.5),
        's5_glu_b': nrm((N_EVEN, S5_WIDTH), 0.02),
        'odd_w_in': nrm((N_ODD, D_MODEL, ODD_IN), D_MODEL ** -0.5),
        'odd_w_out': nrm((N_ODD, MIX_WIDTH, D_MODEL), MIX_WIDTH ** -0.5),
        'ret_log_decay': ret_base * (1.0 + nrm((N_ODD, 2, RET_HEADS), 0.05)),
        'ret_norm': 1.0 + nrm((N_ODD, RET_WIDTH), 0.02),
        'ml_conv_w': nrm((N_ODD, CONV_W, ML_WIDTH), CONV_W ** -0.5),
        'ml_conv_b': nrm((N_ODD, ML_WIDTH), 0.02),
        'ml_wq': nrm((N_ODD, ML_HEADS, HEAD_DIM, HEAD_DIM), HEAD_DIM ** -0.5),
        'ml_wk': nrm((N_ODD, ML_HEADS, HEAD_DIM, HEAD_DIM), HEAD_DIM ** -0.5),
        'ml_wv': nrm((N_ODD, ML_HEADS, HEAD_DIM, HEAD_DIM), HEAD_DIM ** -0.5),
        'ml_gate_b': nrm((N_ODD, 4, ML_HEADS), 0.1) + gate_rows[None, :, None] * forget_base[None, None, :],
        'ml_norm': 1.0 + nrm((N_ODD, ML_WIDTH), 0.02),
    }


def reference(x, meta_tokens, norm_gains, mlp_w1, mlp_w2, even_w_in, even_w_out, att_q_norm, att_k_norm,
              s5_lam_re, s5_lam_im, s5_log_dt, s5_b_re, s5_b_im, s5_c_re, s5_c_im, s5_d, s5_glu_w, s5_glu_b,
              odd_w_in, odd_w_out, ret_log_decay, ret_norm, ml_conv_w, ml_conv_b, ml_wq, ml_wk, ml_wv,
              ml_gate_b, ml_norm):
    b, n_tok, d_model = x.shape
    h = jnp.concatenate([jnp.broadcast_to(meta_tokens.astype(x.dtype)[None], (b, N_META, d_model)), x], axis=1)
    l = h.shape[1]
    row, col = grid_positions(n_tok)
    f_axis = rope_freqs(HEAD_DIM // 2)
    ang_row = row[:, None] * f_axis[None, :]
    ang_col = col[:, None] * f_axis[None, :]
    ang_lin = jnp.arange(l, dtype=jnp.float32)[:, None] * rope_freqs(HEAD_DIM)[None, :]

    for i in range(DEPTH):
        j = i // 2
        hn = rms_norm(h, norm_gains[i, 0])
        if i % 2 == 0:
            mix = even_mixer(hn, even_w_in[j], even_w_out[j], att_q_norm[j], att_k_norm[j],
                             s5_lam_re[j], s5_lam_im[j], s5_log_dt[j], s5_b_re[j], s5_b_im[j],
                             s5_c_re[j], s5_c_im[j], s5_d[j], s5_glu_w[j], s5_glu_b[j], ang_row, ang_col)
        else:
            mix = odd_mixer(hn, odd_w_in[j], odd_w_out[j], ret_log_decay[j], ret_norm[j], ml_conv_w[j],
                            ml_conv_b[j], ml_wq[j], ml_wk[j], ml_wv[j], ml_gate_b[j], ml_norm[j], ang_lin)
        h = h + rms_norm(mix, norm_gains[i, 1])
        h = h + rms_norm(sq_relu_mlp(rms_norm(h, norm_gains[i, 2]), mlp_w1[i], mlp_w2[i]), norm_gains[i, 3])
    return h[:, N_META:]
```

```python
import functools
import math

import jax
import jax.numpy as jnp
from jax import lax
from jax.experimental import pallas as pl
from jax.experimental.pallas import tpu as pltpu

F32 = jnp.float32
BF16 = jnp.bfloat16

N_META = 16
GRID_W = 64
CHUNK = 128
HEAD_DIM = 128
NORM_EPS = 1e-6
ROPE_THETA = 10000.0
ATT_HEADS = 12
ATT_KV_HEADS = 4
ATT_GROUP = ATT_HEADS // ATT_KV_HEADS
S5_GROUP = 16
S5_GROUPS = 32
S5_STATE = 64
S5_T = 16
S5_COLS = S5_T * S5_GROUP
RET_HEADS = 8
ML_HEADS = 8
CONV_W = 5
NEG_GATE = -1e4
VMEM_LIMIT_BYTES = 56 * 1024 * 1024


def _pick(n, cands):
    for c in cands:
        if n % c == 0:
            return c
    raise ValueError(f"no tile for {n} in {cands}")


def _params(*sem):
    return pltpu.CompilerParams(dimension_semantics=sem, vmem_limit_bytes=VMEM_LIMIT_BYTES)


def _dot(a, b):
    return jnp.dot(a, b, preferred_element_type=F32)


def _dot_nt(a, b):
    return lax.dot_general(a, b, (((1,), (1,)), ((), ())), preferred_element_type=F32)


def _dot_tn(a, b):
    return lax.dot_general(a, b, (((0,), (0,)), ((), ())), preferred_element_type=F32)


def _dot_hi(a, b):
    return jnp.dot(a, b, preferred_element_type=F32, precision=lax.Precision.HIGHEST)


def _sigmoid(x):
    return 1.0 / (1.0 + jnp.exp(-x))


def _norm_matmul_kernel(h_ref, g_ref, w_ref, o_ref, xn_ref, *, relu2):
    @pl.when(pl.program_id(1) == 0)
    def _():
        x = h_ref[...]
        ms = jnp.mean(x * x, axis=-1, keepdims=True)
        xn_ref[...] = (x * lax.rsqrt(ms + NORM_EPS) * g_ref[...]).astype(BF16)

    y = _dot(xn_ref[...], w_ref[...])
    if relu2:
        y = jnp.square(jnp.maximum(y, 0.0))
    o_ref[...] = y.astype(o_ref.dtype)


def norm_matmul(h, gain, w, *, relu2=False, out_dtype=F32):
    m, k = h.shape
    n = w.shape[1]
    tm = _pick(m, (512, 384, 256, 128))
    tn = n if n < 128 else _pick(n, (2048, 1536, 1024, 512, 256, 128))
    return pl.pallas_call(
        functools.partial(_norm_matmul_kernel, relu2=relu2),
        grid=(m // tm, n // tn),
        in_specs=[pl.BlockSpec((tm, k), lambda i, j: (i, 0)),
                  pl.BlockSpec((1, k), lambda i, j: (0, 0)),
                  pl.BlockSpec((k, tn), lambda i, j: (0, j))],
        out_specs=pl.BlockSpec((tm, tn), lambda i, j: (i, j)),
        out_shape=jax.ShapeDtypeStruct((m, n), out_dtype),
        scratch_shapes=[pltpu.VMEM((tm, k), BF16)],
        compiler_params=_params("parallel", "arbitrary"),
        name="norm_matmul",
    )(h, gain.reshape(1, k).astype(F32), w)


def _matmul_norm_res_kernel(*refs, widths, nk, tm, lp, pad):
    na = len(widths)
    a_refs = refs[:na]
    w_ref, g_ref, h_ref, o_ref = refs[na:na + 4]
    start = pl.program_id(0) * tm
    row = start + lax.broadcasted_iota(jnp.int32, (tm, 1), 0)
    rel0 = row - (start // lp) * lp
    rel1 = row - ((start + tm - 1) // lp) * lp
    is_pad = ((rel0 >= 0) & (rel0 < pad)) | ((rel1 >= 0) & (rel1 < pad))

    def finish(y):
        ms = jnp.mean(y * y, axis=-1, keepdims=True)
        out = h_ref[...] + y * lax.rsqrt(ms + NORM_EPS) * g_ref[...]
        o_ref[...] = jnp.where(is_pad, 0.0, out)

    part = None
    off = 0
    for a_ref, wd in zip(a_refs, widths):
        d = _dot(a_ref[...], w_ref[off:off + wd, :])
        part = d if part is None else part + d
        off += wd

    if nk == 1:
        finish(part)
    else:
        acc_ref = refs[na + 4]
        kk = pl.program_id(1)

        @pl.when(kk == 0)
        def _():
            acc_ref[...] = part

        @pl.when(kk > 0)
        def _():
            acc_ref[...] += part

        @pl.when(kk == nk - 1)
        def _():
            finish(acc_ref[...])


def matmul_norm_res(parts, w, gain, h, *, lp, pad):
    m, n = h.shape
    widths = tuple(p.shape[1] for p in parts)
    k = sum(widths)
    tm = _pick(m, (512, 384, 256, 128))
    assert tm <= lp
    if len(parts) > 1 or k <= 2048:
        tk, nk = k, 1
    else:
        tk = 2048
        nk = k // tk
        widths = (tk,)
    in_specs = []
    for wd in widths:
        in_specs.append(pl.BlockSpec((tm, wd), lambda i, j: (i, j)))
    in_specs += [pl.BlockSpec((tk, n), lambda i, j: (j, 0)),
                 pl.BlockSpec((1, n), lambda i, j: (0, 0)),
                 pl.BlockSpec((tm, n), lambda i, j: (i, 0))]
    scratch = [pltpu.VMEM((tm, n), F32)] if nk > 1 else []
    return pl.pallas_call(
        functools.partial(_matmul_norm_res_kernel, widths=widths, nk=nk, tm=tm, lp=lp, pad=pad),
        grid=(m // tm, nk),
        in_specs=in_specs,
        out_specs=pl.BlockSpec((tm, n), lambda i, j: (i, 0)),
        out_shape=jax.ShapeDtypeStruct((m, n), F32),
        scratch_shapes=scratch,
        compiler_params=_params("parallel", "arbitrary"),
        name="matmul_norm_res",
    )(*parts, w, gain.reshape(1, n).astype(F32), h)


def _rope_axial(x, c, sa, sb):
    return x * c + pltpu.roll(x, HEAD_DIM - 32, 1) * sa + pltpu.roll(x, 32, 1) * sb


def _attn_kernel(q_ref, k_ref, v_ref, qg_ref, kg_ref, c_ref, sa_ref, sb_ref, o_ref, ks_ref, vs_ref,
                 *, lp, pad, tq):
    qi = pl.program_id(2)

    @pl.when(qi == 0)
    def _():
        k = k_ref[0]
        k = k * lax.rsqrt(jnp.mean(k * k, axis=-1, keepdims=True) + NORM_EPS) * kg_ref[...]
        ks_ref[...] = _rope_axial(k, c_ref[...], sa_ref[...], sb_ref[...]).astype(BF16)
        vs_ref[...] = v_ref[0].astype(BF16)

    r0 = pl.multiple_of(qi * tq, tq)
    c = c_ref[pl.ds(r0, tq), :]
    sa = sa_ref[pl.ds(r0, tq), :]
    sb = sb_ref[pl.ds(r0, tq), :]
    scale = HEAD_DIM ** -0.5
    qs = []
    for g in range(ATT_GROUP):
        q = q_ref[0, :, g * HEAD_DIM:(g + 1) * HEAD_DIM]
        q = q * lax.rsqrt(jnp.mean(q * q, axis=-1, keepdims=True) + NORM_EPS) * qg_ref[...]
        qs.append((_rope_axial(q, c, sa, sb) * scale).astype(BF16))
    qall = jnp.concatenate(qs, axis=0)
    s = _dot_nt(qall, ks_ref[...])
    col = lax.broadcasted_iota(jnp.int32, (1, lp), 1)
    s = s + jnp.where(col < pad, -jnp.inf, 0.0)
    p = jnp.exp(s - jnp.max(s, axis=-1, keepdims=True))
    l = jnp.sum(p, axis=-1, keepdims=True)
    o = _dot(p.astype(BF16), vs_ref[...]) / l
    for g in range(ATT_GROUP):
        o_ref[0, :, g * HEAD_DIM:(g + 1) * HEAD_DIM] = o[g * tq:(g + 1) * tq].astype(o_ref.dtype)


def attention(qkvu3, q_gain, k_gain, tabs, *, pad):
    b, lp, _ = qkvu3.shape
    tq = CHUNK
    gw = ATT_GROUP * HEAD_DIM
    k_col0 = ATT_HEADS
    v_col0 = ATT_HEADS + ATT_KV_HEADS
    full = lambda bi, hi, qi: (0, 0)
    return pl.pallas_call(
        functools.partial(_attn_kernel, lp=lp, pad=pad, tq=tq),
        grid=(b, ATT_KV_HEADS, lp // tq),
        in_specs=[pl.BlockSpec((1, tq, gw), lambda bi, hi, qi: (bi, qi, hi)),
                  pl.BlockSpec((1, lp, HEAD_DIM), lambda bi, hi, qi: (bi, 0, k_col0 + hi)),
                  pl.BlockSpec((1, lp, HEAD_DIM), lambda bi, hi, qi: (bi, 0, v_col0 + hi)),
                  pl.BlockSpec((1, HEAD_DIM), full),
                  pl.BlockSpec((1, HEAD_DIM), full),
                  pl.BlockSpec((lp, HEAD_DIM), full),
                  pl.BlockSpec((lp, HEAD_DIM), full),
                  pl.BlockSpec((lp, HEAD_DIM), full)],
        out_specs=pl.BlockSpec((1, tq, gw), lambda bi, hi, qi: (bi, qi, hi)),
        out_shape=jax.ShapeDtypeStruct((b, lp, ATT_HEADS * HEAD_DIM), BF16),
        scratch_shapes=[pltpu.VMEM((lp, HEAD_DIM), BF16), pltpu.VMEM((lp, HEAD_DIM), BF16)],
        compiler_params=_params("parallel", "parallel", "arbitrary"),
        name="attention",
    )(qkvu3, qkvu3, qkvu3, q_gain.reshape(1, HEAD_DIM).astype(F32), k_gain.reshape(1, HEAD_DIM).astype(F32),
      *tabs)


def _s5_kernel(u_ref, k_ref, w_ref, v_ref, at_ref, y_ref, s_ref, x_ref, *, nchunk, nb):
    u = u_ref[0]
    s_ref[...] = _dot_hi(u, w_ref[0])
    at = at_ref[0]
    afr, afi, abr, abi = (at[:, i * CHUNK:(i + 1) * CHUNK] for i in range(4))

    def body(c, carry):
        xfr, xfi, xbr, xbi = carry
        rf = pl.multiple_of(c * nb, nb)
        rb = pl.multiple_of((nchunk - 1 - c) * nb, nb)
        x_ref[pl.ds(rf, nb), 0:CHUNK] = xfr
        x_ref[pl.ds(rf, nb), CHUNK:2 * CHUNK] = xfi
        x_ref[pl.ds(rb, nb), 2 * CHUNK:3 * CHUNK] = xbr
        x_ref[pl.ds(rb, nb), 3 * CHUNK:4 * CHUNK] = xbi
        sfr = s_ref[pl.ds(rf, nb), 0:CHUNK]
        sfi = s_ref[pl.ds(rf, nb), CHUNK:2 * CHUNK]
        sbr = s_ref[pl.ds(rb, nb), 2 * CHUNK:3 * CHUNK]
        sbi = s_ref[pl.ds(rb, nb), 3 * CHUNK:4 * CHUNK]
        return (afr * xfr - afi * xfi + sfr, afr * xfi + afi * xfr + sfi,
                abr * xbr - abi * xbi + sbr, abr * xbi + abi * xbr + sbi)

    z = jnp.zeros((nb, CHUNK), F32)
    lax.fori_loop(0, nchunk, body, (z, z, z, z))
    y_ref[0] = _dot_hi(u, k_ref[0]) + _dot_hi(x_ref[...], v_ref[0])


def _s5_matrices(lam_re, lam_im, log_dt, b_re, b_im, c_re, c_im, d_skip):
    hi = lax.Precision.HIGHEST
    t = S5_T
    lr = jnp.minimum(lam_re, -1e-4)
    li = lam_im
    dt = jnp.exp(log_dt)[..., None]
    er = jnp.exp(lr * dt)
    abar_re = er * jnp.cos(li * dt)
    abar_im = er * jnp.sin(li * dt)
    nr = abar_re - 1.0
    den = lr * lr + li * li
    coef_re = (nr * lr + abar_im * li) / den
    coef_im = (abar_im * lr - nr * li) / den
    bb_re = coef_re[..., None] * b_re - coef_im[..., None] * b_im
    bb_im = coef_re[..., None] * b_im + coef_im[..., None] * b_re
    kk = jnp.arange(t + 1, dtype=F32)[:, None, None, None]
    mag = jnp.exp(kk * (lr * dt)[None])
    pw_re = mag * jnp.cos(kk * (li * dt)[None])
    pw_im = mag * jnp.sin(kk * (li * dt)[None])
    ce_re = c_re[None] * pw_re[:, :, :, None, :] - c_im[None] * pw_im[:, :, :, None, :]
    ce_im = c_re[None] * pw_im[:, :, :, None, :] + c_im[None] * pw_re[:, :, :, None, :]
    lagk = (jnp.einsum('kdghp,dgpj->kdghj', ce_re, bb_re, precision=hi)
            - jnp.einsum('kdghp,dgpj->kdghj', ce_im, bb_im, precision=hi))
    ti = jnp.arange(t)
    lag = ti[None, :] - ti[:, None]
    mf = lagk[jnp.clip(lag, 0, t), 0]
    mb = lagk[jnp.clip(-lag, 0, t), 1]
    ktot = (jnp.where((lag >= 0)[:, :, None, None, None], mf, 0.0)
            + jnp.where((lag <= 0)[:, :, None, None, None], mb, 0.0))
    g, hh = d_skip.shape
    skip = d_skip[:, :, None] * jnp.eye(hh, dtype=F32)[None]
    ktot = ktot + jnp.where((lag == 0)[:, :, None, None, None], skip[None, None], 0.0)
    ktot = ktot.transpose(2, 0, 4, 1, 3).reshape(g, t * hh, t * hh)

    def bsum(pw_r, pw_i, d):
        wr = pw_r[:, :, :, None] * bb_re[d][None] - pw_i[:, :, :, None] * bb_im[d][None]
        wi = pw_r[:, :, :, None] * bb_im[d][None] + pw_i[:, :, :, None] * bb_re[d][None]
        tr = lambda a: a.transpose(1, 0, 3, 2).reshape(g, t * hh, S5_STATE)
        return tr(wr), tr(wi)

    wf_re, wf_im = bsum(pw_re[t - 1 - ti, 0], pw_im[t - 1 - ti, 0], 0)
    wb_re, wb_im = bsum(pw_re[ti, 1], pw_im[ti, 1], 1)
    padl = lambda a: jnp.pad(a, ((0, 0), (0, 0), (0, CHUNK - S5_STATE)))
    wtot = jnp.concatenate([padl(wf_re), padl(wf_im), padl(wb_re), padl(wb_im)], axis=-1)

    def vmat(ce_r, ce_i):
        tr = lambda a: a.transpose(1, 3, 0, 2).reshape(g, S5_STATE, t * hh)
        return tr(ce_r), tr(-ce_i)

    vf_re, vf_im = vmat(ce_re[ti + 1, 0], ce_im[ti + 1, 0])
    vb_re, vb_im = vmat(ce_re[t - ti, 1], ce_im[t - ti, 1])
    padr = lambda a: jnp.pad(a, ((0, 0), (0, CHUNK - S5_STATE), (0, 0)))
    vtot = jnp.concatenate([padr(vf_re), padr(vf_im), padr(vb_re), padr(vb_im)], axis=1)
    padv = lambda a: jnp.pad(a, ((0, 0), (0, CHUNK - S5_STATE)))
    at = jnp.concatenate([padv(pw_re[t, 0]), padv(pw_im[t, 0]), padv(pw_re[t, 1]), padv(pw_im[t, 1])], axis=-1)
    return ktot, wtot, vtot, at[:, None, :]


def s5_scan(u, mats):
    b, lp, _ = u.shape
    nchunk = lp // S5_T
    ktot, wtot, vtot, at = mats
    ug = u.reshape(b, nchunk, S5_T, S5_GROUPS, S5_GROUP).transpose(3, 1, 0, 2, 4)
    ug = ug.reshape(S5_GROUPS, nchunk * b, S5_COLS)
    rows = nchunk * b
    per_g = lambda g: (g, 0, 0)
    y = pl.pallas_call(
        functools.partial(_s5_kernel, nchunk=nchunk, nb=b),
        grid=(S5_GROUPS,),
        in_specs=[pl.BlockSpec((1, rows, S5_COLS), per_g),
                  pl.BlockSpec((1, S5_COLS, S5_COLS), per_g),
                  pl.BlockSpec((1, S5_COLS, 4 * CHUNK), per_g),
                  pl.BlockSpec((1, 4 * CHUNK, S5_COLS), per_g),
                  pl.BlockSpec((1, 1, 4 * CHUNK), per_g)],
        out_specs=pl.BlockSpec((1, rows, S5_COLS), per_g),
        out_shape=jax.ShapeDtypeStruct((S5_GROUPS, rows, S5_COLS), F32),
        scratch_shapes=[pltpu.VMEM((rows, 4 * CHUNK), F32), pltpu.VMEM((rows, 4 * CHUNK), F32)],
        compiler_params=_params("parallel"),
        name="s5_scan",
    )(ug, ktot, wtot, vtot, at)
    y = y.reshape(S5_GROUPS, nchunk, b, S5_T, S5_GROUP).transpose(2, 1, 3, 0, 4)
    return y.reshape(b * lp, S5_GROUPS * S5_GROUP)


def _s5_glu_kernel(y_ref, w_ref, b_ref, o_ref):
    x = y_ref[...]
    y = x * (0.5 * (1.0 + jnp.tanh(math.sqrt(2.0 / math.pi) * (x + 0.044715 * (x * x * x)))))
    z = _dot(y.astype(BF16), w_ref[...]) + b_ref[...]
    o_ref[...] = (y * _sigmoid(z)).astype(o_ref.dtype)


def s5_glu(y, w, bias):
    m, n = y.shape
    tm = _pick(m, (1024, 512, 384, 256, 128))
    return pl.pallas_call(
        _s5_glu_kernel,
        grid=(m // tm,),
        in_specs=[pl.BlockSpec((tm, n), lambda i: (i, 0)),
                  pl.BlockSpec((n, n), lambda i: (0, 0)),
                  pl.BlockSpec((1, n), lambda i: (0, 0))],
        out_specs=pl.BlockSpec((tm, n), lambda i: (i, 0)),
        out_shape=jax.ShapeDtypeStruct((m, n), BF16),
        compiler_params=_params("parallel"),
        name="s5_glu",
    )(y, w, bias.reshape(1, n).astype(F32))


def _head_norm(x, gain):
    xc = x - jnp.mean(x, axis=-1, keepdims=True)
    return xc * lax.rsqrt(jnp.mean(xc * xc, axis=-1, keepdims=True) + NORM_EPS) * gain


def _ret_kernel(q_ref, k_ref, v_ref, g_ref, c_ref, s_ref, lg_ref, gn_ref, o_ref,
                qs_ref, ks_ref, vs_ref, af_ref, ab_ref, *, nchunk):
    c = c_ref[...]
    s = s_ref[...]
    q = q_ref[0]
    qs_ref[...] = ((q * c + pltpu.roll(q, HEAD_DIM // 2, 1) * s) * HEAD_DIM ** -0.5).astype(BF16)
    k = k_ref[0]
    ks_ref[...] = k * c + pltpu.roll(k, HEAD_DIM // 2, 1) * s
    vs_ref[...] = v_ref[0].astype(BF16)

    lgf = lg_ref[0, 0:1, :]
    lgb = lg_ref[0, 1:2, :]
    ii = lax.broadcasted_iota(jnp.int32, (CHUNK, CHUNK), 0).astype(F32)
    jj = lax.broadcasted_iota(jnp.int32, (CHUNK, CHUNK), 1).astype(F32)
    diff = ii - jj
    dec_f = jnp.where(diff >= 0, jnp.exp(jnp.where(diff >= 0, diff, 0.0) * lgf), 0.0)
    dec_b = jnp.where(diff < 0, jnp.exp(jnp.where(diff < 0, -diff, 0.0) * lgb), 0.0)
    zeta_f = jnp.exp((CHUNK - 1 - ii) * lgf)
    xi_f = jnp.exp((ii + 1.0) * lgf)
    zeta_b = jnp.exp(ii * lgb)
    xi_b = jnp.exp((CHUNK - ii) * lgb)
    gc_f = jnp.exp(CHUNK * lgf)
    gc_b = jnp.exp(CHUNK * lgb)

    def chunk(r, state, dec, zeta, xi, gc, out_ref):
        qc = qs_ref[pl.ds(r, CHUNK), :]
        kc = ks_ref[pl.ds(r, CHUNK), :]
        vc = vs_ref[pl.ds(r, CHUNK), :]
        sc = _dot_nt(qc, kc.astype(BF16)) * dec
        out_ref[pl.ds(r, CHUNK), :] = _dot(sc.astype(BF16), vc) + _dot(qc, state.astype(BF16)) * xi
        return gc * state + _dot_tn((kc * zeta).astype(BF16), vc)

    def body(t, carry):
        rf = pl.multiple_of(t * CHUNK, CHUNK)
        rb = pl.multiple_of((nchunk - 1 - t) * CHUNK, CHUNK)
        return (chunk(rf, carry[0], dec_f, zeta_f, xi_f, gc_f, af_ref),
                chunk(rb, carry[1], dec_b, zeta_b, xi_b, gc_b, ab_ref))

    z = jnp.zeros((HEAD_DIM, HEAD_DIM), F32)
    lax.fori_loop(0, nchunk, body, (z, z))
    gate = g_ref[0]
    o_ref[0] = (_head_norm(af_ref[...] + ab_ref[...], gn_ref[...]) * (gate * _sigmoid(gate))).astype(o_ref.dtype)


def retention(proj3, log_gamma, gain, cos_t, sin_t):
    b, lp, _ = proj3.shape
    hd = HEAD_DIM
    lg = jnp.broadcast_to(log_gamma.T[:, :, None], (RET_HEADS, 2, hd)).astype(F32)
    blk = lambda off: pl.BlockSpec((1, lp, hd), lambda bi, hi: (bi, 0, off + hi))
    full = lambda bi, hi: (0, 0)
    return pl.pallas_call(
        functools.partial(_ret_kernel, nchunk=lp // CHUNK),
        grid=(b, RET_HEADS),
        in_specs=[blk(0), blk(RET_HEADS), blk(2 * RET_HEADS), blk(3 * RET_HEADS),
                  pl.BlockSpec((lp, hd), full), pl.BlockSpec((lp, hd), full),
                  pl.BlockSpec((1, 2, hd), lambda bi, hi: (hi, 0, 0)),
                  pl.BlockSpec((1, hd), lambda bi, hi: (0, hi))],
        out_specs=pl.BlockSpec((1, lp, hd), lambda bi, hi: (bi, 0, hi)),
        out_shape=jax.ShapeDtypeStruct((b, lp, RET_HEADS * hd), BF16),
        scratch_shapes=[pltpu.VMEM((lp, hd), BF16), pltpu.VMEM((lp, hd), F32), pltpu.VMEM((lp, hd), BF16),
                        pltpu.VMEM((lp, hd), F32), pltpu.VMEM((lp, hd), F32)],
        compiler_params=_params("parallel", "parallel"),
        name="retention",
    )(proj3, proj3, proj3, proj3, cos_t, sin_t, lg, gain.reshape(1, RET_HEADS * hd).astype(F32))


def _mlstm_chunk(qc, kc, vc, li, lf, tri, eye, c_prev, n_prev, m_prev):
    bt_col = jnp.sum(jnp.where(tri, lf, 0.0), axis=1, keepdims=True)
    bt_row = jnp.sum(jnp.where(eye, bt_col, 0.0), axis=0, keepdims=True)
    bt_last = jnp.sum(lf, axis=1, keepdims=True)
    dlog = jnp.where(tri, bt_col - bt_row + li, -jnp.inf)
    a_row = bt_last - bt_row + li
    m_loc = jnp.max(a_row, axis=1, keepdims=True)
    a_col = jnp.sum(jnp.where(eye, a_row, 0.0), axis=1, keepdims=True)
    kw = kc * jnp.exp(a_col - m_loc)
    c_loc = _dot_tn(kw.astype(BF16), vc)
    n_loc = jnp.sum(kw, axis=0, keepdims=True)
    g_col = bt_col + m_prev
    m_t = jnp.maximum(g_col, jnp.max(dlog, axis=1, keepdims=True))
    sc = _dot_nt(qc, kc.astype(BF16)) * jnp.exp(dlog - m_t)
    w_int = jnp.exp(g_col - m_t)
    num = _dot(sc.astype(BF16), vc) + w_int * _dot(qc, c_prev.astype(BF16))
    den = (jnp.sum(sc, axis=1, keepdims=True)
           + w_int * jnp.sum(qc.astype(F32) * n_prev, axis=1, keepdims=True))
    out = num / jnp.maximum(jnp.abs(den), jnp.exp(-m_t))
    m_new = jnp.maximum(bt_last + m_prev, m_loc)
    f_prev = jnp.exp(bt_last + m_prev - m_new)
    f_loc = jnp.exp(m_loc - m_new)
    return out, f_prev * c_prev + f_loc * c_loc, f_prev * n_prev + f_loc * n_loc, m_new


def _mlstm_kernel(mu_ref, mo_ref, gt_ref, gb_ref, cw_ref, cb_ref, wq_ref, wk_ref, wv_ref, gn_ref, o_ref,
                  qs_ref, ks_ref, vs_ref, gl_ref, af_ref, ab_ref, *, lp, pad, nchunk):
    mu = mu_ref[0]
    conv = cb_ref[...]
    for j in range(CONV_W):
        conv = conv + cw_ref[j:j + 1, :] * pltpu.roll(mu, (CONV_W // 2 - j) % lp, 0)
    uc = (conv * _sigmoid(conv)).astype(BF16)
    valid_row = lax.broadcasted_iota(jnp.int32, (lp, 1), 0) >= pad
    qs_ref[...] = jnp.where(valid_row, _dot(uc, wq_ref[0]), 0.0).astype(BF16)
    ks_ref[...] = jnp.where(valid_row, _dot(uc, wk_ref[0]) * HEAD_DIM ** -0.5, 0.0)
    vs_ref[...] = jnp.where(valid_row, _dot(mu.astype(BF16), wv_ref[0]), 0.0).astype(BF16)

    g = gt_ref[0, 0] + gb_ref[0][:, 0:1]
    valid_col = lax.broadcasted_iota(jnp.int32, (1, lp), 1) >= pad
    for r in (0, 2):
        gl_ref[r:r + 1, :] = jnp.where(valid_col, g[r:r + 1], NEG_GATE)
        gf = g[r + 1:r + 2]
        log_sig = jnp.minimum(gf, 0.0) - jnp.log(1.0 + jnp.exp(-jnp.abs(gf)))
        gl_ref[r + 1:r + 2, :] = jnp.where(valid_col, log_sig, 0.0)

    ii = lax.broadcasted_iota(jnp.int32, (CHUNK, CHUNK), 0)
    jj = lax.broadcasted_iota(jnp.int32, (CHUNK, CHUNK), 1)
    lower = jj <= ii
    upper = jj >= ii
    eye = jj == ii

    def run(r, state, row, tri, out_ref):
        out, c_s, n_s, m_s = _mlstm_chunk(
            qs_ref[pl.ds(r, CHUNK), :], ks_ref[pl.ds(r, CHUNK), :], vs_ref[pl.ds(r, CHUNK), :],
            gl_ref[row:row + 1, pl.ds(r, CHUNK)], gl_ref[row + 1:row + 2, pl.ds(r, CHUNK)],
            tri, eye, *state)
        out_ref[pl.ds(r, CHUNK), :] = out
        return c_s, n_s, m_s

    def body(t, carry):
        rf = pl.multiple_of(t * CHUNK, CHUNK)
        rb = pl.multiple_of((nchunk - 1 - t) * CHUNK, CHUNK)
        return run(rf, carry[0], 0, lower, af_ref), run(rb, carry[1], 2, upper, ab_ref)

    init = (jnp.zeros((HEAD_DIM, HEAD_DIM), F32), jnp.zeros((1, HEAD_DIM), F32), jnp.zeros((1, 1), F32))
    lax.fori_loop(0, nchunk, body, (init, init))
    o_ref[0] = (_head_norm(af_ref[...] + ab_ref[...], gn_ref[...]) * _sigmoid(mo_ref[0])).astype(o_ref.dtype)


def mlstm(proj3, gates, gate_b, conv_w, conv_b, wq, wk, wv, gain, *, pad, mu_col0, mo_col0):
    b, lp, _ = proj3.shape
    hd = HEAD_DIM
    gt = gates.reshape(b, lp, 4, ML_HEADS).transpose(0, 3, 2, 1)
    gb = jnp.broadcast_to(gate_b.T[:, :, None], (ML_HEADS, 4, hd)).astype(F32)
    blk = lambda off: pl.BlockSpec((1, lp, hd), lambda bi, hi: (bi, 0, off + hi))
    per_h = lambda bi, hi: (hi, 0, 0)
    vec = pl.BlockSpec((1, hd), lambda bi, hi: (0, hi))
    return pl.pallas_call(
        functools.partial(_mlstm_kernel, lp=lp, pad=pad, nchunk=lp // CHUNK),
        grid=(b, ML_HEADS),
        in_specs=[blk(mu_col0), blk(mo_col0),
                  pl.BlockSpec((1, 1, 4, lp), lambda bi, hi: (bi, hi, 0, 0)),
                  pl.BlockSpec((1, 4, hd), per_h),
                  pl.BlockSpec((CONV_W, hd), lambda bi, hi: (0, hi)),
                  vec,
                  pl.BlockSpec((1, hd, hd), per_h), pl.BlockSpec((1, hd, hd), per_h),
                  pl.BlockSpec((1, hd, hd), per_h),
                  vec],
        out_specs=pl.BlockSpec((1, lp, hd), lambda bi, hi: (bi, 0, hi)),
        out_shape=jax.ShapeDtypeStruct((b, lp, ML_HEADS * hd), BF16),
        scratch_shapes=[pltpu.VMEM((lp, hd), BF16), pltpu.VMEM((lp, hd), F32), pltpu.VMEM((lp, hd), BF16),
                        pltpu.VMEM((4, lp), F32), pltpu.VMEM((lp, hd), F32), pltpu.VMEM((lp, hd), F32)],
        compiler_params=_params("parallel", "parallel"),
        name="mlstm",
    )(proj3, proj3, gt, gb, conv_w.astype(F32), conv_b.reshape(1, -1).astype(F32),
      wq.astype(BF16), wk.astype(BF16), wv.astype(BF16), gain.reshape(1, -1).astype(F32))


def _rope_freqs(dim):
    return ROPE_THETA ** (-jnp.arange(dim // 2, dtype=F32) / (dim // 2))


def _axial_tables(n_tok, pad):
    rows = n_tok // GRID_W
    row = jnp.concatenate([jnp.zeros((pad,), F32), -jnp.ones((N_META,), F32),
                           jnp.repeat(jnp.arange(rows, dtype=F32), GRID_W)])
    col = jnp.concatenate([jnp.zeros((pad,), F32), jnp.arange(N_META, dtype=F32),
                           jnp.tile(jnp.arange(GRID_W, dtype=F32), rows)])
    f = _rope_freqs(HEAD_DIM // 2)
    ang = jnp.concatenate([row[:, None] * f[None, :]] * 2 + [col[:, None] * f[None, :]] * 2, axis=-1)
    first = (jnp.arange(HEAD_DIM) % 64) < 32
    sin = jnp.sin(ang)
    return jnp.cos(ang), jnp.where(first, -sin, 0.0), jnp.where(first, 0.0, sin)


def _linear_tables(l, pad):
    pos = jnp.concatenate([jnp.zeros((pad,), F32), jnp.arange(l, dtype=F32)])
    ang = pos[:, None] * _rope_freqs(HEAD_DIM)[None, :]
    ang = jnp.concatenate([ang, ang], axis=-1)
    sin = jnp.sin(ang)
    return jnp.cos(ang), jnp.where(jnp.arange(HEAD_DIM) < HEAD_DIM // 2, -sin, sin)


def _even_mixer_parts(h, gain, w_in, q_norm, k_norm, s5_params, glu_w, glu_b, tabs, *, b, lp, pad):
    att_w = ATT_HEADS * HEAD_DIM
    u0 = att_w + 2 * ATT_KV_HEADS * HEAD_DIM
    qkvu = norm_matmul(h, gain, w_in.astype(BF16))
    qkvu3 = qkvu.reshape(b, lp, -1)
    att = attention(qkvu3, q_norm, k_norm, tabs, pad=pad).reshape(b * lp, att_w)
    y = s5_scan(qkvu3[:, :, u0:], _s5_matrices(*s5_params))
    ssm = s5_glu(y, glu_w.astype(BF16), glu_b)
    return [att, ssm]


def _odd_mixer_parts(h, gain, w_in, ret_log_decay, ret_norm, conv_w, conv_b, wq, wk, wv, gate_b, ml_norm,
                     tabs, *, b, lp, pad):
    ret_w = RET_HEADS * HEAD_DIM
    ml_w = ML_HEADS * HEAD_DIM
    main = 4 * ret_w + 2 * ml_w
    proj = norm_matmul(h, gain, w_in[:, :main].astype(BF16))
    gates = norm_matmul(h, gain, w_in[:, main:].astype(BF16))
    proj3 = proj.reshape(b, lp, main)
    log_gamma = -jnp.abs(ret_log_decay.astype(F32))
    ret = retention(proj3, log_gamma, ret_norm, *tabs)
    nblk = ret_w // HEAD_DIM
    hm = mlstm(proj3, gates, gate_b, conv_w, conv_b, wq, wk, wv, ml_norm,
               pad=pad, mu_col0=4 * nblk, mo_col0=4 * nblk + ml_w // HEAD_DIM)
    return [ret.reshape(b * lp, ret_w), hm.reshape(b * lp, ml_w)]


def kernel(x, meta_tokens, norm_gains, mlp_w1, mlp_w2, even_w_in, even_w_out, att_q_norm, att_k_norm, s5_lam_re, s5_lam_im, s5_log_dt, s5_b_re, s5_b_im, s5_c_re, s5_c_im, s5_d, s5_glu_w, s5_glu_b, odd_w_in, odd_w_out, ret_log_decay, ret_norm, ml_conv_w, ml_conv_b, ml_wq, ml_wk, ml_wv, ml_gate_b, ml_norm):
    b, n_tok, d_model = x.shape
    l = n_tok + N_META
    pad = (-l) % CHUNK
    lp = l + pad
    depth = norm_gains.shape[0]
    h = jnp.concatenate([jnp.zeros((b, pad, d_model), x.dtype),
                         jnp.broadcast_to(meta_tokens.astype(x.dtype)[None], (b, N_META, d_model)), x], axis=1)
    h = h.reshape(b * lp, d_model)
    axial = _axial_tables(n_tok, pad)
    linear = _linear_tables(l, pad)
    dims = dict(b=b, lp=lp, pad=pad)
    for i in range(depth):
        j = i // 2
        if i % 2 == 0:
            s5_params = (s5_lam_re[j], s5_lam_im[j], s5_log_dt[j], s5_b_re[j], s5_b_im[j], s5_c_re[j],
                         s5_c_im[j], s5_d[j])
            parts = _even_mixer_parts(h, norm_gains[i, 0], even_w_in[j], att_q_norm[j], att_k_norm[j],
                                      s5_params, s5_glu_w[j], s5_glu_b[j], axial, **dims)
            w_out = even_w_out[j]
        else:
            parts = _odd_mixer_parts(h, norm_gains[i, 0], odd_w_in[j], ret_log_decay[j], ret_norm[j],
                                     ml_conv_w[j], ml_conv_b[j], ml_wq[j], ml_wk[j], ml_wv[j], ml_gate_b[j],
                                     ml_norm[j], linear, **dims)
            w_out = odd_w_out[j]
        h = matmul_norm_res(parts, w_out.astype(BF16), norm_gains[i, 1], h, lp=lp, pad=pad)
        hid = norm_matmul(h, norm_gains[i, 2], mlp_w1[i].astype(BF16), relu2=True, out_dtype=BF16)
        h = matmul_norm_res([hid], mlp_w2[i].astype(BF16), norm_gains[i, 3], h, lp=lp, pad=pad)
    return h.reshape(b, lp, d_model)[:, pad + N_META:]
```

```python
import functools
import math

import jax
import jax.numpy as jnp
from jax import lax
from jax.experimental import pallas as pl
from jax.experimental.pallas import tpu as pltpu

F32 = jnp.float32
BF16 = jnp.bfloat16

N_META = 16
GRID_W = 64
CHUNK = 128
HEAD_DIM = 128
NORM_EPS = 1e-6
ROPE_THETA = 10000.0
ATT_HEADS = 12
ATT_KV_HEADS = 4
ATT_GROUP = ATT_HEADS // ATT_KV_HEADS
S5_GROUP = 16
S5_GROUPS = 32
S5_STATE = 64
S5_T = 16
S5_COLS = S5_T * S5_GROUP
RET_HEADS = 8
ML_HEADS = 8
CONV_W = 5
NEG_GATE = -1e4
MXU_COLS = 256
VMEM_LIMIT_BYTES = 56 * 1024 * 1024


def _pick(n, cands):
    for c in cands:
        if n % c == 0:
            return c
    raise ValueError(f"no tile for {n} in {cands}")


def _params(*sem):
    return pltpu.CompilerParams(dimension_semantics=sem, vmem_limit_bytes=VMEM_LIMIT_BYTES)


def _dot(a, b):
    return jnp.dot(a, b, preferred_element_type=F32)


def _dot_nt(a, b):
    return lax.dot_general(a, b, (((1,), (1,)), ((), ())), preferred_element_type=F32)


def _dot_tn(a, b):
    return lax.dot_general(a, b, (((0,), (0,)), ((), ())), preferred_element_type=F32)


def _sigmoid(x):
    return 1.0 / (1.0 + jnp.exp(-x))


def _rmsnorm_kernel(x_ref, g_ref, o_ref):
    x = x_ref[...]
    ms = jnp.mean(x * x, axis=-1, keepdims=True)
    o_ref[...] = (x * lax.rsqrt(ms + NORM_EPS) * g_ref[...]).astype(o_ref.dtype)


def rmsnorm(x, gain):
    m, d = x.shape
    tm = _pick(m, (1024, 512, 384, 256, 128))
    return pl.pallas_call(
        _rmsnorm_kernel,
        grid=(m // tm,),
        in_specs=[pl.BlockSpec((tm, d), lambda i: (i, 0)), pl.BlockSpec((1, d), lambda i: (0, 0))],
        out_specs=pl.BlockSpec((tm, d), lambda i: (i, 0)),
        out_shape=jax.ShapeDtypeStruct((m, d), BF16),
        compiler_params=_params("parallel"),
        name="rmsnorm",
    )(x, gain.reshape(1, d).astype(F32))


def _matmul_kernel(a_ref, w_ref, o_ref, *, relu2):
    y = _dot(a_ref[...], w_ref[...])
    if relu2:
        y = jnp.square(jnp.maximum(y, 0.0))
    o_ref[...] = y.astype(o_ref.dtype)


def matmul(a, w, *, relu2=False, out_dtype=F32):
    m, k = a.shape
    n = w.shape[1]
    tm = _pick(m, (1024, 512, 384, 256, 128))
    tn = _pick(n, (2048, 1536, 1280, 1024, 512, 256, 128))
    return pl.pallas_call(
        functools.partial(_matmul_kernel, relu2=relu2),
        grid=(m // tm, n // tn),
        in_specs=[pl.BlockSpec((tm, k), lambda i, j: (i, 0)),
                  pl.BlockSpec((k, tn), lambda i, j: (0, j))],
        out_specs=pl.BlockSpec((tm, tn), lambda i, j: (i, j)),
        out_shape=jax.ShapeDtypeStruct((m, n), out_dtype),
        compiler_params=_params("parallel", "parallel"),
        name="matmul",
    )(a, w)


def _matmul_norm_res_kernel(*refs, widths, nk, tm, lp, pad, emit_next):
    na = len(widths)
    a_refs = refs[:na]
    w_ref, g_ref, h_ref = refs[na:na + 3]
    rest = refs[na + 3:]
    if emit_next:
        g2_ref, o_ref, n_ref = rest
    else:
        o_ref, = rest
    start = pl.program_id(0) * tm
    row = start + lax.broadcasted_iota(jnp.int32, (tm, 1), 0)
    rel0 = row - (start // lp) * lp
    rel1 = row - ((start + tm - 1) // lp) * lp
    is_pad = ((rel0 >= 0) & (rel0 < pad)) | ((rel1 >= 0) & (rel1 < pad))

    def finish(y):
        ms = jnp.mean(y * y, axis=-1, keepdims=True)
        out = jnp.where(is_pad, 0.0, h_ref[...] + y * lax.rsqrt(ms + NORM_EPS) * g_ref[...])
        o_ref[...] = out
        if emit_next:
            ms2 = jnp.mean(out * out, axis=-1, keepdims=True)
            n_ref[...] = (out * lax.rsqrt(ms2 + NORM_EPS) * g2_ref[...]).astype(n_ref.dtype)

    if nk == 1:
        part = None
        off = 0
        for a_ref, wd in zip(a_refs, widths):
            d = _dot(a_ref[...], w_ref[off:off + wd, :])
            part = d if part is None else part + d
            off += wd
        finish(part)
    else:
        kk = pl.program_id(1)

        @pl.when(kk == 0)
        def _():
            o_ref[...] = jnp.zeros_like(o_ref)

        o_ref[...] += _dot(a_refs[0][...], w_ref[...])

        @pl.when(kk == nk - 1)
        def _():
            finish(o_ref[...])


def matmul_norm_res(parts, w, gain, h, next_gain, *, lp, pad):
    m, n = h.shape
    widths = tuple(p.shape[1] for p in parts)
    k = sum(widths)
    tm = _pick(m, (512, 384, 256, 128))
    assert tm <= lp
    if len(parts) > 1 or k <= 2048:
        tk, nk = k, 1
    else:
        tk = 2048
        nk = k // tk
        widths = (tk,)
    emit_next = next_gain is not None
    row_blk = pl.BlockSpec((tm, n), lambda i, j: (i, 0))
    vec = pl.BlockSpec((1, n), lambda i, j: (0, 0))
    in_specs = [pl.BlockSpec((tm, wd), lambda i, j: (i, j)) for wd in widths]
    in_specs += [pl.BlockSpec((tk, n), lambda i, j: (j, 0)), vec, row_blk]
    args = [*parts, w, gain.reshape(1, n).astype(F32), h]
    out_specs = [row_blk]
    out_shape = [jax.ShapeDtypeStruct((m, n), F32)]
    if emit_next:
        in_specs.append(vec)
        args.append(next_gain.reshape(1, n).astype(F32))
        out_specs.append(row_blk)
        out_shape.append(jax.ShapeDtypeStruct((m, n), BF16))
    res = pl.pallas_call(
        functools.partial(_matmul_norm_res_kernel, widths=widths, nk=nk, tm=tm, lp=lp, pad=pad,
                          emit_next=emit_next),
        grid=(m // tm, nk),
        in_specs=in_specs,
        out_specs=out_specs,
        out_shape=out_shape,
        compiler_params=_params("parallel", "arbitrary"),
        name="matmul_norm_res",
    )(*args)
    return (res[0], res[1]) if emit_next else (res[0], None)


def _rope_axial(x, c, sa, sb):
    return x * c + pltpu.roll(x, HEAD_DIM - 32, 1) * sa + pltpu.roll(x, 32, 1) * sb


def _attn_kernel(q_ref, k_ref, v_ref, qg_ref, kg_ref, c_ref, sa_ref, sb_ref, o_ref, ks_ref, vt_ref, p_ref,
                 *, pad, tq):
    qi = pl.program_id(2)

    @pl.when(qi == 0)
    def _():
        k = k_ref[0]
        k = k * lax.rsqrt(jnp.mean(k * k, axis=-1, keepdims=True) + NORM_EPS) * kg_ref[...]
        ks_ref[...] = _rope_axial(k, c_ref[...], sa_ref[...], sb_ref[...]).astype(BF16)
        vt_ref[...] = v_ref[0].T.astype(BF16)
        p_ref[0:pad, :] = jnp.zeros((pad, p_ref.shape[1]), BF16)

    r0 = pl.multiple_of(qi * tq, tq)
    c = c_ref[pl.ds(r0, tq), :]
    sa = sa_ref[pl.ds(r0, tq), :]
    sb = sb_ref[pl.ds(r0, tq), :]
    scale = HEAD_DIM ** -0.5 * math.log2(math.e)
    qs = []
    for g in range(ATT_GROUP):
        q = q_ref[0, :, g * HEAD_DIM:(g + 1) * HEAD_DIM]
        q = q * lax.rsqrt(jnp.mean(q * q, axis=-1, keepdims=True) + NORM_EPS) * qg_ref[...]
        qs.append((_rope_axial(q, c, sa, sb) * scale).astype(BF16))
    qall = jnp.concatenate(qs, axis=0)
    st = _dot_nt(ks_ref[pad:, :], qall)
    p = jnp.exp2(st - jnp.max(st, axis=0, keepdims=True))
    l = jnp.sum(p, axis=0, keepdims=True)
    p_ref[pad:, :] = p.astype(BF16)
    ot = _dot(vt_ref[...], p_ref[...]) / l
    o = ot.T
    for g in range(ATT_GROUP):
        o_ref[0, :, g * HEAD_DIM:(g + 1) * HEAD_DIM] = o[g * tq:(g + 1) * tq].astype(o_ref.dtype)


def attention(qkvu3, q_gain, k_gain, tabs, *, pad):
    b, lp, _ = qkvu3.shape
    tq = CHUNK
    gw = ATT_GROUP * HEAD_DIM
    k_col0 = ATT_HEADS
    v_col0 = ATT_HEADS + ATT_KV_HEADS
    full = lambda bi, hi, qi: (0, 0)
    return pl.pallas_call(
        functools.partial(_attn_kernel, pad=pad, tq=tq),
        grid=(b, ATT_KV_HEADS, lp // tq),
        in_specs=[pl.BlockSpec((1, tq, gw), lambda bi, hi, qi: (bi, qi, hi)),
                  pl.BlockSpec((1, lp, HEAD_DIM), lambda bi, hi, qi: (bi, 0, k_col0 + hi)),
                  pl.BlockSpec((1, lp, HEAD_DIM), lambda bi, hi, qi: (bi, 0, v_col0 + hi)),
                  pl.BlockSpec((1, HEAD_DIM), full),
                  pl.BlockSpec((1, HEAD_DIM), full),
                  pl.BlockSpec((lp, HEAD_DIM), full),
                  pl.BlockSpec((lp, HEAD_DIM), full),
                  pl.BlockSpec((lp, HEAD_DIM), full)],
        out_specs=pl.BlockSpec((1, tq, gw), lambda bi, hi, qi: (bi, qi, hi)),
        out_shape=jax.ShapeDtypeStruct((b, lp, ATT_HEADS * HEAD_DIM), BF16),
        scratch_shapes=[pltpu.VMEM((lp, HEAD_DIM), BF16), pltpu.VMEM((HEAD_DIM, lp), BF16),
                        pltpu.VMEM((lp, ATT_GROUP * tq), BF16)],
        compiler_params=_params("parallel", "parallel", "arbitrary"),
        name="attention",
    )(qkvu3, qkvu3, qkvu3, q_gain.reshape(1, HEAD_DIM).astype(F32), k_gain.reshape(1, HEAD_DIM).astype(F32),
      *tabs)


def _s5_kernel(u_ref, k_ref, w_ref, v_ref, at_ref, y_ref, s_ref, x_ref, *, nchunk, nb):
    u = u_ref[0]
    s_ref[...] = _dot(u, w_ref[0])
    at = at_ref[0]
    afr, afi, abr, abi = (at[:, i * CHUNK:(i + 1) * CHUNK] for i in range(4))

    def body(c, carry):
        xfr, xfi, xbr, xbi = carry
        rf = pl.multiple_of(c * nb, nb)
        rb = pl.multiple_of((nchunk - 1 - c) * nb, nb)
        x_ref[pl.ds(rf, nb), 0:CHUNK] = xfr
        x_ref[pl.ds(rf, nb), CHUNK:2 * CHUNK] = xfi
        x_ref[pl.ds(rb, nb), 2 * CHUNK:3 * CHUNK] = xbr
        x_ref[pl.ds(rb, nb), 3 * CHUNK:4 * CHUNK] = xbi
        sfr = s_ref[pl.ds(rf, nb), 0:CHUNK]
        sfi = s_ref[pl.ds(rf, nb), CHUNK:2 * CHUNK]
        sbr = s_ref[pl.ds(rb, nb), 2 * CHUNK:3 * CHUNK]
        sbi = s_ref[pl.ds(rb, nb), 3 * CHUNK:4 * CHUNK]
        return (afr * xfr - afi * xfi + sfr, afr * xfi + afi * xfr + sfi,
                abr * xbr - abi * xbi + sbr, abr * xbi + abi * xbr + sbi)

    z = jnp.zeros((nb, CHUNK), F32)
    lax.fori_loop(0, nchunk, body, (z, z, z, z))
    y_ref[0] = _dot(u, k_ref[0]) + _dot(x_ref[...].astype(BF16), v_ref[0])


def _s5_matrices(lam_re, lam_im, log_dt, b_re, b_im, c_re, c_im, d_skip):
    hi = lax.Precision.HIGHEST
    t = S5_T
    lr = jnp.minimum(lam_re, -1e-4)
    li = lam_im
    dt = jnp.exp(log_dt)[..., None]
    er = jnp.exp(lr * dt)
    abar_re = er * jnp.cos(li * dt)
    abar_im = er * jnp.sin(li * dt)
    nr = abar_re - 1.0
    den = lr * lr + li * li
    coef_re = (nr * lr + abar_im * li) / den
    coef_im = (abar_im * lr - nr * li) / den
    bb_re = coef_re[..., None] * b_re - coef_im[..., None] * b_im
    bb_im = coef_re[..., None] * b_im + coef_im[..., None] * b_re
    kk = jnp.arange(t + 1, dtype=F32)[:, None, None, None]
    mag = jnp.exp(kk * (lr * dt)[None])
    pw_re = mag * jnp.cos(kk * (li * dt)[None])
    pw_im = mag * jnp.sin(kk * (li * dt)[None])
    ce_re = c_re[None] * pw_re[:, :, :, None, :] - c_im[None] * pw_im[:, :, :, None, :]
    ce_im = c_re[None] * pw_im[:, :, :, None, :] + c_im[None] * pw_re[:, :, :, None, :]
    lagk = (jnp.einsum('kdghp,dgpj->kdghj', ce_re, bb_re, precision=hi)
            - jnp.einsum('kdghp,dgpj->kdghj', ce_im, bb_im, precision=hi))
    ti = jnp.arange(t)
    lag = ti[None, :] - ti[:, None]
    mf = lagk[jnp.clip(lag, 0, t), 0]
    mb = lagk[jnp.clip(-lag, 0, t), 1]
    ktot = (jnp.where((lag >= 0)[:, :, None, None, None], mf, 0.0)
            + jnp.where((lag <= 0)[:, :, None, None, None], mb, 0.0))
    g, hh = d_skip.shape
    skip = d_skip[:, :, None] * jnp.eye(hh, dtype=F32)[None]
    ktot = ktot + jnp.where((lag == 0)[:, :, None, None, None], skip[None, None], 0.0)
    ktot = ktot.transpose(2, 0, 4, 1, 3).reshape(g, t * hh, t * hh)

    def bsum(pw_r, pw_i, d):
        wr = pw_r[:, :, :, None] * bb_re[d][None] - pw_i[:, :, :, None] * bb_im[d][None]
        wi = pw_r[:, :, :, None] * bb_im[d][None] + pw_i[:, :, :, None] * bb_re[d][None]
        tr = lambda a: a.transpose(1, 0, 3, 2).reshape(g, t * hh, S5_STATE)
        return tr(wr), tr(wi)

    wf_re, wf_im = bsum(pw_re[t - 1 - ti, 0], pw_im[t - 1 - ti, 0], 0)
    wb_re, wb_im = bsum(pw_re[ti, 1], pw_im[ti, 1], 1)
    padl = lambda a: jnp.pad(a, ((0, 0), (0, 0), (0, CHUNK - S5_STATE)))
    wtot = jnp.concatenate([padl(wf_re), padl(wf_im), padl(wb_re), padl(wb_im)], axis=-1)

    def vmat(ce_r, ce_i):
        tr = lambda a: a.transpose(1, 3, 0, 2).reshape(g, S5_STATE, t * hh)
        return tr(ce_r), tr(-ce_i)

    vf_re, vf_im = vmat(ce_re[ti + 1, 0], ce_im[ti + 1, 0])
    vb_re, vb_im = vmat(ce_re[t - ti, 1], ce_im[t - ti, 1])
    padr = lambda a: jnp.pad(a, ((0, 0), (0, CHUNK - S5_STATE), (0, 0)))
    vtot = jnp.concatenate([padr(vf_re), padr(vf_im), padr(vb_re), padr(vb_im)], axis=1)
    padv = lambda a: jnp.pad(a, ((0, 0), (0, CHUNK - S5_STATE)))
    at = jnp.concatenate([padv(pw_re[t, 0]), padv(pw_im[t, 0]), padv(pw_re[t, 1]), padv(pw_im[t, 1])], axis=-1)
    return ktot.astype(BF16), wtot.astype(BF16), vtot.astype(BF16), at[:, None, :]


def s5_scan(u, mats):
    b, lp, _ = u.shape
    nchunk = lp // S5_T
    ktot, wtot, vtot, at = mats
    ug = u.astype(BF16).reshape(b, nchunk, S5_T, S5_GROUPS, S5_GROUP).transpose(3, 1, 0, 2, 4)
    ug = ug.reshape(S5_GROUPS, nchunk * b, S5_COLS)
    rows = nchunk * b
    per_g = lambda g: (g, 0, 0)
    y = pl.pallas_call(
        functools.partial(_s5_kernel, nchunk=nchunk, nb=b),
        grid=(S5_GROUPS,),
        in_specs=[pl.BlockSpec((1, rows, S5_COLS), per_g),
                  pl.BlockSpec((1, S5_COLS, S5_COLS), per_g),
                  pl.BlockSpec((1, S5_COLS, 4 * CHUNK), per_g),
                  pl.BlockSpec((1, 4 * CHUNK, S5_COLS), per_g),
                  pl.BlockSpec((1, 1, 4 * CHUNK), per_g)],
        out_specs=pl.BlockSpec((1, rows, S5_COLS), per_g),
        out_shape=jax.ShapeDtypeStruct((S5_GROUPS, rows, S5_COLS), F32),
        scratch_shapes=[pltpu.VMEM((rows, 4 * CHUNK), F32), pltpu.VMEM((rows, 4 * CHUNK), F32)],
        compiler_params=_params("parallel"),
        name="s5_scan",
    )(ug, ktot, wtot, vtot, at)
    y = y.reshape(S5_GROUPS, nchunk, b, S5_T, S5_GROUP).transpose(2, 1, 3, 0, 4)
    return y.reshape(b * lp, S5_GROUPS * S5_GROUP)


def _s5_glu_kernel(y_ref, w_ref, b_ref, o_ref):
    x = y_ref[...]
    y = x * (0.5 * (1.0 + jnp.tanh(math.sqrt(2.0 / math.pi) * (x + 0.044715 * (x * x * x)))))
    z = _dot(y.astype(BF16), w_ref[...]) + b_ref[...]
    o_ref[...] = (y * _sigmoid(z)).astype(o_ref.dtype)


def s5_glu(y, w, bias):
    m, n = y.shape
    tm = _pick(m, (1024, 512, 384, 256, 128))
    return pl.pallas_call(
        _s5_glu_kernel,
        grid=(m // tm,),
        in_specs=[pl.BlockSpec((tm, n), lambda i: (i, 0)),
                  pl.BlockSpec((n, n), lambda i: (0, 0)),
                  pl.BlockSpec((1, n), lambda i: (0, 0))],
        out_specs=pl.BlockSpec((tm, n), lambda i: (i, 0)),
        out_shape=jax.ShapeDtypeStruct((m, n), BF16),
        compiler_params=_params("parallel"),
        name="s5_glu",
    )(y, w, bias.reshape(1, n).astype(F32))


def _head_norm(x, gain):
    xc = x - jnp.mean(x, axis=-1, keepdims=True)
    return xc * lax.rsqrt(jnp.mean(xc * xc, axis=-1, keepdims=True) + NORM_EPS) * gain


def _ret_kernel(q_ref, k_ref, v_ref, g_ref, c_ref, s_ref, lg_ref, gn_ref, o_ref,
                qs_ref, ks_ref, vs_ref, af_ref, ab_ref, *, nchunk):
    c = c_ref[...]
    s = s_ref[...]
    q = q_ref[0]
    qs_ref[...] = ((q * c + pltpu.roll(q, HEAD_DIM // 2, 1) * s) * HEAD_DIM ** -0.5).astype(BF16)
    k = k_ref[0]
    ks_ref[...] = k * c + pltpu.roll(k, HEAD_DIM // 2, 1) * s
    vs_ref[...] = v_ref[0].astype(BF16)

    lgf = lg_ref[0, 0:1, :]
    lgb = lg_ref[0, 1:2, :]
    ii = lax.broadcasted_iota(jnp.int32, (CHUNK, CHUNK), 0).astype(F32)
    jj = lax.broadcasted_iota(jnp.int32, (CHUNK, CHUNK), 1).astype(F32)
    diff = ii - jj
    dec_f = jnp.where(diff >= 0, jnp.exp(jnp.where(diff >= 0, diff, 0.0) * lgf), 0.0)
    dec_b = jnp.where(diff < 0, jnp.exp(jnp.where(diff < 0, -diff, 0.0) * lgb), 0.0)
    zeta_f = jnp.exp((CHUNK - 1 - ii) * lgf)
    xi_f = jnp.exp((ii + 1.0) * lgf)
    zeta_b = jnp.exp(ii * lgb)
    xi_b = jnp.exp((CHUNK - ii) * lgb)
    gc_f = jnp.exp(CHUNK * lgf)
    gc_b = jnp.exp(CHUNK * lgb)

    def chunk(r, state, dec, zeta, xi, gc, out_ref):
        qc = qs_ref[pl.ds(r, CHUNK), :]
        kc = ks_ref[pl.ds(r, CHUNK), :]
        vc = vs_ref[pl.ds(r, CHUNK), :]
        sc = _dot_nt(qc, kc.astype(BF16)) * dec
        out_ref[pl.ds(r, CHUNK), :] = _dot(sc.astype(BF16), vc) + _dot(qc, state.astype(BF16)) * xi
        return gc * state + _dot_tn((kc * zeta).astype(BF16), vc)

    def body(t, carry):
        rf = pl.multiple_of(t * CHUNK, CHUNK)
        rb = pl.multiple_of((nchunk - 1 - t) * CHUNK, CHUNK)
        return (chunk(rf, carry[0], dec_f, zeta_f, xi_f, gc_f, af_ref),
                chunk(rb, carry[1], dec_b, zeta_b, xi_b, gc_b, ab_ref))

    z = jnp.zeros((HEAD_DIM, HEAD_DIM), F32)
    lax.fori_loop(0, nchunk, body, (z, z))
    gate = g_ref[0]
    o_ref[0] = (_head_norm(af_ref[...] + ab_ref[...], gn_ref[...]) * (gate * _sigmoid(gate))).astype(o_ref.dtype)


def retention(proj3, log_gamma, gain, cos_t, sin_t):
    b, lp, _ = proj3.shape
    hd = HEAD_DIM
    lg = jnp.broadcast_to(log_gamma.T[:, :, None], (RET_HEADS, 2, hd)).astype(F32)
    blk = lambda off: pl.BlockSpec((1, lp, hd), lambda bi, hi: (bi, 0, off + hi))
    full = lambda bi, hi: (0, 0)
    return pl.pallas_call(
        functools.partial(_ret_kernel, nchunk=lp // CHUNK),
        grid=(b, RET_HEADS),
        in_specs=[blk(0), blk(RET_HEADS), blk(2 * RET_HEADS), blk(3 * RET_HEADS),
                  pl.BlockSpec((lp, hd), full), pl.BlockSpec((lp, hd), full),
                  pl.BlockSpec((1, 2, hd), lambda bi, hi: (hi, 0, 0)),
                  pl.BlockSpec((1, hd), lambda bi, hi: (0, hi))],
        out_specs=pl.BlockSpec((1, lp, hd), lambda bi, hi: (bi, 0, hi)),
        out_shape=jax.ShapeDtypeStruct((b, lp, RET_HEADS * hd), BF16),
        scratch_shapes=[pltpu.VMEM((lp, hd), BF16), pltpu.VMEM((lp, hd), F32), pltpu.VMEM((lp, hd), BF16),
                        pltpu.VMEM((lp, hd), F32), pltpu.VMEM((lp, hd), F32)],
        compiler_params=_params("parallel", "parallel"),
        name="retention",
    )(proj3, proj3, proj3, proj3, cos_t, sin_t, lg, gain.reshape(1, RET_HEADS * hd).astype(F32))


def _mlstm_chunk(qc, kc, vc, li, lf, tri, eye, c_prev, n_prev, m_prev):
    bt_col = jnp.sum(jnp.where(tri, lf, 0.0), axis=1, keepdims=True)
    bt_row = jnp.sum(jnp.where(eye, bt_col, 0.0), axis=0, keepdims=True)
    bt_last = jnp.sum(lf, axis=1, keepdims=True)
    dlog = jnp.where(tri, bt_col - bt_row + li, -jnp.inf)
    a_row = bt_last - bt_row + li
    m_loc = jnp.max(a_row, axis=1, keepdims=True)
    a_col = jnp.sum(jnp.where(eye, a_row, 0.0), axis=1, keepdims=True)
    kw = kc * jnp.exp(a_col - m_loc)
    c_loc = _dot_tn(kw.astype(BF16), vc)
    n_loc = jnp.sum(kw, axis=0, keepdims=True)
    g_col = bt_col + m_prev
    m_t = jnp.maximum(g_col, jnp.max(dlog, axis=1, keepdims=True))
    sc = _dot_nt(qc, kc.astype(BF16)) * jnp.exp(dlog - m_t)
    w_int = jnp.exp(g_col - m_t)
    num = _dot(sc.astype(BF16), vc) + w_int * _dot(qc, c_prev.astype(BF16))
    den = (jnp.sum(sc, axis=1, keepdims=True)
           + w_int * jnp.sum(qc.astype(F32) * n_prev, axis=1, keepdims=True))
    out = num / jnp.maximum(jnp.abs(den), jnp.exp(-m_t))
    m_new = jnp.maximum(bt_last + m_prev, m_loc)
    f_prev = jnp.exp(bt_last + m_prev - m_new)
    f_loc = jnp.exp(m_loc - m_new)
    return out, f_prev * c_prev + f_loc * c_loc, f_prev * n_prev + f_loc * n_loc, m_new


def _mlstm_kernel(mu_ref, mo_ref, gt_ref, gb_ref, cw_ref, cb_ref, wq_ref, wk_ref, wv_ref, gn_ref, o_ref,
                  qs_ref, ks_ref, vs_ref, gl_ref, af_ref, ab_ref, *, lp, pad, nchunk):
    mu = mu_ref[0]
    conv = cb_ref[...]
    for j in range(CONV_W):
        conv = conv + cw_ref[j:j + 1, :] * pltpu.roll(mu, (CONV_W // 2 - j) % lp, 0)
    uc = (conv * _sigmoid(conv)).astype(BF16)
    valid_row = lax.broadcasted_iota(jnp.int32, (lp, 1), 0) >= pad
    qs_ref[...] = jnp.where(valid_row, _dot(uc, wq_ref[0]), 0.0).astype(BF16)
    ks_ref[...] = jnp.where(valid_row, _dot(uc, wk_ref[0]) * HEAD_DIM ** -0.5, 0.0)
    vs_ref[...] = jnp.where(valid_row, _dot(mu.astype(BF16), wv_ref[0]), 0.0).astype(BF16)

    g = gt_ref[0, 0] + gb_ref[0][:, 0:1]
    valid_col = lax.broadcasted_iota(jnp.int32, (1, lp), 1) >= pad
    for r in (0, 2):
        gl_ref[r:r + 1, :] = jnp.where(valid_col, g[r:r + 1], NEG_GATE)
        gf = g[r + 1:r + 2]
        log_sig = jnp.minimum(gf, 0.0) - jnp.log(1.0 + jnp.exp(-jnp.abs(gf)))
        gl_ref[r + 1:r + 2, :] = jnp.where(valid_col, log_sig, 0.0)

    ii = lax.broadcasted_iota(jnp.int32, (CHUNK, CHUNK), 0)
    jj = lax.broadcasted_iota(jnp.int32, (CHUNK, CHUNK), 1)
    lower = jj <= ii
    upper = jj >= ii
    eye = jj == ii

    def run(r, state, row, tri, out_ref):
        out, c_s, n_s, m_s = _mlstm_chunk(
            qs_ref[pl.ds(r, CHUNK), :], ks_ref[pl.ds(r, CHUNK), :], vs_ref[pl.ds(r, CHUNK), :],
            gl_ref[row:row + 1, pl.ds(r, CHUNK)], gl_ref[row + 1:row + 2, pl.ds(r, CHUNK)],
            tri, eye, *state)
        out_ref[pl.ds(r, CHUNK), :] = out
        return c_s, n_s, m_s

    def body(t, carry):
        rf = pl.multiple_of(t * CHUNK, CHUNK)
        rb = pl.multiple_of((nchunk - 1 - t) * CHUNK, CHUNK)
        return run(rf, carry[0], 0, lower, af_ref), run(rb, carry[1], 2, upper, ab_ref)

    init = (jnp.zeros((HEAD_DIM, HEAD_DIM), F32), jnp.zeros((1, HEAD_DIM), F32), jnp.zeros((1, 1), F32))
    lax.fori_loop(0, nchunk, body, (init, init))
    o_ref[0] = (_head_norm(af_ref[...] + ab_ref[...], gn_ref[...]) * _sigmoid(mo_ref[0])).astype(o_ref.dtype)


def mlstm(proj3, gates, gate_b, conv_w, conv_b, wq, wk, wv, gain, *, pad, mu_col0, mo_col0):
    b, lp, _ = proj3.shape
    hd = HEAD_DIM
    gt = gates.reshape(b, lp, 4, ML_HEADS).transpose(0, 3, 2, 1)
    gb = jnp.broadcast_to(gate_b.T[:, :, None], (ML_HEADS, 4, hd)).astype(F32)
    blk = lambda off: pl.BlockSpec((1, lp, hd), lambda bi, hi: (bi, 0, off + hi))
    per_h = lambda bi, hi: (hi, 0, 0)
    vec = pl.BlockSpec((1, hd), lambda bi, hi: (0, hi))
    return pl.pallas_call(
        functools.partial(_mlstm_kernel, lp=lp, pad=pad, nchunk=lp // CHUNK),
        grid=(b, ML_HEADS),
        in_specs=[blk(mu_col0), blk(mo_col0),
                  pl.BlockSpec((1, 1, 4, lp), lambda bi, hi: (bi, hi, 0, 0)),
                  pl.BlockSpec((1, 4, hd), per_h),
                  pl.BlockSpec((CONV_W, hd), lambda bi, hi: (0, hi)),
                  vec,
                  pl.BlockSpec((1, hd, hd), per_h), pl.BlockSpec((1, hd, hd), per_h),
                  pl.BlockSpec((1, hd, hd), per_h),
                  vec],
        out_specs=pl.BlockSpec((1, lp, hd), lambda bi, hi: (bi, 0, hi)),
        out_shape=jax.ShapeDtypeStruct((b, lp, ML_HEADS * hd), BF16),
        scratch_shapes=[pltpu.VMEM((lp, hd), BF16), pltpu.VMEM((lp, hd), F32), pltpu.VMEM((lp, hd), BF16),
                        pltpu.VMEM((4, lp), F32), pltpu.VMEM((lp, hd), F32), pltpu.VMEM((lp, hd), F32)],
        compiler_params=_params("parallel", "parallel"),
        name="mlstm",
    )(proj3, proj3, gt, gb, conv_w.astype(F32), conv_b.reshape(1, -1).astype(F32),
      wq.astype(BF16), wk.astype(BF16), wv.astype(BF16), gain.reshape(1, -1).astype(F32))


def _rope_freqs(dim):
    return ROPE_THETA ** (-jnp.arange(dim // 2, dtype=F32) / (dim // 2))


def _axial_tables(n_tok, pad):
    rows = n_tok // GRID_W
    row = jnp.concatenate([jnp.zeros((pad,), F32), -jnp.ones((N_META,), F32),
                           jnp.repeat(jnp.arange(rows, dtype=F32), GRID_W)])
    col = jnp.concatenate([jnp.zeros((pad,), F32), jnp.arange(N_META, dtype=F32),
                           jnp.tile(jnp.arange(GRID_W, dtype=F32), rows)])
    f = _rope_freqs(HEAD_DIM // 2)
    ang = jnp.concatenate([row[:, None] * f[None, :]] * 2 + [col[:, None] * f[None, :]] * 2, axis=-1)
    first = (jnp.arange(HEAD_DIM) % 64) < 32
    sin = jnp.sin(ang)
    return jnp.cos(ang), jnp.where(first, -sin, 0.0), jnp.where(first, 0.0, sin)


def _linear_tables(l, pad):
    pos = jnp.concatenate([jnp.zeros((pad,), F32), jnp.arange(l, dtype=F32)])
    ang = pos[:, None] * _rope_freqs(HEAD_DIM)[None, :]
    ang = jnp.concatenate([ang, ang], axis=-1)
    sin = jnp.sin(ang)
    return jnp.cos(ang), jnp.where(jnp.arange(HEAD_DIM) < HEAD_DIM // 2, -sin, sin)


def _even_mixer_parts(hn, w_in, q_norm, k_norm, s5_params, glu_w, glu_b, tabs, *, b, lp, pad):
    att_w = ATT_HEADS * HEAD_DIM
    u0 = att_w + 2 * ATT_KV_HEADS * HEAD_DIM
    qkvu = matmul(hn, w_in.astype(BF16))
    qkvu3 = qkvu.reshape(b, lp, -1)
    att = attention(qkvu3, q_norm, k_norm, tabs, pad=pad).reshape(b * lp, att_w)
    y = s5_scan(qkvu3[:, :, u0:], _s5_matrices(*s5_params))
    ssm = s5_glu(y, glu_w.astype(BF16), glu_b)
    return [att, ssm]


def _odd_mixer_parts(hn, w_in, ret_log_decay, ret_norm, conv_w, conv_b, wq, wk, wv, gate_b, ml_norm,
                     tabs, *, b, lp, pad):
    ret_w = RET_HEADS * HEAD_DIM
    ml_w = ML_HEADS * HEAD_DIM
    main = 4 * ret_w + 2 * ml_w
    n_in = w_in.shape[1]
    n_pad = -n_in % MXU_COLS
    proj = matmul(hn, jnp.pad(w_in.astype(BF16), ((0, 0), (0, n_pad))))
    proj3 = proj.reshape(b, lp, n_in + n_pad)
    log_gamma = -jnp.abs(ret_log_decay.astype(F32))
    ret = retention(proj3, log_gamma, ret_norm, *tabs)
    nblk = ret_w // HEAD_DIM
    hm = mlstm(proj3, proj3[:, :, main:n_in], gate_b, conv_w, conv_b, wq, wk, wv, ml_norm,
               pad=pad, mu_col0=4 * nblk, mo_col0=4 * nblk + ml_w // HEAD_DIM)
    return [ret.reshape(b * lp, ret_w), hm.reshape(b * lp, ml_w)]


def kernel(x, meta_tokens, norm_gains, mlp_w1, mlp_w2, even_w_in, even_w_out, att_q_norm, att_k_norm, s5_lam_re, s5_lam_im, s5_log_dt, s5_b_re, s5_b_im, s5_c_re, s5_c_im, s5_d, s5_glu_w, s5_glu_b, odd_w_in, odd_w_out, ret_log_decay, ret_norm, ml_conv_w, ml_conv_b, ml_wq, ml_wk, ml_wv, ml_gate_b, ml_norm):
    b, n_tok, d_model = x.shape
    l = n_tok + N_META
    pad = (-l) % CHUNK
    lp = l + pad
    depth = norm_gains.shape[0]
    h = jnp.concatenate([jnp.zeros((b, pad, d_model), x.dtype),
                         jnp.broadcast_to(meta_tokens.astype(x.dtype)[None], (b, N_META, d_model)), x], axis=1)
    h = h.reshape(b * lp, d_model)
    axial = _axial_tables(n_tok, pad)
    linear = _linear_tables(l, pad)
    dims = dict(b=b, lp=lp, pad=pad)
    hn = rmsnorm(h, norm_gains[0, 0])
    for i in range(depth):
        j = i // 2
        if i % 2 == 0:
            s5_params = (s5_lam_re[j], s5_lam_im[j], s5_log_dt[j], s5_b_re[j], s5_b_im[j], s5_c_re[j],
                         s5_c_im[j], s5_d[j])
            parts = _even_mixer_parts(hn, even_w_in[j], att_q_norm[j], att_k_norm[j],
                                      s5_params, s5_glu_w[j], s5_glu_b[j], axial, **dims)
            w_out = even_w_out[j]
        else:
            parts = _odd_mixer_parts(hn, odd_w_in[j], ret_log_decay[j], ret_norm[j],
                                     ml_conv_w[j], ml_conv_b[j], ml_wq[j], ml_wk[j], ml_wv[j], ml_gate_b[j],
                                     ml_norm[j], linear, **dims)
            w_out = odd_w_out[j]
        h, hn = matmul_norm_res(parts, w_out.astype(BF16), norm_gains[i, 1], h, norm_gains[i, 2], lp=lp, pad=pad)
        hid = matmul(hn, mlp_w1[i].astype(BF16), relu2=True, out_dtype=BF16)
        next_gain = norm_gains[i + 1, 0] if i + 1 < depth else None
        h, hn = matmul_norm_res([hid], mlp_w2[i].astype(BF16), norm_gains[i, 3], h, next_gain, lp=lp, pad=pad)
    return h.reshape(b, lp, d_model)[:, pad + N_META:]
```

```python
import functools
import math

import jax
import jax.numpy as jnp
from jax import lax
from jax.experimental import pallas as pl
from jax.experimental.pallas import tpu as pltpu

F32 = jnp.float32
BF16 = jnp.bfloat16

N_META = 16
GRID_W = 64
CHUNK = 128
HEAD_DIM = 128
NORM_EPS = 1e-6
ROPE_THETA = 10000.0
ATT_HEADS = 12
ATT_KV_HEADS = 4
ATT_GROUP = ATT_HEADS // ATT_KV_HEADS
S5_GROUP = 16
S5_GROUPS = 32
S5_STATE = 64
S5_T = 16
S5_COLS = S5_T * S5_GROUP
RET_HEADS = 8
ML_HEADS = 8
CONV_W = 5
NEG_GATE = -1e4
HEADS_PER_STEP = 2
VMEM_LIMIT_BYTES = 56 * 1024 * 1024


def _pick(n, cands):
    for c in cands:
        if n % c == 0:
            return c
    raise ValueError(f"no tile for {n} in {cands}")


def _params(*sem):
    return pltpu.CompilerParams(dimension_semantics=sem, vmem_limit_bytes=VMEM_LIMIT_BYTES)


def _dot(a, b):
    return jnp.dot(a, b, preferred_element_type=F32)


def _dot_nt(a, b):
    return lax.dot_general(a, b, (((1,), (1,)), ((), ())), preferred_element_type=F32)


def _dot_tn(a, b):
    return lax.dot_general(a, b, (((0,), (0,)), ((), ())), preferred_element_type=F32)


def _sigmoid(x):
    return 1.0 / (1.0 + jnp.exp(-x))


def _rmsnorm_kernel(x_ref, g_ref, o_ref):
    x = x_ref[...]
    ms = jnp.mean(x * x, axis=-1, keepdims=True)
    o_ref[...] = (x * lax.rsqrt(ms + NORM_EPS) * g_ref[...]).astype(o_ref.dtype)


def rmsnorm(x, gain):
    m, d = x.shape
    tm = _pick(m, (1024, 512, 384, 256, 128))
    return pl.pallas_call(
        _rmsnorm_kernel,
        grid=(m // tm,),
        in_specs=[pl.BlockSpec((tm, d), lambda i: (i, 0)), pl.BlockSpec((1, d), lambda i: (0, 0))],
        out_specs=pl.BlockSpec((tm, d), lambda i: (i, 0)),
        out_shape=jax.ShapeDtypeStruct((m, d), BF16),
        compiler_params=_params("parallel"),
        name="rmsnorm",
    )(x, gain.reshape(1, d).astype(F32))


def _matmul_kernel(a_ref, w_ref, o_ref, *, relu2):
    y = _dot(a_ref[...], w_ref[...])
    if relu2:
        y = jnp.square(jnp.maximum(y, 0.0))
    o_ref[...] = y.astype(o_ref.dtype)


def matmul(a, w, *, relu2=False, out_dtype=F32):
    m, k = a.shape
    n = w.shape[1]
    tm = _pick(m, (1024, 512, 384, 256, 128))
    tn = _pick(n, (2048, 1536, 1280, 1024, 512, 256, 128))
    return pl.pallas_call(
        functools.partial(_matmul_kernel, relu2=relu2),
        grid=(m // tm, n // tn),
        in_specs=[pl.BlockSpec((tm, k), lambda i, j: (i, 0)),
                  pl.BlockSpec((k, tn), lambda i, j: (0, j))],
        out_specs=pl.BlockSpec((tm, tn), lambda i, j: (i, j)),
        out_shape=jax.ShapeDtypeStruct((m, n), out_dtype),
        compiler_params=_params("parallel", "parallel"),
        name="matmul",
    )(a, w)


def _matmul_wcast_kernel(a_ref, w_ref, o_ref, wb_ref, *, relu2):
    @pl.when(pl.program_id(1) == 0)
    def _():
        wb_ref[...] = w_ref[...].astype(BF16)

    y = _dot(a_ref[...], wb_ref[...])
    if relu2:
        y = jnp.square(jnp.maximum(y, 0.0))
    o_ref[...] = y.astype(o_ref.dtype)


def matmul_wcast(a, w3, layer, n_cols, *, relu2=False, out_dtype=F32):
    m, k = a.shape
    tm = _pick(m, (1024, 512, 384, 256, 128))
    tn = _pick(n_cols, (1024, 512, 256, 128))
    return pl.pallas_call(
        functools.partial(_matmul_wcast_kernel, relu2=relu2),
        grid=(n_cols // tn, m // tm),
        in_specs=[pl.BlockSpec((tm, k), lambda j, i: (i, 0)),
                  pl.BlockSpec((None, k, tn), lambda j, i: (layer, 0, j))],
        out_specs=pl.BlockSpec((tm, tn), lambda j, i: (i, j)),
        out_shape=jax.ShapeDtypeStruct((m, n_cols), out_dtype),
        scratch_shapes=[pltpu.VMEM((k, tn), BF16)],
        compiler_params=_params("parallel", "arbitrary"),
        name="matmul_wcast",
    )(a, w3)


def _matmul_norm_res_kernel(*refs, widths, nk, tm, lp, pad, emit_next):
    na = len(widths)
    a_refs = refs[:na]
    w_ref, g_ref, h_ref = refs[na:na + 3]
    rest = refs[na + 3:]
    if emit_next:
        g2_ref, o_ref, n_ref = rest
    else:
        o_ref, = rest
    start = pl.program_id(0) * tm
    row = start + lax.broadcasted_iota(jnp.int32, (tm, 1), 0)
    rel0 = row - (start // lp) * lp
    rel1 = row - ((start + tm - 1) // lp) * lp
    is_pad = ((rel0 >= 0) & (rel0 < pad)) | ((rel1 >= 0) & (rel1 < pad))

    def finish(y):
        ms = jnp.mean(y * y, axis=-1, keepdims=True)
        out = jnp.where(is_pad, 0.0, h_ref[...] + y * lax.rsqrt(ms + NORM_EPS) * g_ref[...])
        o_ref[...] = out
        if emit_next:
            ms2 = jnp.mean(out * out, axis=-1, keepdims=True)
            n_ref[...] = (out * lax.rsqrt(ms2 + NORM_EPS) * g2_ref[...]).astype(n_ref.dtype)

    if nk == 1:
        part = None
        off = 0
        for a_ref, wd in zip(a_refs, widths):
            d = _dot(a_ref[...], w_ref[off:off + wd, :])
            part = d if part is None else part + d
            off += wd
        finish(part)
    else:
        kk = pl.program_id(1)

        @pl.when(kk == 0)
        def _():
            o_ref[...] = jnp.zeros_like(o_ref)

        o_ref[...] += _dot(a_refs[0][...], w_ref[...])

        @pl.when(kk == nk - 1)
        def _():
            finish(o_ref[...])


def matmul_norm_res(parts, w, gain, h, next_gain, *, lp, pad):
    m, n = h.shape
    widths = tuple(p.shape[1] for p in parts)
    k = sum(widths)
    tm = _pick(m, (512, 384, 256, 128))
    assert tm <= lp
    if len(parts) > 1 or k <= 2048:
        tk, nk = k, 1
    else:
        tk = 2048
        nk = k // tk
        widths = (tk,)
    emit_next = next_gain is not None
    row_blk = pl.BlockSpec((tm, n), lambda i, j: (i, 0))
    vec = pl.BlockSpec((1, n), lambda i, j: (0, 0))
    in_specs = [pl.BlockSpec((tm, wd), lambda i, j: (i, j)) for wd in widths]
    in_specs += [pl.BlockSpec((tk, n), lambda i, j: (j, 0)), vec, row_blk]
    args = [*parts, w, gain.reshape(1, n).astype(F32), h]
    out_specs = [row_blk]
    out_shape = [jax.ShapeDtypeStruct((m, n), F32)]
    if emit_next:
        in_specs.append(vec)
        args.append(next_gain.reshape(1, n).astype(F32))
        out_specs.append(row_blk)
        out_shape.append(jax.ShapeDtypeStruct((m, n), BF16))
    res = pl.pallas_call(
        functools.partial(_matmul_norm_res_kernel, widths=widths, nk=nk, tm=tm, lp=lp, pad=pad,
                          emit_next=emit_next),
        grid=(m // tm, nk),
        in_specs=in_specs,
        out_specs=out_specs,
        out_shape=out_shape,
        compiler_params=_params("parallel", "arbitrary"),
        name="matmul_norm_res",
    )(*args)
    return (res[0], res[1]) if emit_next else (res[0], None)


def _rope_axial(x, c, sa, sb):
    return x * c + pltpu.roll(x, HEAD_DIM - 32, 1) * sa + pltpu.roll(x, 32, 1) * sb


def _attn_kernel(q_ref, k_ref, v_ref, qg_ref, kg_ref, c_ref, sa_ref, sb_ref, o_ref, ks_ref, vt_ref, p_ref,
                 *, pad, tq):
    qi = pl.program_id(2)

    @pl.when(qi == 0)
    def _():
        k = k_ref[0]
        k = k * lax.rsqrt(jnp.mean(k * k, axis=-1, keepdims=True) + NORM_EPS) * kg_ref[...]
        ks_ref[...] = _rope_axial(k, c_ref[...], sa_ref[...], sb_ref[...]).astype(BF16)
        vt_ref[...] = v_ref[0].T.astype(BF16)
        p_ref[0:pad, :] = jnp.zeros((pad, p_ref.shape[1]), BF16)

    r0 = pl.multiple_of(qi * tq, tq)
    c = c_ref[pl.ds(r0, tq), :]
    sa = sa_ref[pl.ds(r0, tq), :]
    sb = sb_ref[pl.ds(r0, tq), :]
    scale = HEAD_DIM ** -0.5 * math.log2(math.e)
    qs = []
    for g in range(ATT_GROUP):
        q = q_ref[0, :, g * HEAD_DIM:(g + 1) * HEAD_DIM]
        q = q * lax.rsqrt(jnp.mean(q * q, axis=-1, keepdims=True) + NORM_EPS) * qg_ref[...]
        qs.append((_rope_axial(q, c, sa, sb) * scale).astype(BF16))
    qall = jnp.concatenate(qs, axis=0)
    st = _dot_nt(ks_ref[pad:, :], qall)
    p = jnp.exp2(st - jnp.max(st, axis=0, keepdims=True))
    l = jnp.sum(p, axis=0, keepdims=True)
    p_ref[pad:, :] = p.astype(BF16)
    ot = _dot(vt_ref[...], p_ref[...]) / l
    o = ot.T
    for g in range(ATT_GROUP):
        o_ref[0, :, g * HEAD_DIM:(g + 1) * HEAD_DIM] = o[g * tq:(g + 1) * tq].astype(o_ref.dtype)


def attention(qkvu3, q_gain, k_gain, tabs, *, pad):
    b, lp, _ = qkvu3.shape
    tq = CHUNK
    gw = ATT_GROUP * HEAD_DIM
    k_col0 = ATT_HEADS
    v_col0 = ATT_HEADS + ATT_KV_HEADS
    full = lambda bi, hi, qi: (0, 0)
    return pl.pallas_call(
        functools.partial(_attn_kernel, pad=pad, tq=tq),
        grid=(b, ATT_KV_HEADS, lp // tq),
        in_specs=[pl.BlockSpec((1, tq, gw), lambda bi, hi, qi: (bi, qi, hi)),
                  pl.BlockSpec((1, lp, HEAD_DIM), lambda bi, hi, qi: (bi, 0, k_col0 + hi)),
                  pl.BlockSpec((1, lp, HEAD_DIM), lambda bi, hi, qi: (bi, 0, v_col0 + hi)),
                  pl.BlockSpec((1, HEAD_DIM), full),
                  pl.BlockSpec((1, HEAD_DIM), full),
                  pl.BlockSpec((lp, HEAD_DIM), full),
                  pl.BlockSpec((lp, HEAD_DIM), full),
                  pl.BlockSpec((lp, HEAD_DIM), full)],
        out_specs=pl.BlockSpec((1, tq, gw), lambda bi, hi, qi: (bi, qi, hi)),
        out_shape=jax.ShapeDtypeStruct((b, lp, ATT_HEADS * HEAD_DIM), BF16),
        scratch_shapes=[pltpu.VMEM((lp, HEAD_DIM), BF16), pltpu.VMEM((HEAD_DIM, lp), BF16),
                        pltpu.VMEM((lp, ATT_GROUP * tq), BF16)],
        compiler_params=_params("parallel", "parallel", "arbitrary"),
        name="attention",
    )(qkvu3, qkvu3, qkvu3, q_gain.reshape(1, HEAD_DIM).astype(F32), k_gain.reshape(1, HEAD_DIM).astype(F32),
      *tabs)


def _s5_kernel(u_ref, k_ref, w_ref, v_ref, at_ref, y_ref, s_ref, x_ref, *, nchunk, nb):
    u = u_ref[0]
    s_ref[...] = _dot(u, w_ref[0])
    at = at_ref[0]
    afr, afi, abr, abi = (at[:, i * CHUNK:(i + 1) * CHUNK] for i in range(4))

    def body(c, carry):
        xfr, xfi, xbr, xbi = carry
        rf = pl.multiple_of(c * nb, nb)
        rb = pl.multiple_of((nchunk - 1 - c) * nb, nb)
        x_ref[pl.ds(rf, nb), 0:CHUNK] = xfr
        x_ref[pl.ds(rf, nb), CHUNK:2 * CHUNK] = xfi
        x_ref[pl.ds(rb, nb), 2 * CHUNK:3 * CHUNK] = xbr
        x_ref[pl.ds(rb, nb), 3 * CHUNK:4 * CHUNK] = xbi
        sfr = s_ref[pl.ds(rf, nb), 0:CHUNK]
        sfi = s_ref[pl.ds(rf, nb), CHUNK:2 * CHUNK]
        sbr = s_ref[pl.ds(rb, nb), 2 * CHUNK:3 * CHUNK]
        sbi = s_ref[pl.ds(rb, nb), 3 * CHUNK:4 * CHUNK]
        return (afr * xfr - afi * xfi + sfr, afr * xfi + afi * xfr + sfi,
                abr * xbr - abi * xbi + sbr, abr * xbi + abi * xbr + sbi)

    z = jnp.zeros((nb, CHUNK), F32)
    lax.fori_loop(0, nchunk, body, (z, z, z, z))
    y_ref[0] = _dot(u, k_ref[0]) + _dot(x_ref[...].astype(BF16), v_ref[0])


def _s5_matrices(lam_re, lam_im, log_dt, b_re, b_im, c_re, c_im, d_skip):
    hi = lax.Precision.HIGHEST
    t = S5_T
    lr = jnp.minimum(lam_re, -1e-4)
    li = lam_im
    dt = jnp.exp(log_dt)[..., None]
    er = jnp.exp(lr * dt)
    abar_re = er * jnp.cos(li * dt)
    abar_im = er * jnp.sin(li * dt)
    nr = abar_re - 1.0
    den = lr * lr + li * li
    coef_re = (nr * lr + abar_im * li) / den
    coef_im = (abar_im * lr - nr * li) / den
    bb_re = coef_re[..., None] * b_re - coef_im[..., None] * b_im
    bb_im = coef_re[..., None] * b_im + coef_im[..., None] * b_re
    kk = jnp.arange(t + 1, dtype=F32)[:, None, None, None]
    mag = jnp.exp(kk * (lr * dt)[None])
    pw_re = mag * jnp.cos(kk * (li * dt)[None])
    pw_im = mag * jnp.sin(kk * (li * dt)[None])
    g, hh = d_skip.shape
    ct_re = c_re.transpose(0, 1, 3, 2)
    ct_im = c_im.transpose(0, 1, 3, 2)
    flat = lambda a: a.reshape(2, g, S5_STATE, hh * hh)
    bc_re = flat(bb_re[..., :, None] * ct_re[..., None, :] - bb_im[..., :, None] * ct_im[..., None, :])
    bc_im = flat(bb_re[..., :, None] * ct_im[..., None, :] + bb_im[..., :, None] * ct_re[..., None, :])
    lagk = (jnp.einsum('kdgp,dgpn->dgkn', pw_re, bc_re, precision=hi)
            - jnp.einsum('kdgp,dgpn->dgkn', pw_im, bc_im, precision=hi))
    ti = jnp.arange(t)
    lag = ti[None, :] - ti[:, None]
    sel = lambda m: m[None, :, :, None]
    skip = (jnp.eye(hh, dtype=F32)[None] * d_skip[:, None, :]).reshape(g, 1, 1, hh * hh)
    ktot = (jnp.where(sel(lag >= 0), lagk[0][:, jnp.clip(lag, 0, t)], 0.0)
            + jnp.where(sel(lag <= 0), lagk[1][:, jnp.clip(-lag, 0, t)], 0.0)
            + jnp.where(sel(lag == 0), skip, 0.0))
    ktot = ktot.reshape(g, t, t, hh, hh).transpose(0, 1, 3, 2, 4).reshape(g, t * hh, t * hh)

    def bsum(pw_r, pw_i, d):
        wr = pw_r[:, :, :, None] * bb_re[d][None] - pw_i[:, :, :, None] * bb_im[d][None]
        wi = pw_r[:, :, :, None] * bb_im[d][None] + pw_i[:, :, :, None] * bb_re[d][None]
        tr = lambda a: a.transpose(1, 0, 3, 2).reshape(g, t * hh, S5_STATE)
        return tr(wr), tr(wi)

    wf_re, wf_im = bsum(pw_re[:t, 0][::-1], pw_im[:t, 0][::-1], 0)
    wb_re, wb_im = bsum(pw_re[:t, 1], pw_im[:t, 1], 1)
    padl = lambda a: jnp.pad(a, ((0, 0), (0, 0), (0, CHUNK - S5_STATE)))
    wtot = jnp.concatenate([padl(wf_re), padl(wf_im), padl(wb_re), padl(wb_im)], axis=-1)

    def vmat(pw_r, pw_i, d):
        vr = pw_r[:, :, None, :] * c_re[d][None] - pw_i[:, :, None, :] * c_im[d][None]
        vi = pw_r[:, :, None, :] * c_im[d][None] + pw_i[:, :, None, :] * c_re[d][None]
        tr = lambda a: a.transpose(1, 3, 0, 2).reshape(g, S5_STATE, t * hh)
        return tr(vr), tr(-vi)

    vf_re, vf_im = vmat(pw_re[1:, 0], pw_im[1:, 0], 0)
    vb_re, vb_im = vmat(pw_re[1:, 1][::-1], pw_im[1:, 1][::-1], 1)
    padr = lambda a: jnp.pad(a, ((0, 0), (0, CHUNK - S5_STATE), (0, 0)))
    vtot = jnp.concatenate([padr(vf_re), padr(vf_im), padr(vb_re), padr(vb_im)], axis=1)
    padv = lambda a: jnp.pad(a, ((0, 0), (0, CHUNK - S5_STATE)))
    at = jnp.concatenate([padv(pw_re[t, 0]), padv(pw_im[t, 0]), padv(pw_re[t, 1]), padv(pw_im[t, 1])], axis=-1)
    return ktot.astype(BF16), wtot.astype(BF16), vtot.astype(BF16), at[:, None, :]


def s5_scan(u, mats):
    b, lp, _ = u.shape
    nchunk = lp // S5_T
    ktot, wtot, vtot, at = mats
    ug = u.astype(BF16).reshape(b, nchunk, S5_T, S5_GROUPS, S5_GROUP).transpose(3, 1, 0, 2, 4)
    ug = ug.reshape(S5_GROUPS, nchunk * b, S5_COLS)
    rows = nchunk * b
    per_g = lambda g: (g, 0, 0)
    y = pl.pallas_call(
        functools.partial(_s5_kernel, nchunk=nchunk, nb=b),
        grid=(S5_GROUPS,),
        in_specs=[pl.BlockSpec((1, rows, S5_COLS), per_g),
                  pl.BlockSpec((1, S5_COLS, S5_COLS), per_g),
                  pl.BlockSpec((1, S5_COLS, 4 * CHUNK), per_g),
                  pl.BlockSpec((1, 4 * CHUNK, S5_COLS), per_g),
                  pl.BlockSpec((1, 1, 4 * CHUNK), per_g)],
        out_specs=pl.BlockSpec((1, rows, S5_COLS), per_g),
        out_shape=jax.ShapeDtypeStruct((S5_GROUPS, rows, S5_COLS), F32),
        scratch_shapes=[pltpu.VMEM((rows, 4 * CHUNK), F32), pltpu.VMEM((rows, 4 * CHUNK), F32)],
        compiler_params=_params("parallel"),
        name="s5_scan",
    )(ug, ktot, wtot, vtot, at)
    y = y.reshape(S5_GROUPS, nchunk, b, S5_T, S5_GROUP).transpose(2, 1, 3, 0, 4)
    return y.reshape(b * lp, S5_GROUPS * S5_GROUP)


def _s5_glu_kernel(y_ref, w_ref, b_ref, o_ref):
    x = y_ref[...]
    y = x * (0.5 * (1.0 + jnp.tanh(math.sqrt(2.0 / math.pi) * (x + 0.044715 * (x * x * x)))))
    z = _dot(y.astype(BF16), w_ref[...]) + b_ref[...]
    o_ref[...] = (y * _sigmoid(z)).astype(o_ref.dtype)


def s5_glu(y, w, bias):
    m, n = y.shape
    tm = _pick(m, (1024, 512, 384, 256, 128))
    return pl.pallas_call(
        _s5_glu_kernel,
        grid=(m // tm,),
        in_specs=[pl.BlockSpec((tm, n), lambda i: (i, 0)),
                  pl.BlockSpec((n, n), lambda i: (0, 0)),
                  pl.BlockSpec((1, n), lambda i: (0, 0))],
        out_specs=pl.BlockSpec((tm, n), lambda i: (i, 0)),
        out_shape=jax.ShapeDtypeStruct((m, n), BF16),
        compiler_params=_params("parallel"),
        name="s5_glu",
    )(y, w, bias.reshape(1, n).astype(F32))


def _head_norm(x, gain):
    xc = x - jnp.mean(x, axis=-1, keepdims=True)
    return xc * lax.rsqrt(jnp.mean(xc * xc, axis=-1, keepdims=True) + NORM_EPS) * gain


def _ret_kernel(q_ref, k_ref, v_ref, g_ref, c_ref, s_ref, lg_ref, gn_ref, o_ref,
                qs_ref, ks_ref, vs_ref, af_ref, ab_ref, st_ref, *, nchunk):
    c = c_ref[...]
    s = s_ref[...]
    ii = lax.broadcasted_iota(jnp.int32, (CHUNK, CHUNK), 0).astype(F32)
    jj = lax.broadcasted_iota(jnp.int32, (CHUNK, CHUNK), 1).astype(F32)
    diff = ii - jj
    lanes = [slice(hh * HEAD_DIM, (hh + 1) * HEAD_DIM) for hh in range(HEADS_PER_STEP)]
    consts = []
    for hh, ln in enumerate(lanes):
        q = q_ref[0, :, ln]
        qs_ref[:, ln] = ((q * c + pltpu.roll(q, HEAD_DIM // 2, 1) * s) * HEAD_DIM ** -0.5).astype(BF16)
        k = k_ref[0, :, ln]
        ks_ref[:, ln] = k * c + pltpu.roll(k, HEAD_DIM // 2, 1) * s
        vs_ref[:, ln] = v_ref[0, :, ln].astype(BF16)
        lgf = lg_ref[hh, 0:1, :]
        lgb = lg_ref[hh, 1:2, :]
        fwd = (jnp.where(diff >= 0, jnp.exp(jnp.where(diff >= 0, diff, 0.0) * lgf), 0.0),
               jnp.exp((CHUNK - 1 - ii) * lgf), jnp.exp((ii + 1.0) * lgf), jnp.exp(CHUNK * lgf))
        bwd = (jnp.where(diff < 0, jnp.exp(jnp.where(diff < 0, -diff, 0.0) * lgb), 0.0),
               jnp.exp(ii * lgb), jnp.exp((CHUNK - ii) * lgb), jnp.exp(CHUNK * lgb))
        consts.append((fwd, bwd))
    st_ref[...] = jnp.zeros_like(st_ref)

    def chunk(r, ln, slot, cst, out_ref):
        dec, zeta, xi, gc = cst
        qc = qs_ref[pl.ds(r, CHUNK), ln]
        kc = ks_ref[pl.ds(r, CHUNK), ln]
        vc = vs_ref[pl.ds(r, CHUNK), ln]
        state = st_ref[slot]
        sc = _dot_nt(qc, kc.astype(BF16)) * dec
        out_ref[pl.ds(r, CHUNK), ln] = _dot(sc.astype(BF16), vc) + _dot(qc, state.astype(BF16)) * xi
        st_ref[slot] = gc * state + _dot_tn((kc * zeta).astype(BF16), vc)

    def body(t, carry):
        rf = pl.multiple_of(t * CHUNK, CHUNK)
        rb = pl.multiple_of((nchunk - 1 - t) * CHUNK, CHUNK)
        for hh, ln in enumerate(lanes):
            chunk(rf, ln, 2 * hh, consts[hh][0], af_ref)
            chunk(rb, ln, 2 * hh + 1, consts[hh][1], ab_ref)
        return carry

    lax.fori_loop(0, nchunk, body, 0)
    for ln in lanes:
        gate = g_ref[0, :, ln]
        y = _head_norm(af_ref[:, ln] + ab_ref[:, ln], gn_ref[:, ln])
        o_ref[0, :, ln] = (y * (gate * _sigmoid(gate))).astype(o_ref.dtype)


def retention(proj3, log_gamma, gain, cos_t, sin_t):
    b, lp, _ = proj3.shape
    hd = HEAD_DIM
    hps = HEADS_PER_STEP
    wd = hps * hd
    nblk = RET_HEADS // hps
    lg = jnp.broadcast_to(log_gamma.T[:, :, None], (RET_HEADS, 2, hd)).astype(F32)
    blk = lambda off: pl.BlockSpec((1, lp, wd), lambda bi, hi: (bi, 0, off + hi))
    full = lambda bi, hi: (0, 0)
    return pl.pallas_call(
        functools.partial(_ret_kernel, nchunk=lp // CHUNK),
        grid=(b, nblk),
        in_specs=[blk(0), blk(nblk), blk(2 * nblk), blk(3 * nblk),
                  pl.BlockSpec((lp, hd), full), pl.BlockSpec((lp, hd), full),
                  pl.BlockSpec((hps, 2, hd), lambda bi, hi: (hi, 0, 0)),
                  pl.BlockSpec((1, wd), lambda bi, hi: (0, hi))],
        out_specs=pl.BlockSpec((1, lp, wd), lambda bi, hi: (bi, 0, hi)),
        out_shape=jax.ShapeDtypeStruct((b, lp, RET_HEADS * hd), BF16),
        scratch_shapes=[pltpu.VMEM((lp, wd), BF16), pltpu.VMEM((lp, wd), F32), pltpu.VMEM((lp, wd), BF16),
                        pltpu.VMEM((lp, wd), F32), pltpu.VMEM((lp, wd), F32),
                        pltpu.VMEM((2 * hps, hd, hd), F32)],
        compiler_params=_params("parallel", "parallel"),
        name="retention",
    )(proj3, proj3, proj3, proj3, cos_t, sin_t, lg, gain.reshape(1, RET_HEADS * hd).astype(F32))


def _mlstm_chunk(qc, kc, vc, li, lf, tri, eye, c_prev, n_prev, m_prev):
    bt_col = jnp.sum(jnp.where(tri, lf, 0.0), axis=1, keepdims=True)
    bt_row = jnp.sum(jnp.where(eye, bt_col, 0.0), axis=0, keepdims=True)
    bt_last = jnp.sum(lf, axis=1, keepdims=True)
    dlog = jnp.where(tri, bt_col - bt_row + li, -jnp.inf)
    a_row = bt_last - bt_row + li
    m_loc = jnp.max(a_row, axis=1, keepdims=True)
    a_col = jnp.sum(jnp.where(eye, a_row, 0.0), axis=1, keepdims=True)
    kw = kc * jnp.exp(a_col - m_loc)
    c_loc = _dot_tn(kw.astype(BF16), vc)
    n_loc = jnp.sum(kw, axis=0, keepdims=True)
    g_col = bt_col + m_prev
    m_t = jnp.maximum(g_col, jnp.max(dlog, axis=1, keepdims=True))
    sc = _dot_nt(qc, kc.astype(BF16)) * jnp.exp(dlog - m_t)
    w_int = jnp.exp(g_col - m_t)
    num = _dot(sc.astype(BF16), vc) + w_int * _dot(qc, c_prev.astype(BF16))
    den = (jnp.sum(sc, axis=1, keepdims=True)
           + w_int * jnp.sum(qc.astype(F32) * n_prev, axis=1, keepdims=True))
    out = num / jnp.maximum(jnp.abs(den), jnp.exp(-m_t))
    m_new = jnp.maximum(bt_last + m_prev, m_loc)
    f_prev = jnp.exp(bt_last + m_prev - m_new)
    f_loc = jnp.exp(m_loc - m_new)
    return out, f_prev * c_prev + f_loc * c_loc, f_prev * n_prev + f_loc * n_loc, m_new


def _mlstm_kernel(mu_ref, mo_ref, gt_ref, gb_ref, cw_ref, cb_ref, wq_ref, wk_ref, wv_ref, gn_ref, o_ref,
                  qs_ref, ks_ref, vs_ref, gl_ref, af_ref, ab_ref, cs_ref, ns_ref, ms_ref, *, lp, pad, nchunk):
    lanes = [slice(hh * HEAD_DIM, (hh + 1) * HEAD_DIM) for hh in range(HEADS_PER_STEP)]
    valid_row = lax.broadcasted_iota(jnp.int32, (lp, 1), 0) >= pad
    valid_col = lax.broadcasted_iota(jnp.int32, (1, lp), 1) >= pad
    for hh, ln in enumerate(lanes):
        mu = mu_ref[0, :, ln]
        conv = cb_ref[:, ln]
        for j in range(CONV_W):
            conv = conv + cw_ref[j:j + 1, ln] * pltpu.roll(mu, (CONV_W // 2 - j) % lp, 0)
        uc = (conv * _sigmoid(conv)).astype(BF16)
        qs_ref[:, ln] = jnp.where(valid_row, _dot(uc, wq_ref[hh]), 0.0).astype(BF16)
        ks_ref[:, ln] = jnp.where(valid_row, _dot(uc, wk_ref[hh]) * HEAD_DIM ** -0.5, 0.0)
        vs_ref[:, ln] = jnp.where(valid_row, _dot(mu.astype(BF16), wv_ref[hh]), 0.0).astype(BF16)

        g = gt_ref[0, hh] + gb_ref[hh][:, 0:1]
        for r in (0, 2):
            gl_ref[hh, r:r + 1, :] = jnp.where(valid_col, g[r:r + 1], NEG_GATE)
            gf = g[r + 1:r + 2]
            log_sig = jnp.minimum(gf, 0.0) - jnp.log(1.0 + jnp.exp(-jnp.abs(gf)))
            gl_ref[hh, r + 1:r + 2, :] = jnp.where(valid_col, log_sig, 0.0)
    cs_ref[...] = jnp.zeros_like(cs_ref)
    ns_ref[...] = jnp.zeros_like(ns_ref)
    ms_ref[...] = jnp.zeros_like(ms_ref)

    ii = lax.broadcasted_iota(jnp.int32, (CHUNK, CHUNK), 0)
    jj = lax.broadcasted_iota(jnp.int32, (CHUNK, CHUNK), 1)
    lower = jj <= ii
    upper = jj >= ii
    eye = jj == ii

    def run(r, hh, ln, row, slot, tri, out_ref):
        out, c_s, n_s, m_s = _mlstm_chunk(
            qs_ref[pl.ds(r, CHUNK), ln], ks_ref[pl.ds(r, CHUNK), ln], vs_ref[pl.ds(r, CHUNK), ln],
            gl_ref[hh, row:row + 1, pl.ds(r, CHUNK)], gl_ref[hh, row + 1:row + 2, pl.ds(r, CHUNK)],
            tri, eye, cs_ref[slot], ns_ref[slot], ms_ref[slot][:, 0:1])
        out_ref[pl.ds(r, CHUNK), ln] = out
        cs_ref[slot] = c_s
        ns_ref[slot] = n_s
        ms_ref[slot] = jnp.broadcast_to(m_s, (1, HEAD_DIM))

    def body(t, carry):
        rf = pl.multiple_of(t * CHUNK, CHUNK)
        rb = pl.multiple_of((nchunk - 1 - t) * CHUNK, CHUNK)
        for hh, ln in enumerate(lanes):
            run(rf, hh, ln, 0, 2 * hh, lower, af_ref)
            run(rb, hh, ln, 2, 2 * hh + 1, upper, ab_ref)
        return carry

    lax.fori_loop(0, nchunk, body, 0)
    for ln in lanes:
        y = _head_norm(af_ref[:, ln] + ab_ref[:, ln], gn_ref[:, ln])
        o_ref[0, :, ln] = (y * _sigmoid(mo_ref[0, :, ln])).astype(o_ref.dtype)


def mlstm(proj3, gates, gate_b, conv_w, conv_b, wq, wk, wv, gain, *, pad, mu_col0, mo_col0):
    b, lp, _ = proj3.shape
    hd = HEAD_DIM
    hps = HEADS_PER_STEP
    wd = hps * hd
    gt = gates.reshape(b, lp, 4, ML_HEADS).transpose(0, 3, 2, 1)
    gb = jnp.broadcast_to(gate_b.T[:, :, None], (ML_HEADS, 4, hd)).astype(F32)
    blk = lambda off: pl.BlockSpec((1, lp, wd), lambda bi, hi: (bi, 0, off // hps + hi))
    per_h = lambda bi, hi: (hi, 0, 0)
    vec = pl.BlockSpec((1, wd), lambda bi, hi: (0, hi))
    sq = pl.BlockSpec((hps, hd, hd), per_h)
    return pl.pallas_call(
        functools.partial(_mlstm_kernel, lp=lp, pad=pad, nchunk=lp // CHUNK),
        grid=(b, ML_HEADS // hps),
        in_specs=[blk(mu_col0), blk(mo_col0),
                  pl.BlockSpec((1, hps, 4, lp), lambda bi, hi: (bi, hi, 0, 0)),
                  pl.BlockSpec((hps, 4, hd), per_h),
                  pl.BlockSpec((CONV_W, wd), lambda bi, hi: (0, hi)),
                  vec, sq, sq, sq, vec],
        out_specs=pl.BlockSpec((1, lp, wd), lambda bi, hi: (bi, 0, hi)),
        out_shape=jax.ShapeDtypeStruct((b, lp, ML_HEADS * hd), BF16),
        scratch_shapes=[pltpu.VMEM((lp, wd), BF16), pltpu.VMEM((lp, wd), F32), pltpu.VMEM((lp, wd), BF16),
                        pltpu.VMEM((hps, 4, lp), F32), pltpu.VMEM((lp, wd), F32), pltpu.VMEM((lp, wd), F32),
                        pltpu.VMEM((2 * hps, hd, hd), F32), pltpu.VMEM((2 * hps, 1, hd), F32),
                        pltpu.VMEM((2 * hps, 1, hd), F32)],
        compiler_params=_params("parallel", "parallel"),
        name="mlstm",
    )(proj3, proj3, gt, gb, conv_w.astype(F32), conv_b.reshape(1, -1).astype(F32),
      wq.astype(BF16), wk.astype(BF16), wv.astype(BF16), gain.reshape(1, -1).astype(F32))


def _rope_freqs(dim):
    return ROPE_THETA ** (-jnp.arange(dim // 2, dtype=F32) / (dim // 2))


def _axial_tables(n_tok, pad):
    rows = n_tok // GRID_W
    row = jnp.concatenate([jnp.zeros((pad,), F32), -jnp.ones((N_META,), F32),
                           jnp.repeat(jnp.arange(rows, dtype=F32), GRID_W)])
    col = jnp.concatenate([jnp.zeros((pad,), F32), jnp.arange(N_META, dtype=F32),
                           jnp.tile(jnp.arange(GRID_W, dtype=F32), rows)])
    f = _rope_freqs(HEAD_DIM // 2)
    ang = jnp.concatenate([row[:, None] * f[None, :]] * 2 + [col[:, None] * f[None, :]] * 2, axis=-1)
    first = (jnp.arange(HEAD_DIM) % 64) < 32
    sin = jnp.sin(ang)
    return jnp.cos(ang), jnp.where(first, -sin, 0.0), jnp.where(first, 0.0, sin)


def _linear_tables(l, pad):
    pos = jnp.concatenate([jnp.zeros((pad,), F32), jnp.arange(l, dtype=F32)])
    ang = pos[:, None] * _rope_freqs(HEAD_DIM)[None, :]
    ang = jnp.concatenate([ang, ang], axis=-1)
    sin = jnp.sin(ang)
    return jnp.cos(ang), jnp.where(jnp.arange(HEAD_DIM) < HEAD_DIM // 2, -sin, sin)


def _even_mixer_parts(hn, w_in_all, j, q_norm, k_norm, s5_params, glu_w, glu_b, tabs, *, b, lp, pad):
    att_w = ATT_HEADS * HEAD_DIM
    u0 = att_w + 2 * ATT_KV_HEADS * HEAD_DIM
    qkvu = matmul_wcast(hn, w_in_all, j, w_in_all.shape[2])
    qkvu3 = qkvu.reshape(b, lp, -1)
    att = attention(qkvu3, q_norm, k_norm, tabs, pad=pad).reshape(b * lp, att_w)
    y = s5_scan(qkvu3[:, :, u0:], _s5_matrices(*s5_params))
    ssm = s5_glu(y, glu_w.astype(BF16), glu_b)
    return [att, ssm]


def _odd_mixer_parts(hn, w_in_all, j, ret_log_decay, ret_norm, conv_w, conv_b, wq, wk, wv, gate_b, ml_norm,
                     tabs, *, b, lp, pad):
    ret_w = RET_HEADS * HEAD_DIM
    ml_w = ML_HEADS * HEAD_DIM
    main = 4 * ret_w + 2 * ml_w
    n_gate = w_in_all.shape[2] - main
    proj = matmul_wcast(hn, w_in_all, j, main)
    proj3 = proj.reshape(b, lp, main)
    w_gate = jnp.pad(w_in_all[j, :, main:].astype(BF16), ((0, 0), (0, CHUNK - n_gate)))
    gates = matmul(hn, w_gate)[:, :n_gate].reshape(b, lp, n_gate)
    log_gamma = -jnp.abs(ret_log_decay.astype(F32))
    ret = retention(proj3, log_gamma, ret_norm, *tabs)
    nblk = ret_w // HEAD_DIM
    hm = mlstm(proj3, gates, gate_b, conv_w, conv_b, wq, wk, wv, ml_norm,
               pad=pad, mu_col0=4 * nblk, mo_col0=4 * nblk + ml_w // HEAD_DIM)
    return [ret.reshape(b * lp, ret_w), hm.reshape(b * lp, ml_w)]


def kernel(x, meta_tokens, norm_gains, mlp_w1, mlp_w2, even_w_in, even_w_out, att_q_norm, att_k_norm, s5_lam_re, s5_lam_im, s5_log_dt, s5_b_re, s5_b_im, s5_c_re, s5_c_im, s5_d, s5_glu_w, s5_glu_b, odd_w_in, odd_w_out, ret_log_decay, ret_norm, ml_conv_w, ml_conv_b, ml_wq, ml_wk, ml_wv, ml_gate_b, ml_norm):
    b, n_tok, d_model = x.shape
    l = n_tok + N_META
    pad = (-l) % CHUNK
    lp = l + pad
    depth = norm_gains.shape[0]
    h = jnp.concatenate([jnp.zeros((b, pad, d_model), x.dtype),
                         jnp.broadcast_to(meta_tokens.astype(x.dtype)[None], (b, N_META, d_model)), x], axis=1)
    h = h.reshape(b * lp, d_model)
    axial = _axial_tables(n_tok, pad)
    linear = _linear_tables(l, pad)
    dims = dict(b=b, lp=lp, pad=pad)
    hn = rmsnorm(h, norm_gains[0, 0])
    for i in range(depth):
        j = i // 2
        if i % 2 == 0:
            s5_params = (s5_lam_re[j], s5_lam_im[j], s5_log_dt[j], s5_b_re[j], s5_b_im[j], s5_c_re[j],
                         s5_c_im[j], s5_d[j])
            parts = _even_mixer_parts(hn, even_w_in, j, att_q_norm[j], att_k_norm[j],
                                      s5_params, s5_glu_w[j], s5_glu_b[j], axial, **dims)
            w_out = even_w_out[j]
        else:
            parts = _odd_mixer_parts(hn, odd_w_in, j, ret_log_decay[j], ret_norm[j],
                                     ml_conv_w[j], ml_conv_b[j], ml_wq[j], ml_wk[j], ml_wv[j], ml_gate_b[j],
                                     ml_norm[j], linear, **dims)
            w_out = odd_w_out[j]
        h, hn = matmul_norm_res(parts, w_out.astype(BF16), norm_gains[i, 1], h, norm_gains[i, 2], lp=lp, pad=pad)
        hid = matmul_wcast(hn, mlp_w1, i, mlp_w1.shape[2], relu2=True, out_dtype=BF16)
        next_gain = norm_gains[i + 1, 0] if i + 1 < depth else None
        h, hn = matmul_norm_res([hid], mlp_w2[i].astype(BF16), norm_gains[i, 3], h, next_gain, lp=lp, pad=pad)
    return h.reshape(b, lp, d_model)[:, pad + N_META:]
```

```python
import functools
import math

import jax
import jax.numpy as jnp
from jax import lax
from jax.experimental import pallas as pl
from jax.experimental.pallas import tpu as pltpu

F32 = jnp.float32
BF16 = jnp.bfloat16

N_META = 16
GRID_W = 64
CHUNK = 128
HEAD_DIM = 128
NORM_EPS = 1e-6
ROPE_THETA = 10000.0
ATT_HEADS = 12
ATT_KV_HEADS = 4
ATT_GROUP = ATT_HEADS // ATT_KV_HEADS
S5_GROUP = 16
S5_GROUPS = 32
S5_STATE = 64
S5_T = 16
S5_COLS = S5_T * S5_GROUP
RET_HEADS = 8
ML_HEADS = 8
CONV_W = 5
NEG_GATE = -1e4
HEADS_PER_STEP = 2
BF16_ROWS = 16
VMEM_LIMIT_BYTES = 56 * 1024 * 1024


def _pick(n, cands):
    for c in cands:
        if n % c == 0:
            return c
    raise ValueError(f"no tile for {n} in {cands}")


def _params(*sem):
    return pltpu.CompilerParams(dimension_semantics=sem, vmem_limit_bytes=VMEM_LIMIT_BYTES)


def _dot(a, b):
    return jnp.dot(a, b, preferred_element_type=F32)


def _dot_nt(a, b):
    return lax.dot_general(a, b, (((1,), (1,)), ((), ())), preferred_element_type=F32)


def _dot_tn(a, b):
    return lax.dot_general(a, b, (((0,), (0,)), ((), ())), preferred_element_type=F32)


def _sigmoid(x):
    return 1.0 / (1.0 + jnp.exp(-x))


def _rmsnorm_kernel(x_ref, g_ref, o_ref):
    x = x_ref[...]
    ms = jnp.mean(x * x, axis=-1, keepdims=True)
    o_ref[...] = (x * lax.rsqrt(ms + NORM_EPS) * g_ref[...]).astype(o_ref.dtype)


def rmsnorm(x, gain):
    m, d = x.shape
    tm = _pick(m, (1024, 512, 384, 256, 128))
    return pl.pallas_call(
        _rmsnorm_kernel,
        grid=(m // tm,),
        in_specs=[pl.BlockSpec((tm, d), lambda i: (i, 0)), pl.BlockSpec((1, d), lambda i: (0, 0))],
        out_specs=pl.BlockSpec((tm, d), lambda i: (i, 0)),
        out_shape=jax.ShapeDtypeStruct((m, d), BF16),
        compiler_params=_params("parallel"),
        name="rmsnorm",
    )(x, gain.reshape(1, d).astype(F32))


def _matmul_kernel(a_ref, w_ref, o_ref, *, relu2):
    y = _dot(a_ref[...], w_ref[...])
    if relu2:
        y = jnp.square(jnp.maximum(y, 0.0))
    o_ref[...] = y.astype(o_ref.dtype)


def matmul(a, w, *, relu2=False, out_dtype=F32):
    m, k = a.shape
    n = w.shape[1]
    tm = _pick(m, (1024, 512, 384, 256, 128))
    tn = _pick(n, (2048, 1536, 1280, 1024, 512, 256, 128))
    return pl.pallas_call(
        functools.partial(_matmul_kernel, relu2=relu2),
        grid=(m // tm, n // tn),
        in_specs=[pl.BlockSpec((tm, k), lambda i, j: (i, 0)),
                  pl.BlockSpec((k, tn), lambda i, j: (0, j))],
        out_specs=pl.BlockSpec((tm, tn), lambda i, j: (i, j)),
        out_shape=jax.ShapeDtypeStruct((m, n), out_dtype),
        compiler_params=_params("parallel", "parallel"),
        name="matmul",
    )(a, w)


def _matmul_wcast_kernel(a_ref, w_ref, o_ref, wb_ref, *, relu2):
    @pl.when(pl.program_id(1) == 0)
    def _():
        wb_ref[...] = w_ref[...].astype(BF16)

    y = _dot(a_ref[...], wb_ref[...])
    if relu2:
        y = jnp.square(jnp.maximum(y, 0.0))
    o_ref[...] = y.astype(o_ref.dtype)


def matmul_wcast(a, w3, layer, n_cols, *, relu2=False, out_dtype=F32):
    m, k = a.shape
    tm = _pick(m, (1024, 512, 384, 256, 128))
    tn = _pick(n_cols, (1024, 512, 256, 128))
    return pl.pallas_call(
        functools.partial(_matmul_wcast_kernel, relu2=relu2),
        grid=(n_cols // tn, m // tm),
        in_specs=[pl.BlockSpec((tm, k), lambda j, i: (i, 0)),
                  pl.BlockSpec((None, k, tn), lambda j, i: (layer, 0, j))],
        out_specs=pl.BlockSpec((tm, tn), lambda j, i: (i, j)),
        out_shape=jax.ShapeDtypeStruct((m, n_cols), out_dtype),
        scratch_shapes=[pltpu.VMEM((k, tn), BF16)],
        compiler_params=_params("parallel", "arbitrary"),
        name="matmul_wcast",
    )(a, w3)


def _matmul_norm_res_kernel(*refs, widths, nk, tm, lp, pad, emit_next):
    na = len(widths)
    a_refs = refs[:na]
    w_ref, g_ref, h_ref = refs[na:na + 3]
    rest = refs[na + 3:]
    if emit_next:
        g2_ref, o_ref, n_ref = rest
    else:
        o_ref, = rest
    start = pl.program_id(0) * tm
    row = start + lax.broadcasted_iota(jnp.int32, (tm, 1), 0)
    rel0 = row - (start // lp) * lp
    rel1 = row - ((start + tm - 1) // lp) * lp
    is_pad = ((rel0 >= 0) & (rel0 < pad)) | ((rel1 >= 0) & (rel1 < pad))

    def finish(y):
        ms = jnp.mean(y * y, axis=-1, keepdims=True)
        out = jnp.where(is_pad, 0.0, h_ref[...] + y * lax.rsqrt(ms + NORM_EPS) * g_ref[...])
        o_ref[...] = out
        if emit_next:
            ms2 = jnp.mean(out * out, axis=-1, keepdims=True)
            n_ref[...] = (out * lax.rsqrt(ms2 + NORM_EPS) * g2_ref[...]).astype(n_ref.dtype)

    if nk == 1:
        part = None
        off = 0
        for a_ref, wd in zip(a_refs, widths):
            d = _dot(a_ref[...], w_ref[off:off + wd, :])
            part = d if part is None else part + d
            off += wd
        finish(part)
    else:
        kk = pl.program_id(1)

        @pl.when(kk == 0)
        def _():
            o_ref[...] = jnp.zeros_like(o_ref)

        o_ref[...] += _dot(a_refs[0][...], w_ref[...])

        @pl.when(kk == nk - 1)
        def _():
            finish(o_ref[...])


def matmul_norm_res(parts, w, gain, h, next_gain, *, lp, pad):
    m, n = h.shape
    widths = tuple(p.shape[1] for p in parts)
    k = sum(widths)
    tm = _pick(m, (512, 384, 256, 128))
    assert tm <= lp
    if len(parts) > 1 or k <= 2048:
        tk, nk = k, 1
    else:
        tk = 2048
        nk = k // tk
        widths = (tk,)
    emit_next = next_gain is not None
    row_blk = pl.BlockSpec((tm, n), lambda i, j: (i, 0))
    vec = pl.BlockSpec((1, n), lambda i, j: (0, 0))
    in_specs = [pl.BlockSpec((tm, wd), lambda i, j: (i, j)) for wd in widths]
    in_specs += [pl.BlockSpec((tk, n), lambda i, j: (j, 0)), vec, row_blk]
    args = [*parts, w, gain.reshape(1, n).astype(F32), h]
    out_specs = [row_blk]
    out_shape = [jax.ShapeDtypeStruct((m, n), F32)]
    if emit_next:
        in_specs.append(vec)
        args.append(next_gain.reshape(1, n).astype(F32))
        out_specs.append(row_blk)
        out_shape.append(jax.ShapeDtypeStruct((m, n), BF16))
    res = pl.pallas_call(
        functools.partial(_matmul_norm_res_kernel, widths=widths, nk=nk, tm=tm, lp=lp, pad=pad,
                          emit_next=emit_next),
        grid=(m // tm, nk),
        in_specs=in_specs,
        out_specs=out_specs,
        out_shape=out_shape,
        compiler_params=_params("parallel", "arbitrary"),
        name="matmul_norm_res",
    )(*args)
    return (res[0], res[1]) if emit_next else (res[0], None)


def _rope_axial(x, c, sa, sb):
    return x * c + pltpu.roll(x, HEAD_DIM - 32, 1) * sa + pltpu.roll(x, 32, 1) * sb


def _attn_kernel(q_ref, k_ref, v_ref, qg_ref, kg_ref, c_ref, sa_ref, sb_ref, o_ref, ks_ref, vt_ref, p_ref,
                 *, pad, tq):
    qi = pl.program_id(2)

    @pl.when(qi == 0)
    def _():
        k = k_ref[0]
        k = k * lax.rsqrt(jnp.mean(k * k, axis=-1, keepdims=True) + NORM_EPS) * kg_ref[...]
        ks_ref[...] = _rope_axial(k, c_ref[...], sa_ref[...], sb_ref[...]).astype(BF16)
        vt_ref[...] = v_ref[0].T.astype(BF16)
        p_ref[0:pad, :] = jnp.zeros((pad, p_ref.shape[1]), BF16)

    r0 = pl.multiple_of(qi * tq, tq)
    c = c_ref[pl.ds(r0, tq), :]
    sa = sa_ref[pl.ds(r0, tq), :]
    sb = sb_ref[pl.ds(r0, tq), :]
    scale = HEAD_DIM ** -0.5 * math.log2(math.e)
    qs = []
    for g in range(ATT_GROUP):
        q = q_ref[0, :, g * HEAD_DIM:(g + 1) * HEAD_DIM]
        q = q * lax.rsqrt(jnp.mean(q * q, axis=-1, keepdims=True) + NORM_EPS) * qg_ref[...]
        qs.append((_rope_axial(q, c, sa, sb) * scale).astype(BF16))
    qall = jnp.concatenate(qs, axis=0)
    st = _dot_nt(ks_ref[pad:, :], qall)
    p = jnp.exp2(st - jnp.max(st, axis=0, keepdims=True))
    l = jnp.sum(p, axis=0, keepdims=True)
    p_ref[pad:, :] = p.astype(BF16)
    ot = _dot(vt_ref[...], p_ref[...]) / l
    o = ot.T
    for g in range(ATT_GROUP):
        o_ref[0, :, g * HEAD_DIM:(g + 1) * HEAD_DIM] = o[g * tq:(g + 1) * tq].astype(o_ref.dtype)


def attention(qkvu3, q_gain, k_gain, tabs, *, pad):
    b, lp, _ = qkvu3.shape
    tq = CHUNK
    gw = ATT_GROUP * HEAD_DIM
    k_col0 = ATT_HEADS
    v_col0 = ATT_HEADS + ATT_KV_HEADS
    full = lambda bi, hi, qi: (0, 0)
    return pl.pallas_call(
        functools.partial(_attn_kernel, pad=pad, tq=tq),
        grid=(b, ATT_KV_HEADS, lp // tq),
        in_specs=[pl.BlockSpec((1, tq, gw), lambda bi, hi, qi: (bi, qi, hi)),
                  pl.BlockSpec((1, lp, HEAD_DIM), lambda bi, hi, qi: (bi, 0, k_col0 + hi)),
                  pl.BlockSpec((1, lp, HEAD_DIM), lambda bi, hi, qi: (bi, 0, v_col0 + hi)),
                  pl.BlockSpec((1, HEAD_DIM), full),
                  pl.BlockSpec((1, HEAD_DIM), full),
                  pl.BlockSpec((lp, HEAD_DIM), full),
                  pl.BlockSpec((lp, HEAD_DIM), full),
                  pl.BlockSpec((lp, HEAD_DIM), full)],
        out_specs=pl.BlockSpec((1, tq, gw), lambda bi, hi, qi: (bi, qi, hi)),
        out_shape=jax.ShapeDtypeStruct((b, lp, ATT_HEADS * HEAD_DIM), BF16),
        scratch_shapes=[pltpu.VMEM((lp, HEAD_DIM), BF16), pltpu.VMEM((HEAD_DIM, lp), BF16),
                        pltpu.VMEM((lp, ATT_GROUP * tq), BF16)],
        compiler_params=_params("parallel", "parallel", "arbitrary"),
        name="attention",
    )(qkvu3, qkvu3, qkvu3, q_gain.reshape(1, HEAD_DIM).astype(F32), k_gain.reshape(1, HEAD_DIM).astype(F32),
      *tabs)


def _s5_kernel(u_ref, k_ref, w_ref, v_ref, at_ref, y_ref, s_ref, x_ref, *, nchunk, nb):
    u = u_ref[0]
    s_ref[...] = _dot(u, w_ref[0])
    at = at_ref[0]
    afr, afi, abr, abi = (at[:, i * CHUNK:(i + 1) * CHUNK] for i in range(4))

    def body(c, carry):
        xfr, xfi, xbr, xbi = carry
        rf = pl.multiple_of(c * nb, nb)
        rb = pl.multiple_of((nchunk - 1 - c) * nb, nb)
        x_ref[pl.ds(rf, nb), 0:CHUNK] = xfr
        x_ref[pl.ds(rf, nb), CHUNK:2 * CHUNK] = xfi
        x_ref[pl.ds(rb, nb), 2 * CHUNK:3 * CHUNK] = xbr
        x_ref[pl.ds(rb, nb), 3 * CHUNK:4 * CHUNK] = xbi
        sfr = s_ref[pl.ds(rf, nb), 0:CHUNK]
        sfi = s_ref[pl.ds(rf, nb), CHUNK:2 * CHUNK]
        sbr = s_ref[pl.ds(rb, nb), 2 * CHUNK:3 * CHUNK]
        sbi = s_ref[pl.ds(rb, nb), 3 * CHUNK:4 * CHUNK]
        return (afr * xfr - afi * xfi + sfr, afr * xfi + afi * xfr + sfi,
                abr * xbr - abi * xbi + sbr, abr * xbi + abi * xbr + sbi)

    z = jnp.zeros((nb, CHUNK), F32)
    lax.fori_loop(0, nchunk, body, (z, z, z, z))
    y_ref[0] = _dot(u, k_ref[0]) + _dot(x_ref[...].astype(BF16), v_ref[0])


def _s5_matrices(lam_re, lam_im, log_dt, b_re, b_im, c_re, c_im, d_skip):
    hi = lax.Precision.HIGHEST
    t = S5_T
    lr = jnp.minimum(lam_re, -1e-4)
    li = lam_im
    dt = jnp.exp(log_dt)[..., None]
    er = jnp.exp(lr * dt)
    abar_re = er * jnp.cos(li * dt)
    abar_im = er * jnp.sin(li * dt)
    nr = abar_re - 1.0
    den = lr * lr + li * li
    coef_re = (nr * lr + abar_im * li) / den
    coef_im = (abar_im * lr - nr * li) / den
    bb_re = coef_re[..., None] * b_re - coef_im[..., None] * b_im
    bb_im = coef_re[..., None] * b_im + coef_im[..., None] * b_re
    kk = jnp.arange(t + 1, dtype=F32)[:, None, None, None]
    mag = jnp.exp(kk * (lr * dt)[None])
    pw_re = mag * jnp.cos(kk * (li * dt)[None])
    pw_im = mag * jnp.sin(kk * (li * dt)[None])
    g, hh = d_skip.shape
    ct_re = c_re.transpose(0, 1, 3, 2)
    ct_im = c_im.transpose(0, 1, 3, 2)
    flat = lambda a: a.reshape(2, g, S5_STATE, hh * hh)
    bc_re = flat(bb_re[..., :, None] * ct_re[..., None, :] - bb_im[..., :, None] * ct_im[..., None, :])
    bc_im = flat(bb_re[..., :, None] * ct_im[..., None, :] + bb_im[..., :, None] * ct_re[..., None, :])
    lagk = (jnp.einsum('kdgp,dgpn->dgkn', pw_re, bc_re, precision=hi)
            - jnp.einsum('kdgp,dgpn->dgkn', pw_im, bc_im, precision=hi))
    ti = jnp.arange(t)
    lag = ti[None, :] - ti[:, None]
    sel = lambda m: m[None, :, :, None]
    skip = (jnp.eye(hh, dtype=F32)[None] * d_skip[:, None, :]).reshape(g, 1, 1, hh * hh)
    ktot = (jnp.where(sel(lag >= 0), lagk[0][:, jnp.clip(lag, 0, t)], 0.0)
            + jnp.where(sel(lag <= 0), lagk[1][:, jnp.clip(-lag, 0, t)], 0.0)
            + jnp.where(sel(lag == 0), skip, 0.0))
    ktot = ktot.reshape(g, t, t, hh, hh).transpose(0, 1, 3, 2, 4).reshape(g, t * hh, t * hh)

    def bsum(pw_r, pw_i, d):
        wr = pw_r[:, :, :, None] * bb_re[d][None] - pw_i[:, :, :, None] * bb_im[d][None]
        wi = pw_r[:, :, :, None] * bb_im[d][None] + pw_i[:, :, :, None] * bb_re[d][None]
        tr = lambda a: a.transpose(1, 0, 3, 2).reshape(g, t * hh, S5_STATE)
        return tr(wr), tr(wi)

    wf_re, wf_im = bsum(pw_re[:t, 0][::-1], pw_im[:t, 0][::-1], 0)
    wb_re, wb_im = bsum(pw_re[:t, 1], pw_im[:t, 1], 1)
    padl = lambda a: jnp.pad(a, ((0, 0), (0, 0), (0, CHUNK - S5_STATE)))
    wtot = jnp.concatenate([padl(wf_re), padl(wf_im), padl(wb_re), padl(wb_im)], axis=-1)

    def vmat(pw_r, pw_i, d):
        vr = pw_r[:, :, None, :] * c_re[d][None] - pw_i[:, :, None, :] * c_im[d][None]
        vi = pw_r[:, :, None, :] * c_im[d][None] + pw_i[:, :, None, :] * c_re[d][None]
        tr = lambda a: a.transpose(1, 3, 0, 2).reshape(g, S5_STATE, t * hh)
        return tr(vr), tr(-vi)

    vf_re, vf_im = vmat(pw_re[1:, 0], pw_im[1:, 0], 0)
    vb_re, vb_im = vmat(pw_re[1:, 1][::-1], pw_im[1:, 1][::-1], 1)
    padr = lambda a: jnp.pad(a, ((0, 0), (0, CHUNK - S5_STATE), (0, 0)))
    vtot = jnp.concatenate([padr(vf_re), padr(vf_im), padr(vb_re), padr(vb_im)], axis=1)
    padv = lambda a: jnp.pad(a, ((0, 0), (0, CHUNK - S5_STATE)))
    at = jnp.concatenate([padv(pw_re[t, 0]), padv(pw_im[t, 0]), padv(pw_re[t, 1]), padv(pw_im[t, 1])], axis=-1)
    return ktot.astype(BF16), wtot.astype(BF16), vtot.astype(BF16), at[:, None, :]


def s5_scan(u, mats):
    b, lp, _ = u.shape
    nchunk = lp // S5_T
    ktot, wtot, vtot, at = mats
    ug = u.astype(BF16).reshape(b, nchunk, S5_T, S5_GROUPS, S5_GROUP).transpose(3, 1, 0, 2, 4)
    ug = ug.reshape(S5_GROUPS, nchunk * b, S5_COLS)
    rows = nchunk * b
    per_g = lambda g: (g, 0, 0)
    y = pl.pallas_call(
        functools.partial(_s5_kernel, nchunk=nchunk, nb=b),
        grid=(S5_GROUPS,),
        in_specs=[pl.BlockSpec((1, rows, S5_COLS), per_g),
                  pl.BlockSpec((1, S5_COLS, S5_COLS), per_g),
                  pl.BlockSpec((1, S5_COLS, 4 * CHUNK), per_g),
                  pl.BlockSpec((1, 4 * CHUNK, S5_COLS), per_g),
                  pl.BlockSpec((1, 1, 4 * CHUNK), per_g)],
        out_specs=pl.BlockSpec((1, rows, S5_COLS), per_g),
        out_shape=jax.ShapeDtypeStruct((S5_GROUPS, rows, S5_COLS), F32),
        scratch_shapes=[pltpu.VMEM((rows, 4 * CHUNK), F32), pltpu.VMEM((rows, 4 * CHUNK), F32)],
        compiler_params=_params("parallel"),
        name="s5_scan",
    )(ug, ktot, wtot, vtot, at)
    y = y.reshape(S5_GROUPS, nchunk, b, S5_T, S5_GROUP).transpose(2, 1, 3, 0, 4)
    return y.reshape(b * lp, S5_GROUPS * S5_GROUP)


def _s5_glu_kernel(y_ref, w_ref, b_ref, o_ref):
    x = y_ref[...]
    y = x * (0.5 * (1.0 + jnp.tanh(math.sqrt(2.0 / math.pi) * (x + 0.044715 * (x * x * x)))))
    z = _dot(y.astype(BF16), w_ref[...]) + b_ref[...]
    o_ref[...] = (y * _sigmoid(z)).astype(o_ref.dtype)


def s5_glu(y, w, bias):
    m, n = y.shape
    tm = _pick(m, (1024, 512, 384, 256, 128))
    return pl.pallas_call(
        _s5_glu_kernel,
        grid=(m // tm,),
        in_specs=[pl.BlockSpec((tm, n), lambda i: (i, 0)),
                  pl.BlockSpec((n, n), lambda i: (0, 0)),
                  pl.BlockSpec((1, n), lambda i: (0, 0))],
        out_specs=pl.BlockSpec((tm, n), lambda i: (i, 0)),
        out_shape=jax.ShapeDtypeStruct((m, n), BF16),
        compiler_params=_params("parallel"),
        name="s5_glu",
    )(y, w, bias.reshape(1, n).astype(F32))


def _head_norm(x, gain):
    xc = x - jnp.mean(x, axis=-1, keepdims=True)
    return xc * lax.rsqrt(jnp.mean(xc * xc, axis=-1, keepdims=True) + NORM_EPS) * gain


def _ret_kernel(q_ref, k_ref, v_ref, g_ref, c_ref, s_ref, lg_ref, gn_ref, o_ref,
                qs_ref, ks_ref, vs_ref, vt_ref, af_ref, ab_ref, st_ref, *, nchunk):
    c = c_ref[...]
    s = s_ref[...]
    ii = lax.broadcasted_iota(jnp.int32, (CHUNK, CHUNK), 0).astype(F32)
    jj = lax.broadcasted_iota(jnp.int32, (CHUNK, CHUNK), 1).astype(F32)
    lane = (lax.broadcasted_iota(jnp.int32, (1, nchunk * CHUNK), 1) & (CHUNK - 1)).astype(F32)
    diff = ii - jj
    lanes = [slice(hh * HEAD_DIM, (hh + 1) * HEAD_DIM) for hh in range(HEADS_PER_STEP)]
    consts = []
    for hh, ln in enumerate(lanes):
        q = q_ref[0, :, ln]
        qs_ref[:, ln] = ((q * c + pltpu.roll(q, HEAD_DIM // 2, 1) * s) * HEAD_DIM ** -0.5).astype(BF16)
        k = k_ref[0, :, ln]
        ks_ref[:, ln] = (k * c + pltpu.roll(k, HEAD_DIM // 2, 1) * s).astype(BF16)
        v = v_ref[0, :, ln]
        vs_ref[:, ln] = v.astype(BF16)
        v_t = v.T
        lgf = lg_ref[hh, 0:1, :]
        lgb = lg_ref[hh, 1:2, :]
        vt_ref[2 * hh] = (v_t * jnp.exp((CHUNK - 1 - lane) * lgf[:, 0:1])).astype(BF16)
        vt_ref[2 * hh + 1] = (v_t * jnp.exp(lane * lgb[:, 0:1])).astype(BF16)
        fwd = (jnp.where(diff >= 0, jnp.exp(jnp.where(diff >= 0, diff, 0.0) * lgf), 0.0),
               jnp.exp((ii + 1.0) * lgf), jnp.exp(CHUNK * lgf))
        bwd = (jnp.where(diff < 0, jnp.exp(jnp.where(diff < 0, -diff, 0.0) * lgb), 0.0),
               jnp.exp((CHUNK - ii) * lgb), jnp.exp(CHUNK * lgb))
        consts.append((fwd, bwd))
    st_ref[...] = jnp.zeros_like(st_ref)

    def chunk(r, ln, slot, cst, out_ref):
        dec, xi, gc = cst
        qc = qs_ref[pl.ds(r, CHUNK), ln]
        kc = ks_ref[pl.ds(r, CHUNK), ln]
        vc = vs_ref[pl.ds(r, CHUNK), ln]
        state_t = st_ref[slot]
        both = _dot_nt(qc, jnp.concatenate([kc, state_t.astype(BF16)], axis=0))
        sc = both[:, :CHUNK] * dec
        res = _dot(jnp.concatenate([sc.astype(BF16), vt_ref[slot, :, pl.ds(r, CHUNK)]], axis=0),
                   jnp.concatenate([vc, kc], axis=1))
        out_ref[pl.ds(r, CHUNK), ln] = res[:CHUNK, :HEAD_DIM] + both[:, CHUNK:] * xi
        st_ref[slot] = gc * state_t + res[CHUNK:, HEAD_DIM:]

    def body(t, carry):
        rf = pl.multiple_of(t * CHUNK, CHUNK)
        rb = pl.multiple_of((nchunk - 1 - t) * CHUNK, CHUNK)
        for hh, ln in enumerate(lanes):
            chunk(rf, ln, 2 * hh, consts[hh][0], af_ref)
            chunk(rb, ln, 2 * hh + 1, consts[hh][1], ab_ref)
        return carry

    lax.fori_loop(0, nchunk, body, 0)
    for ln in lanes:
        gate = g_ref[0, :, ln]
        y = _head_norm(af_ref[:, ln] + ab_ref[:, ln], gn_ref[:, ln])
        o_ref[0, :, ln] = (y * (gate * _sigmoid(gate))).astype(o_ref.dtype)


def retention(proj3, log_gamma, gain, cos_t, sin_t):
    b, lp, _ = proj3.shape
    hd = HEAD_DIM
    hps = HEADS_PER_STEP
    wd = hps * hd
    nblk = RET_HEADS // hps
    lg = jnp.broadcast_to(log_gamma.T[:, :, None], (RET_HEADS, 2, hd)).astype(F32)
    blk = lambda off: pl.BlockSpec((1, lp, wd), lambda bi, hi: (bi, 0, off + hi))
    full = lambda bi, hi: (0, 0)
    return pl.pallas_call(
        functools.partial(_ret_kernel, nchunk=lp // CHUNK),
        grid=(b, nblk),
        in_specs=[blk(0), blk(nblk), blk(2 * nblk), blk(3 * nblk),
                  pl.BlockSpec((lp, hd), full), pl.BlockSpec((lp, hd), full),
                  pl.BlockSpec((hps, 2, hd), lambda bi, hi: (hi, 0, 0)),
                  pl.BlockSpec((1, wd), lambda bi, hi: (0, hi))],
        out_specs=pl.BlockSpec((1, lp, wd), lambda bi, hi: (bi, 0, hi)),
        out_shape=jax.ShapeDtypeStruct((b, lp, RET_HEADS * hd), BF16),
        scratch_shapes=[pltpu.VMEM((lp, wd), BF16), pltpu.VMEM((lp, wd), BF16), pltpu.VMEM((lp, wd), BF16),
                        pltpu.VMEM((2 * hps, hd, lp), BF16), pltpu.VMEM((lp, wd), F32), pltpu.VMEM((lp, wd), F32),
                        pltpu.VMEM((2 * hps, hd, hd), F32)],
        compiler_params=_params("parallel", "parallel"),
        name="retention",
    )(proj3, proj3, proj3, proj3, cos_t, sin_t, lg, gain.reshape(1, RET_HEADS * hd).astype(F32))


def _mlstm_gate_tables(li, lf, tri_sum):
    bt = jnp.dot(lf, tri_sum, preferred_element_type=F32, precision=lax.Precision.HIGHEST)
    bt_last = jnp.sum(lf, axis=1, keepdims=True)
    a = bt_last - bt + li
    m_loc = jnp.max(a, axis=1, keepdims=True)
    return bt, jnp.exp(a - m_loc), m_loc, bt_last, li - bt


def _mlstm_chunk_t(kc, qtc, vtc, vwc, bt_row, w_row, m_loc, bt_last, colb, mask_t, ct_prev, n_prev, m_prev):
    d, c, pk = HEAD_DIM, CHUNK, BF16_ROWS
    dlog_t = jnp.where(mask_t, bt_row + colb, -jnp.inf)
    g_row = bt_row + m_prev
    m_t = jnp.maximum(g_row, jnp.max(dlog_t, axis=0, keepdims=True))
    r1 = _dot(jnp.concatenate([kc, ct_prev.astype(BF16), jnp.broadcast_to(n_prev, (pk, d)).astype(BF16)], axis=0),
              qtc)
    s_t = r1[:c] * jnp.exp(dlog_t - m_t)
    w_int = jnp.exp(g_row - m_t)
    r2 = _dot(jnp.concatenate([vtc, vwc, jnp.broadcast_to(w_row, (pk, c)).astype(BF16)], axis=0),
              jnp.concatenate([s_t.astype(BF16), kc], axis=1))
    num_t = r2[:d, :c] + w_int * r1[c:c + d]
    den_t = jnp.sum(s_t, axis=0, keepdims=True) + w_int * r1[c + d:c + d + 1]
    out_t = num_t / jnp.maximum(jnp.abs(den_t), jnp.exp(-m_t))
    m_new = jnp.maximum(bt_last + m_prev, m_loc)
    f_prev = jnp.exp(bt_last + m_prev - m_new)
    f_loc = jnp.exp(m_loc - m_new)
    c_new = f_prev * ct_prev + f_loc * r2[d:2 * d, c:]
    n_new = f_prev * n_prev + f_loc * r2[2 * d:2 * d + 1, c:]
    return out_t, c_new, n_new, m_new


def _mlstm_kernel(mu_ref, mo_ref, gt_ref, gb_ref, cw_ref, cb_ref, wk_ref, wqt_ref, wvt_ref, gn_ref, o_ref,
                  ks_ref, qt_ref, vt_ref, vw_ref, row_ref, d_ref, colb_ref, aft_ref, abt_ref,
                  cs_ref, ns_ref, ms_ref, *, lp, pad, nchunk):
    hd = HEAD_DIM
    lanes = [slice(hh * hd, (hh + 1) * hd) for hh in range(HEADS_PER_STEP)]
    valid_row = lax.broadcasted_iota(jnp.int32, (lp, 1), 0) >= pad
    valid_col = lax.broadcasted_iota(jnp.int32, (1, lp), 1) >= pad
    pos = (lax.broadcasted_iota(jnp.int32, (nchunk, CHUNK), 0) * CHUNK
           + lax.broadcasted_iota(jnp.int32, (nchunk, CHUNK), 1))
    valid_pos = pos >= pad
    ii = lax.broadcasted_iota(jnp.int32, (CHUNK, CHUNK), 0)
    jj = lax.broadcasted_iota(jnp.int32, (CHUNK, CHUNK), 1)
    eye = ii == jj
    upper = ii <= jj
    lower = ii >= jj
    ones = jnp.ones((CHUNK, CHUNK), BF16)
    for hh, ln in enumerate(lanes):
        mu = mu_ref[0, :, ln]
        conv = cb_ref[:, ln]
        for j in range(CONV_W):
            conv = conv + cw_ref[j:j + 1, ln] * pltpu.roll(mu, (CONV_W // 2 - j) % lp, 0)
        uc = (conv * _sigmoid(conv)).astype(BF16)
        ks_ref[:, ln] = jnp.where(valid_row, _dot(uc, wk_ref[hh]) * hd ** -0.5, 0.0).astype(BF16)
        qt_ref[ln, :] = jnp.where(valid_col, _dot_nt(wqt_ref[hh], uc), 0.0).astype(BF16)
        v_t = jnp.where(valid_col, _dot_nt(wvt_ref[hh], mu.astype(BF16)), 0.0)
        vt_ref[ln, :] = v_t.astype(BF16)

        for d, tri_sum in enumerate((upper, lower)):
            g_i = gt_ref[0, hh, 2 * d] + gb_ref[hh, 2 * d:2 * d + 1, 0:1]
            g_f = gt_ref[0, hh, 2 * d + 1] + gb_ref[hh, 2 * d + 1:2 * d + 2, 0:1]
            li = jnp.where(valid_pos, g_i, NEG_GATE)
            lf = jnp.where(valid_pos, jnp.minimum(g_f, 0.0) - jnp.log(1.0 + jnp.exp(-jnp.abs(g_f))), 0.0)
            bt, w, m_loc, bt_last, colv = _mlstm_gate_tables(li, lf, jnp.where(tri_sum, 1.0, 0.0))
            slot = 2 * hh + d
            row_ref[4 * slot + 0] = bt
            row_ref[4 * slot + 1] = w
            row_ref[4 * slot + 2] = jnp.broadcast_to(m_loc, (nchunk, CHUNK))
            row_ref[4 * slot + 3] = jnp.broadcast_to(bt_last, (nchunk, CHUNK))
            for n in range(nchunk):
                cols = slice(n * CHUNK, (n + 1) * CHUNK)
                vw_ref[slot, :, cols] = (v_t[:, cols] * w[n:n + 1, :]).astype(BF16)
                d_ref[cols, :] = jnp.where(eye, colv[n:n + 1, :], 0.0)
            diag = d_ref[...]
            d_hi = diag.astype(BF16)
            d_lo = (diag - d_hi.astype(F32)).astype(BF16)
            colb_ref[slot] = _dot(d_hi, ones) + _dot(d_lo, ones)
    cs_ref[...] = jnp.zeros_like(cs_ref)
    ns_ref[...] = jnp.zeros_like(ns_ref)
    ms_ref[...] = jnp.zeros_like(ms_ref)

    def run(n, r, ln, slot, mask_t, out_ref):
        row = lambda kind: row_ref[4 * slot + kind, pl.ds(n, 1), :]
        out_t, c_s, n_s, m_s = _mlstm_chunk_t(
            ks_ref[pl.ds(r, CHUNK), ln], qt_ref[ln, pl.ds(r, CHUNK)], vt_ref[ln, pl.ds(r, CHUNK)],
            vw_ref[slot, :, pl.ds(r, CHUNK)],
            row(0), row(1), row(2)[:, 0:1], row(3)[:, 0:1], colb_ref[slot, pl.ds(r, CHUNK), :], mask_t,
            cs_ref[slot], ns_ref[slot], ms_ref[slot][:, 0:1])
        out_ref[ln, pl.ds(r, CHUNK)] = out_t
        cs_ref[slot] = c_s
        ns_ref[slot] = n_s
        ms_ref[slot] = jnp.broadcast_to(m_s, (1, hd))

    def body(t, carry):
        tb = nchunk - 1 - t
        rf = pl.multiple_of(t * CHUNK, CHUNK)
        rb = pl.multiple_of(tb * CHUNK, CHUNK)
        for hh, ln in enumerate(lanes):
            run(t, rf, ln, 2 * hh, upper, aft_ref)
            run(tb, rb, ln, 2 * hh + 1, lower, abt_ref)
        return carry

    lax.fori_loop(0, nchunk, body, 0)
    for ln in lanes:
        x_t = aft_ref[ln, :] + abt_ref[ln, :]
        xc = x_t - jnp.mean(x_t, axis=0, keepdims=True)
        y = (xc * lax.rsqrt(jnp.mean(xc * xc, axis=0, keepdims=True) + NORM_EPS)).T
        o_ref[0, :, ln] = (y * gn_ref[:, ln] * _sigmoid(mo_ref[0, :, ln])).astype(o_ref.dtype)


def mlstm(proj3, gates, gate_b, conv_w, conv_b, wq, wk, wv, gain, *, pad, mu_col0, mo_col0):
    b, lp, _ = proj3.shape
    hd = HEAD_DIM
    hps = HEADS_PER_STEP
    wd = hps * hd
    nchunk = lp // CHUNK
    gt = gates.reshape(b, lp, 4, ML_HEADS).transpose(0, 3, 2, 1).reshape(b, ML_HEADS, 4, nchunk, CHUNK)
    gb = jnp.broadcast_to(gate_b.T[:, :, None], (ML_HEADS, 4, hd)).astype(F32)
    blk = lambda off: pl.BlockSpec((1, lp, wd), lambda bi, hi: (bi, 0, off // hps + hi))
    per_h = lambda bi, hi: (hi, 0, 0)
    vec = pl.BlockSpec((1, wd), lambda bi, hi: (0, hi))
    sq = pl.BlockSpec((hps, hd, hd), per_h)
    tr = lambda w: jnp.swapaxes(w, 1, 2).astype(BF16)
    return pl.pallas_call(
        functools.partial(_mlstm_kernel, lp=lp, pad=pad, nchunk=nchunk),
        grid=(b, ML_HEADS // hps),
        in_specs=[blk(mu_col0), blk(mo_col0),
                  pl.BlockSpec((1, hps, 4, nchunk, CHUNK), lambda bi, hi: (bi, hi, 0, 0, 0)),
                  pl.BlockSpec((hps, 4, hd), per_h),
                  pl.BlockSpec((CONV_W, wd), lambda bi, hi: (0, hi)),
                  vec, sq, sq, sq, vec],
        out_specs=pl.BlockSpec((1, lp, wd), lambda bi, hi: (bi, 0, hi)),
        out_shape=jax.ShapeDtypeStruct((b, lp, ML_HEADS * hd), BF16),
        scratch_shapes=[pltpu.VMEM((lp, wd), BF16), pltpu.VMEM((wd, lp), BF16), pltpu.VMEM((wd, lp), BF16),
                        pltpu.VMEM((2 * hps, hd, lp), BF16),
                        pltpu.VMEM((8 * hps, nchunk, CHUNK), F32), pltpu.VMEM((lp, CHUNK), F32),
                        pltpu.VMEM((2 * hps, lp, CHUNK), F32),
                        pltpu.VMEM((wd, lp), F32), pltpu.VMEM((wd, lp), F32),
                        pltpu.VMEM((2 * hps, hd, hd), F32), pltpu.VMEM((2 * hps, 1, hd), F32),
                        pltpu.VMEM((2 * hps, 1, hd), F32)],
        compiler_params=_params("parallel", "parallel"),
        name="mlstm",
    )(proj3, proj3, gt, gb, conv_w.astype(F32), conv_b.reshape(1, -1).astype(F32),
      wk.astype(BF16), tr(wq), tr(wv), gain.reshape(1, -1).astype(F32))


def _rope_freqs(dim):
    return ROPE_THETA ** (-jnp.arange(dim // 2, dtype=F32) / (dim // 2))


def _axial_tables(n_tok, pad):
    rows = n_tok // GRID_W
    row = jnp.concatenate([jnp.zeros((pad,), F32), -jnp.ones((N_META,), F32),
                           jnp.repeat(jnp.arange(rows, dtype=F32), GRID_W)])
    col = jnp.concatenate([jnp.zeros((pad,), F32), jnp.arange(N_META, dtype=F32),
                           jnp.tile(jnp.arange(GRID_W, dtype=F32), rows)])
    f = _rope_freqs(HEAD_DIM // 2)
    ang = jnp.concatenate([row[:, None] * f[None, :]] * 2 + [col[:, None] * f[None, :]] * 2, axis=-1)
    first = (jnp.arange(HEAD_DIM) % 64) < 32
    sin = jnp.sin(ang)
    return jnp.cos(ang), jnp.where(first, -sin, 0.0), jnp.where(first, 0.0, sin)


def _linear_tables(l, pad):
    pos = jnp.concatenate([jnp.zeros((pad,), F32), jnp.arange(l, dtype=F32)])
    ang = pos[:, None] * _rope_freqs(HEAD_DIM)[None, :]
    ang = jnp.concatenate([ang, ang], axis=-1)
    sin = jnp.sin(ang)
    return jnp.cos(ang), jnp.where(jnp.arange(HEAD_DIM) < HEAD_DIM // 2, -sin, sin)


def _even_mixer_parts(hn, w_in_all, j, q_norm, k_norm, s5_params, glu_w, glu_b, tabs, *, b, lp, pad):
    att_w = ATT_HEADS * HEAD_DIM
    u0 = att_w + 2 * ATT_KV_HEADS * HEAD_DIM
    qkvu = matmul_wcast(hn, w_in_all, j, w_in_all.shape[2])
    qkvu3 = qkvu.reshape(b, lp, -1)
    att = attention(qkvu3, q_norm, k_norm, tabs, pad=pad).reshape(b * lp, att_w)
    y = s5_scan(qkvu3[:, :, u0:], _s5_matrices(*s5_params))
    ssm = s5_glu(y, glu_w.astype(BF16), glu_b)
    return [att, ssm]


def _odd_mixer_parts(hn, w_in_all, j, ret_log_decay, ret_norm, conv_w, conv_b, wq, wk, wv, gate_b, ml_norm,
                     tabs, *, b, lp, pad):
    ret_w = RET_HEADS * HEAD_DIM
    ml_w = ML_HEADS * HEAD_DIM
    main = 4 * ret_w + 2 * ml_w
    n_gate = w_in_all.shape[2] - main
    proj = matmul_wcast(hn, w_in_all, j, main)
    proj3 = proj.reshape(b, lp, main)
    w_gate = jnp.pad(w_in_all[j, :, main:].astype(BF16), ((0, 0), (0, CHUNK - n_gate)))
    gates = matmul(hn, w_gate)[:, :n_gate].reshape(b, lp, n_gate)
    log_gamma = -jnp.abs(ret_log_decay.astype(F32))
    ret = retention(proj3, log_gamma, ret_norm, *tabs)
    nblk = ret_w // HEAD_DIM
    hm = mlstm(proj3, gates, gate_b, conv_w, conv_b, wq, wk, wv, ml_norm,
               pad=pad, mu_col0=4 * nblk, mo_col0=4 * nblk + ml_w // HEAD_DIM)
    return [ret.reshape(b * lp, ret_w), hm.reshape(b * lp, ml_w)]


def kernel(x, meta_tokens, norm_gains, mlp_w1, mlp_w2, even_w_in, even_w_out, att_q_norm, att_k_norm, s5_lam_re, s5_lam_im, s5_log_dt, s5_b_re, s5_b_im, s5_c_re, s5_c_im, s5_d, s5_glu_w, s5_glu_b, odd_w_in, odd_w_out, ret_log_decay, ret_norm, ml_conv_w, ml_conv_b, ml_wq, ml_wk, ml_wv, ml_gate_b, ml_norm):
    b, n_tok, d_model = x.shape
    l = n_tok + N_META
    pad = (-l) % CHUNK
    lp = l + pad
    depth = norm_gains.shape[0]
    h = jnp.concatenate([jnp.zeros((b, pad, d_model), x.dtype),
                         jnp.broadcast_to(meta_tokens.astype(x.dtype)[None], (b, N_META, d_model)), x], axis=1)
    h = h.reshape(b * lp, d_model)
    axial = _axial_tables(n_tok, pad)
    linear = _linear_tables(l, pad)
    dims = dict(b=b, lp=lp, pad=pad)
    hn = rmsnorm(h, norm_gains[0, 0])
    for i in range(depth):
        j = i // 2
        if i % 2 == 0:
            s5_params = (s5_lam_re[j], s5_lam_im[j], s5_log_dt[j], s5_b_re[j], s5_b_im[j], s5_c_re[j],
                         s5_c_im[j], s5_d[j])
            parts = _even_mixer_parts(hn, even_w_in, j, att_q_norm[j], att_k_norm[j],
                                      s5_params, s5_glu_w[j], s5_glu_b[j], axial, **dims)
            w_out = even_w_out[j]
        else:
            parts = _odd_mixer_parts(hn, odd_w_in, j, ret_log_decay[j], ret_norm[j],
                                     ml_conv_w[j], ml_conv_b[j], ml_wq[j], ml_wk[j], ml_wv[j], ml_gate_b[j],
                                     ml_norm[j], linear, **dims)
            w_out = odd_w_out[j]
        h, hn = matmul_norm_res(parts, w_out.astype(BF16), norm_gains[i, 1], h, norm_gains[i, 2], lp=lp, pad=pad)
        hid = matmul_wcast(hn, mlp_w1, i, mlp_w1.shape[2], relu2=True, out_dtype=BF16)
        next_gain = norm_gains[i + 1, 0] if i + 1 < depth else None
        h, hn = matmul_norm_res([hid], mlp_w2[i].astype(BF16), norm_gains[i, 3], h, next_gain, lp=lp, pad=pad)
    return h.reshape(b, lp, d_model)[:, pad + N_META:]
```

```python
import functools
import math

import jax
import jax.numpy as jnp
from jax import lax
from jax.experimental import pallas as pl
from jax.experimental.pallas import tpu as pltpu

F32 = jnp.float32
BF16 = jnp.bfloat16

N_META = 16
GRID_W = 64
CHUNK = 128
HEAD_DIM = 128
NORM_EPS = 1e-6
ROPE_THETA = 10000.0
ATT_HEADS = 12
ATT_KV_HEADS = 4
ATT_GROUP = ATT_HEADS // ATT_KV_HEADS
S5_GROUP = 16
S5_GROUPS = 32
S5_STATE = 64
S5_T = 16
S5_COLS = S5_T * S5_GROUP
RET_HEADS = 8
ML_HEADS = 8
CONV_W = 5
NEG_GATE = -1e4
HEADS_PER_STEP = 2
BF16_ROWS = 16
VMEM_LIMIT_BYTES = 56 * 1024 * 1024


def _pick(n, cands):
    for c in cands:
        if n % c == 0:
            return c
    raise ValueError(f"no tile for {n} in {cands}")


def _params(*sem):
    return pltpu.CompilerParams(dimension_semantics=sem, vmem_limit_bytes=VMEM_LIMIT_BYTES)


def _dot(a, b):
    return jnp.dot(a, b, preferred_element_type=F32)


def _dot_nt(a, b):
    return lax.dot_general(a, b, (((1,), (1,)), ((), ())), preferred_element_type=F32)


def _dot_tn(a, b):
    return lax.dot_general(a, b, (((0,), (0,)), ((), ())), preferred_element_type=F32)


def _sigmoid(x):
    return 1.0 / (1.0 + jnp.exp(-x))


def _rmsnorm_kernel(x_ref, g_ref, o_ref):
    x = x_ref[...]
    ms = jnp.mean(x * x, axis=-1, keepdims=True)
    o_ref[...] = (x * lax.rsqrt(ms + NORM_EPS) * g_ref[...]).astype(o_ref.dtype)


def rmsnorm(x, gain):
    m, d = x.shape
    tm = _pick(m, (1024, 512, 384, 256, 128))
    return pl.pallas_call(
        _rmsnorm_kernel,
        grid=(m // tm,),
        in_specs=[pl.BlockSpec((tm, d), lambda i: (i, 0)), pl.BlockSpec((1, d), lambda i: (0, 0))],
        out_specs=pl.BlockSpec((tm, d), lambda i: (i, 0)),
        out_shape=jax.ShapeDtypeStruct((m, d), BF16),
        compiler_params=_params("parallel"),
        name="rmsnorm",
    )(x, gain.reshape(1, d).astype(F32))


def _matmul_kernel(a_ref, w_ref, o_ref, *, relu2):
    y = _dot(a_ref[...], w_ref[...])
    if relu2:
        y = jnp.square(jnp.maximum(y, 0.0))
    o_ref[...] = y.astype(o_ref.dtype)


def matmul(a, w, *, relu2=False, out_dtype=F32):
    m, k = a.shape
    n = w.shape[1]
    tm = _pick(m, (1024, 512, 384, 256, 128))
    tn = _pick(n, (2048, 1536, 1280, 1024, 512, 256, 128))
    return pl.pallas_call(
        functools.partial(_matmul_kernel, relu2=relu2),
        grid=(m // tm, n // tn),
        in_specs=[pl.BlockSpec((tm, k), lambda i, j: (i, 0)),
                  pl.BlockSpec((k, tn), lambda i, j: (0, j))],
        out_specs=pl.BlockSpec((tm, tn), lambda i, j: (i, j)),
        out_shape=jax.ShapeDtypeStruct((m, n), out_dtype),
        compiler_params=_params("parallel", "parallel"),
        name="matmul",
    )(a, w)


def _matmul_wcast_kernel(a_ref, w_ref, o_ref, wb_ref, *, relu2):
    @pl.when(pl.program_id(1) == 0)
    def _():
        wb_ref[...] = w_ref[...].astype(BF16)

    y = _dot(a_ref[...], wb_ref[...])
    if relu2:
        y = jnp.square(jnp.maximum(y, 0.0))
    o_ref[...] = y.astype(o_ref.dtype)


def matmul_wcast(a, w3, layer, n_cols, *, relu2=False, out_dtype=F32):
    m, k = a.shape
    tm = _pick(m, (1024, 512, 384, 256, 128))
    tn = _pick(n_cols, (1024, 512, 256, 128))
    return pl.pallas_call(
        functools.partial(_matmul_wcast_kernel, relu2=relu2),
        grid=(n_cols // tn, m // tm),
        in_specs=[pl.BlockSpec((tm, k), lambda j, i: (i, 0)),
                  pl.BlockSpec((None, k, tn), lambda j, i: (layer, 0, j))],
        out_specs=pl.BlockSpec((tm, tn), lambda j, i: (i, j)),
        out_shape=jax.ShapeDtypeStruct((m, n_cols), out_dtype),
        scratch_shapes=[pltpu.VMEM((k, tn), BF16)],
        compiler_params=_params("parallel", "arbitrary"),
        name="matmul_wcast",
    )(a, w3)


def _matmul_norm_res_kernel(*refs, widths, nk, tm, lp, pad, emit_next):
    na = len(widths)
    a_refs = refs[:na]
    w_ref, g_ref, h_ref = refs[na:na + 3]
    rest = refs[na + 3:]
    if emit_next:
        g2_ref, o_ref, n_ref = rest
    else:
        o_ref, = rest
    start = pl.program_id(0) * tm
    row = start + lax.broadcasted_iota(jnp.int32, (tm, 1), 0)
    rel0 = row - (start // lp) * lp
    rel1 = row - ((start + tm - 1) // lp) * lp
    is_pad = ((rel0 >= 0) & (rel0 < pad)) | ((rel1 >= 0) & (rel1 < pad))

    def finish(y):
        ms = jnp.mean(y * y, axis=-1, keepdims=True)
        out = jnp.where(is_pad, 0.0, h_ref[...] + y * lax.rsqrt(ms + NORM_EPS) * g_ref[...])
        o_ref[...] = out
        if emit_next:
            ms2 = jnp.mean(out * out, axis=-1, keepdims=True)
            n_ref[...] = (out * lax.rsqrt(ms2 + NORM_EPS) * g2_ref[...]).astype(n_ref.dtype)

    if nk == 1:
        part = None
        off = 0
        for a_ref, wd in zip(a_refs, widths):
            d = _dot(a_ref[...], w_ref[off:off + wd, :])
            part = d if part is None else part + d
            off += wd
        finish(part)
    else:
        kk = pl.program_id(1)

        @pl.when(kk == 0)
        def _():
            o_ref[...] = jnp.zeros_like(o_ref)

        o_ref[...] += _dot(a_refs[0][...], w_ref[...])

        @pl.when(kk == nk - 1)
        def _():
            finish(o_ref[...])


def matmul_norm_res(parts, w, gain, h, next_gain, *, lp, pad):
    m, n = h.shape
    widths = tuple(p.shape[1] for p in parts)
    k = sum(widths)
    tm = _pick(m, (512, 384, 256, 128))
    assert tm <= lp
    if len(parts) > 1 or k <= 2048:
        tk, nk = k, 1
    else:
        tk = 2048
        nk = k // tk
        widths = (tk,)
    emit_next = next_gain is not None
    row_blk = pl.BlockSpec((tm, n), lambda i, j: (i, 0))
    vec = pl.BlockSpec((1, n), lambda i, j: (0, 0))
    in_specs = [pl.BlockSpec((tm, wd), lambda i, j: (i, j)) for wd in widths]
    in_specs += [pl.BlockSpec((tk, n), lambda i, j: (j, 0)), vec, row_blk]
    args = [*parts, w, gain.reshape(1, n).astype(F32), h]
    out_specs = [row_blk]
    out_shape = [jax.ShapeDtypeStruct((m, n), F32)]
    if emit_next:
        in_specs.append(vec)
        args.append(next_gain.reshape(1, n).astype(F32))
        out_specs.append(row_blk)
        out_shape.append(jax.ShapeDtypeStruct((m, n), BF16))
    res = pl.pallas_call(
        functools.partial(_matmul_norm_res_kernel, widths=widths, nk=nk, tm=tm, lp=lp, pad=pad,
                          emit_next=emit_next),
        grid=(m // tm, nk),
        in_specs=in_specs,
        out_specs=out_specs,
        out_shape=out_shape,
        compiler_params=_params("parallel", "arbitrary"),
        name="matmul_norm_res",
    )(*args)
    return (res[0], res[1]) if emit_next else (res[0], None)


def _rope_axial(x, c, sa, sb):
    return x * c + pltpu.roll(x, HEAD_DIM - 32, 1) * sa + pltpu.roll(x, 32, 1) * sb


def _attn_kernel(q_ref, k_ref, v_ref, qg_ref, kg_ref, c_ref, sa_ref, sb_ref, o_ref,
                 ks_ref, vt_ref, sta_ref, stb_ref, pa_ref, pb_ref, *, lp, pad, tq):
    k = k_ref[0]
    k = k * lax.rsqrt(jnp.mean(k * k, axis=-1, keepdims=True) + NORM_EPS) * kg_ref[...]
    ks_ref[...] = _rope_axial(k, c_ref[...], sa_ref[...], sb_ref[...]).astype(BF16)
    vt_ref[...] = v_ref[0].T.astype(BF16)
    for p_ref in (pa_ref, pb_ref):
        p_ref[0:pad, :] = jnp.zeros((pad, p_ref.shape[1]), BF16)
    scale = HEAD_DIM ** -0.5 * math.log2(math.e)

    def scores(r0, rows, st_ref):
        c = c_ref[pl.ds(r0, rows), :]
        sa = sa_ref[pl.ds(r0, rows), :]
        sb = sb_ref[pl.ds(r0, rows), :]
        qs = []
        for g in range(ATT_GROUP):
            q = q_ref[0, pl.ds(r0, rows), g * HEAD_DIM:(g + 1) * HEAD_DIM]
            q = q * lax.rsqrt(jnp.mean(q * q, axis=-1, keepdims=True) + NORM_EPS) * qg_ref[...]
            qs.append((_rope_axial(q, c, sa, sb) * scale).astype(BF16))
        qall = jnp.concatenate(qs, axis=0)
        st_ref[:, 0:ATT_GROUP * rows] = _dot_nt(ks_ref[pad:, :], qall)

    def attend(r0, rows, st_ref, p_ref):
        n = ATT_GROUP * rows
        st = st_ref[:, 0:n]
        p = jnp.exp2(st - jnp.max(st, axis=0, keepdims=True))
        l = jnp.sum(p, axis=0, keepdims=True)
        p_ref[pad:, 0:n] = p.astype(BF16)
        o = (_dot(vt_ref[...], p_ref[:, 0:n]) / l).T
        for g in range(ATT_GROUP):
            o_ref[0, pl.ds(r0, rows), g * HEAD_DIM:(g + 1) * HEAD_DIM] = (
                o[g * rows:(g + 1) * rows].astype(o_ref.dtype))

    bufs = ((sta_ref, pa_ref), (stb_ref, pb_ref))
    nbig = lp // tq
    blocks = [(i * tq, tq) for i in range(nbig)]
    if lp % tq:
        blocks.append((nbig * tq, lp % tq))
    npairs = max(0, (nbig - 1) // 2)
    scores(0, blocks[0][1], sta_ref)

    def body(u, carry):
        r, r1, r2 = (pl.multiple_of((2 * u + i) * tq, tq) for i in range(3))
        scores(r1, tq, stb_ref)
        attend(r, tq, sta_ref, pa_ref)
        scores(r2, tq, sta_ref)
        attend(r1, tq, stb_ref, pb_ref)
        return carry

    lax.fori_loop(0, npairs, body, 0)
    for i in range(2 * npairs, len(blocks)):
        if i + 1 < len(blocks):
            scores(*blocks[i + 1], bufs[(i + 1) % 2][0])
        attend(*blocks[i], *bufs[i % 2])


def attention(qkvu3, q_gain, k_gain, tabs, *, pad):
    b, lp, _ = qkvu3.shape
    tq = 2 * CHUNK
    gw = ATT_GROUP * HEAD_DIM
    k_col0 = ATT_HEADS
    v_col0 = ATT_HEADS + ATT_KV_HEADS
    full = lambda bi, hi: (0, 0)
    st = pltpu.VMEM((lp - pad, ATT_GROUP * tq), F32)
    pb = pltpu.VMEM((lp, ATT_GROUP * tq), BF16)
    return pl.pallas_call(
        functools.partial(_attn_kernel, lp=lp, pad=pad, tq=tq),
        grid=(b, ATT_KV_HEADS),
        in_specs=[pl.BlockSpec((1, lp, gw), lambda bi, hi: (bi, 0, hi)),
                  pl.BlockSpec((1, lp, HEAD_DIM), lambda bi, hi: (bi, 0, k_col0 + hi)),
                  pl.BlockSpec((1, lp, HEAD_DIM), lambda bi, hi: (bi, 0, v_col0 + hi)),
                  pl.BlockSpec((1, HEAD_DIM), full),
                  pl.BlockSpec((1, HEAD_DIM), full),
                  pl.BlockSpec((lp, HEAD_DIM), full),
                  pl.BlockSpec((lp, HEAD_DIM), full),
                  pl.BlockSpec((lp, HEAD_DIM), full)],
        out_specs=pl.BlockSpec((1, lp, gw), lambda bi, hi: (bi, 0, hi)),
        out_shape=jax.ShapeDtypeStruct((b, lp, ATT_HEADS * HEAD_DIM), BF16),
        scratch_shapes=[pltpu.VMEM((lp, HEAD_DIM), BF16), pltpu.VMEM((HEAD_DIM, lp), BF16), st, st, pb, pb],
        compiler_params=_params("parallel", "parallel"),
        name="attention",
    )(qkvu3, qkvu3, qkvu3, q_gain.reshape(1, HEAD_DIM).astype(F32), k_gain.reshape(1, HEAD_DIM).astype(F32),
      *tabs)


def _s5_kernel(u_ref, k_ref, w_ref, v_ref, at_ref, y_ref, s_ref, x_ref, *, nchunk, nb):
    u = u_ref[0]
    s_ref[...] = _dot(u, w_ref[0])
    at = at_ref[0]
    afr, afi, abr, abi = (at[:, i * CHUNK:(i + 1) * CHUNK] for i in range(4))

    def body(c, carry):
        xfr, xfi, xbr, xbi = carry
        rf = pl.multiple_of(c * nb, nb)
        rb = pl.multiple_of((nchunk - 1 - c) * nb, nb)
        x_ref[pl.ds(rf, nb), 0:CHUNK] = xfr
        x_ref[pl.ds(rf, nb), CHUNK:2 * CHUNK] = xfi
        x_ref[pl.ds(rb, nb), 2 * CHUNK:3 * CHUNK] = xbr
        x_ref[pl.ds(rb, nb), 3 * CHUNK:4 * CHUNK] = xbi
        sfr = s_ref[pl.ds(rf, nb), 0:CHUNK]
        sfi = s_ref[pl.ds(rf, nb), CHUNK:2 * CHUNK]
        sbr = s_ref[pl.ds(rb, nb), 2 * CHUNK:3 * CHUNK]
        sbi = s_ref[pl.ds(rb, nb), 3 * CHUNK:4 * CHUNK]
        return (afr * xfr - afi * xfi + sfr, afr * xfi + afi * xfr + sfi,
                abr * xbr - abi * xbi + sbr, abr * xbi + abi * xbr + sbi)

    z = jnp.zeros((nb, CHUNK), F32)
    lax.fori_loop(0, nchunk, body, (z, z, z, z))
    y_ref[0] = _dot(u, k_ref[0]) + _dot(x_ref[...].astype(BF16), v_ref[0])


def _s5_matrices(lam_re, lam_im, log_dt, b_re, b_im, c_re, c_im, d_skip):
    hi = lax.Precision.HIGHEST
    t = S5_T
    lr = jnp.minimum(lam_re, -1e-4)
    li = lam_im
    dt = jnp.exp(log_dt)[..., None]
    er = jnp.exp(lr * dt)
    abar_re = er * jnp.cos(li * dt)
    abar_im = er * jnp.sin(li * dt)
    nr = abar_re - 1.0
    den = lr * lr + li * li
    coef_re = (nr * lr + abar_im * li) / den
    coef_im = (abar_im * lr - nr * li) / den
    bb_re = coef_re[..., None] * b_re - coef_im[..., None] * b_im
    bb_im = coef_re[..., None] * b_im + coef_im[..., None] * b_re
    kk = jnp.arange(t + 1, dtype=F32)[:, None, None, None]
    mag = jnp.exp(kk * (lr * dt)[None])
    pw_re = mag * jnp.cos(kk * (li * dt)[None])
    pw_im = mag * jnp.sin(kk * (li * dt)[None])
    g, hh = d_skip.shape
    ct_re = c_re.transpose(0, 1, 3, 2)
    ct_im = c_im.transpose(0, 1, 3, 2)
    flat = lambda a: a.reshape(2, g, S5_STATE, hh * hh)
    bc_re = flat(bb_re[..., :, None] * ct_re[..., None, :] - bb_im[..., :, None] * ct_im[..., None, :])
    bc_im = flat(bb_re[..., :, None] * ct_im[..., None, :] + bb_im[..., :, None] * ct_re[..., None, :])
    lagk = (jnp.einsum('kdgp,dgpn->dgkn', pw_re, bc_re, precision=hi)
            - jnp.einsum('kdgp,dgpn->dgkn', pw_im, bc_im, precision=hi))
    ti = jnp.arange(t)
    lag = ti[None, :] - ti[:, None]
    sel = lambda m: m[None, :, :, None]
    skip = (jnp.eye(hh, dtype=F32)[None] * d_skip[:, None, :]).reshape(g, 1, 1, hh * hh)
    ktot = (jnp.where(sel(lag >= 0), lagk[0][:, jnp.clip(lag, 0, t)], 0.0)
            + jnp.where(sel(lag <= 0), lagk[1][:, jnp.clip(-lag, 0, t)], 0.0)
            + jnp.where(sel(lag == 0), skip, 0.0))
    ktot = ktot.reshape(g, t, t, hh, hh).transpose(0, 1, 3, 2, 4).reshape(g, t * hh, t * hh)

    def bsum(pw_r, pw_i, d):
        wr = pw_r[:, :, :, None] * bb_re[d][None] - pw_i[:, :, :, None] * bb_im[d][None]
        wi = pw_r[:, :, :, None] * bb_im[d][None] + pw_i[:, :, :, None] * bb_re[d][None]
        tr = lambda a: a.transpose(1, 0, 3, 2).reshape(g, t * hh, S5_STATE)
        return tr(wr), tr(wi)

    wf_re, wf_im = bsum(pw_re[:t, 0][::-1], pw_im[:t, 0][::-1], 0)
    wb_re, wb_im = bsum(pw_re[:t, 1], pw_im[:t, 1], 1)
    padl = lambda a: jnp.pad(a, ((0, 0), (0, 0), (0, CHUNK - S5_STATE)))
    wtot = jnp.concatenate([padl(wf_re), padl(wf_im), padl(wb_re), padl(wb_im)], axis=-1)

    def vmat(pw_r, pw_i, d):
        vr = pw_r[:, :, None, :] * c_re[d][None] - pw_i[:, :, None, :] * c_im[d][None]
        vi = pw_r[:, :, None, :] * c_im[d][None] + pw_i[:, :, None, :] * c_re[d][None]
        tr = lambda a: a.transpose(1, 3, 0, 2).reshape(g, S5_STATE, t * hh)
        return tr(vr), tr(-vi)

    vf_re, vf_im = vmat(pw_re[1:, 0], pw_im[1:, 0], 0)
    vb_re, vb_im = vmat(pw_re[1:, 1][::-1], pw_im[1:, 1][::-1], 1)
    padr = lambda a: jnp.pad(a, ((0, 0), (0, CHUNK - S5_STATE), (0, 0)))
    vtot = jnp.concatenate([padr(vf_re), padr(vf_im), padr(vb_re), padr(vb_im)], axis=1)
    padv = lambda a: jnp.pad(a, ((0, 0), (0, CHUNK - S5_STATE)))
    at = jnp.concatenate([padv(pw_re[t, 0]), padv(pw_im[t, 0]), padv(pw_re[t, 1]), padv(pw_im[t, 1])], axis=-1)
    return ktot.astype(BF16), wtot.astype(BF16), vtot.astype(BF16), at[:, None, :]


def s5_scan(u, mats):
    b, lp, _ = u.shape
    nchunk = lp // S5_T
    ktot, wtot, vtot, at = mats
    ug = u.astype(BF16).reshape(b, nchunk, S5_T, S5_GROUPS, S5_GROUP).transpose(3, 1, 0, 2, 4)
    ug = ug.reshape(S5_GROUPS, nchunk * b, S5_COLS)
    rows = nchunk * b
    per_g = lambda g: (g, 0, 0)
    y = pl.pallas_call(
        functools.partial(_s5_kernel, nchunk=nchunk, nb=b),
        grid=(S5_GROUPS,),
        in_specs=[pl.BlockSpec((1, rows, S5_COLS), per_g),
                  pl.BlockSpec((1, S5_COLS, S5_COLS), per_g),
                  pl.BlockSpec((1, S5_COLS, 4 * CHUNK), per_g),
                  pl.BlockSpec((1, 4 * CHUNK, S5_COLS), per_g),
                  pl.BlockSpec((1, 1, 4 * CHUNK), per_g)],
        out_specs=pl.BlockSpec((1, rows, S5_COLS), per_g),
        out_shape=jax.ShapeDtypeStruct((S5_GROUPS, rows, S5_COLS), F32),
        scratch_shapes=[pltpu.VMEM((rows, 4 * CHUNK), F32), pltpu.VMEM((rows, 4 * CHUNK), F32)],
        compiler_params=_params("parallel"),
        name="s5_scan",
    )(ug, ktot, wtot, vtot, at)
    y = y.reshape(S5_GROUPS, nchunk, b, S5_T, S5_GROUP).transpose(2, 1, 3, 0, 4)
    return y.reshape(b * lp, S5_GROUPS * S5_GROUP)


def _s5_glu_kernel(y_ref, w_ref, b_ref, o_ref):
    x = y_ref[...]
    y = x * (0.5 * (1.0 + jnp.tanh(math.sqrt(2.0 / math.pi) * (x + 0.044715 * (x * x * x)))))
    z = _dot(y.astype(BF16), w_ref[...]) + b_ref[...]
    o_ref[...] = (y * _sigmoid(z)).astype(o_ref.dtype)


def s5_glu(y, w, bias):
    m, n = y.shape
    tm = _pick(m, (1024, 512, 384, 256, 128))
    return pl.pallas_call(
        _s5_glu_kernel,
        grid=(m // tm,),
        in_specs=[pl.BlockSpec((tm, n), lambda i: (i, 0)),
                  pl.BlockSpec((n, n), lambda i: (0, 0)),
                  pl.BlockSpec((1, n), lambda i: (0, 0))],
        out_specs=pl.BlockSpec((tm, n), lambda i: (i, 0)),
        out_shape=jax.ShapeDtypeStruct((m, n), BF16),
        compiler_params=_params("parallel"),
        name="s5_glu",
    )(y, w, bias.reshape(1, n).astype(F32))


def _head_norm(x, gain):
    xc = x - jnp.mean(x, axis=-1, keepdims=True)
    return xc * lax.rsqrt(jnp.mean(xc * xc, axis=-1, keepdims=True) + NORM_EPS) * gain


def _ret_kernel(q_ref, k_ref, v_ref, g_ref, c_ref, s_ref, lg_ref, gn_ref, o_ref,
                qs_ref, ks_ref, vs_ref, vt_ref, af_ref, ab_ref, st_ref, *, nchunk):
    c = c_ref[...]
    s = s_ref[...]
    ii = lax.broadcasted_iota(jnp.int32, (CHUNK, CHUNK), 0).astype(F32)
    jj = lax.broadcasted_iota(jnp.int32, (CHUNK, CHUNK), 1).astype(F32)
    lane = (lax.broadcasted_iota(jnp.int32, (1, nchunk * CHUNK), 1) & (CHUNK - 1)).astype(F32)
    diff = ii - jj
    lanes = [slice(hh * HEAD_DIM, (hh + 1) * HEAD_DIM) for hh in range(HEADS_PER_STEP)]
    consts = []
    for hh, ln in enumerate(lanes):
        q = q_ref[0, :, ln]
        qs_ref[:, ln] = ((q * c + pltpu.roll(q, HEAD_DIM // 2, 1) * s) * HEAD_DIM ** -0.5).astype(BF16)
        k = k_ref[0, :, ln]
        ks_ref[:, ln] = (k * c + pltpu.roll(k, HEAD_DIM // 2, 1) * s).astype(BF16)
        v = v_ref[0, :, ln]
        vs_ref[:, ln] = v.astype(BF16)
        v_t = v.T
        lgf = lg_ref[hh, 0:1, :]
        lgb = lg_ref[hh, 1:2, :]
        vt_ref[2 * hh] = (v_t * jnp.exp((CHUNK - 1 - lane) * lgf[:, 0:1])).astype(BF16)
        vt_ref[2 * hh + 1] = (v_t * jnp.exp(lane * lgb[:, 0:1])).astype(BF16)
        fwd = (jnp.where(diff >= 0, jnp.exp(jnp.where(diff >= 0, diff, 0.0) * lgf), 0.0),
               jnp.exp((ii + 1.0) * lgf), jnp.exp(CHUNK * lgf))
        bwd = (jnp.where(diff < 0, jnp.exp(jnp.where(diff < 0, -diff, 0.0) * lgb), 0.0),
               jnp.exp((CHUNK - ii) * lgb), jnp.exp(CHUNK * lgb))
        consts.append((fwd, bwd))
    st_ref[...] = jnp.zeros_like(st_ref)

    def chunk(r, ln, slot, cst, out_ref):
        dec, xi, gc = cst
        qc = qs_ref[pl.ds(r, CHUNK), ln]
        kc = ks_ref[pl.ds(r, CHUNK), ln]
        vc = vs_ref[pl.ds(r, CHUNK), ln]
        state_t = st_ref[slot]
        both = _dot_nt(qc, jnp.concatenate([kc, state_t.astype(BF16)], axis=0))
        sc = both[:, :CHUNK] * dec
        res = _dot(jnp.concatenate([sc.astype(BF16), vt_ref[slot, :, pl.ds(r, CHUNK)]], axis=0),
                   jnp.concatenate([vc, kc], axis=1))
        out_ref[pl.ds(r, CHUNK), ln] = res[:CHUNK, :HEAD_DIM] + both[:, CHUNK:] * xi
        st_ref[slot] = gc * state_t + res[CHUNK:, HEAD_DIM:]

    def body(t, carry):
        rf = pl.multiple_of(t * CHUNK, CHUNK)
        rb = pl.multiple_of((nchunk - 1 - t) * CHUNK, CHUNK)
        for hh, ln in enumerate(lanes):
            chunk(rf, ln, 2 * hh, consts[hh][0], af_ref)
            chunk(rb, ln, 2 * hh + 1, consts[hh][1], ab_ref)
        return carry

    lax.fori_loop(0, nchunk, body, 0)
    for ln in lanes:
        gate = g_ref[0, :, ln]
        y = _head_norm(af_ref[:, ln] + ab_ref[:, ln], gn_ref[:, ln])
        o_ref[0, :, ln] = (y * (gate * _sigmoid(gate))).astype(o_ref.dtype)


def retention(proj3, log_gamma, gain, cos_t, sin_t):
    b, lp, _ = proj3.shape
    hd = HEAD_DIM
    hps = HEADS_PER_STEP
    wd = hps * hd
    nblk = RET_HEADS // hps
    lg = jnp.broadcast_to(log_gamma.T[:, :, None], (RET_HEADS, 2, hd)).astype(F32)
    blk = lambda off: pl.BlockSpec((1, lp, wd), lambda bi, hi: (bi, 0, off + hi))
    full = lambda bi, hi: (0, 0)
    return pl.pallas_call(
        functools.partial(_ret_kernel, nchunk=lp // CHUNK),
        grid=(b, nblk),
        in_specs=[blk(0), blk(nblk), blk(2 * nblk), blk(3 * nblk),
                  pl.BlockSpec((lp, hd), full), pl.BlockSpec((lp, hd), full),
                  pl.BlockSpec((hps, 2, hd), lambda bi, hi: (hi, 0, 0)),
                  pl.BlockSpec((1, wd), lambda bi, hi: (0, hi))],
        out_specs=pl.BlockSpec((1, lp, wd), lambda bi, hi: (bi, 0, hi)),
        out_shape=jax.ShapeDtypeStruct((b, lp, RET_HEADS * hd), BF16),
        scratch_shapes=[pltpu.VMEM((lp, wd), BF16), pltpu.VMEM((lp, wd), BF16), pltpu.VMEM((lp, wd), BF16),
                        pltpu.VMEM((2 * hps, hd, lp), BF16), pltpu.VMEM((lp, wd), F32), pltpu.VMEM((lp, wd), F32),
                        pltpu.VMEM((2 * hps, hd, hd), F32)],
        compiler_params=_params("parallel", "parallel"),
        name="retention",
    )(proj3, proj3, proj3, proj3, cos_t, sin_t, lg, gain.reshape(1, RET_HEADS * hd).astype(F32))


def _mlstm_gate_tables(li, lf, tri_sum):
    bt = jnp.dot(lf, tri_sum, preferred_element_type=F32, precision=lax.Precision.HIGHEST)
    bt_last = jnp.sum(lf, axis=1, keepdims=True)
    a = bt_last - bt + li
    m_loc = jnp.max(a, axis=1, keepdims=True)
    return bt, jnp.exp(a - m_loc), m_loc, bt_last, li - bt


def _mlstm_chunk_t(kc, qtc, vtc, vwc, bt_row, w_row, m_loc, bt_last, colb, mask_t, ct_prev, n_prev, m_prev):
    d, c, pk = HEAD_DIM, CHUNK, BF16_ROWS
    dlog_t = jnp.where(mask_t, bt_row + colb, -jnp.inf)
    g_row = bt_row + m_prev
    m_t = jnp.maximum(g_row, jnp.max(dlog_t, axis=0, keepdims=True))
    r1 = _dot(jnp.concatenate([kc, ct_prev.astype(BF16), jnp.broadcast_to(n_prev, (pk, d)).astype(BF16)], axis=0),
              qtc)
    s_t = r1[:c] * jnp.exp(dlog_t - m_t)
    w_int = jnp.exp(g_row - m_t)
    r2 = _dot(jnp.concatenate([vtc, vwc, jnp.broadcast_to(w_row, (pk, c)).astype(BF16)], axis=0),
              jnp.concatenate([s_t.astype(BF16), kc], axis=1))
    num_t = r2[:d, :c] + w_int * r1[c:c + d]
    den_t = jnp.sum(s_t, axis=0, keepdims=True) + w_int * r1[c + d:c + d + 1]
    out_t = num_t / jnp.maximum(jnp.abs(den_t), jnp.exp(-m_t))
    m_new = jnp.maximum(bt_last + m_prev, m_loc)
    f_prev = jnp.exp(bt_last + m_prev - m_new)
    f_loc = jnp.exp(m_loc - m_new)
    c_new = f_prev * ct_prev + f_loc * r2[d:2 * d, c:]
    n_new = f_prev * n_prev + f_loc * r2[2 * d:2 * d + 1, c:]
    return out_t, c_new, n_new, m_new


def _mlstm_kernel(mu_ref, mo_ref, gt_ref, gb_ref, cw_ref, cb_ref, wk_ref, wqt_ref, wvt_ref, gn_ref, o_ref,
                  ks_ref, qt_ref, vt_ref, vw_ref, row_ref, d_ref, colb_ref, aft_ref, abt_ref,
                  cs_ref, ns_ref, ms_ref, *, lp, pad, nchunk):
    hd = HEAD_DIM
    lanes = [slice(hh * hd, (hh + 1) * hd) for hh in range(HEADS_PER_STEP)]
    valid_row = lax.broadcasted_iota(jnp.int32, (lp, 1), 0) >= pad
    valid_col = lax.broadcasted_iota(jnp.int32, (1, lp), 1) >= pad
    pos = (lax.broadcasted_iota(jnp.int32, (nchunk, CHUNK), 0) * CHUNK
           + lax.broadcasted_iota(jnp.int32, (nchunk, CHUNK), 1))
    valid_pos = pos >= pad
    ii = lax.broadcasted_iota(jnp.int32, (CHUNK, CHUNK), 0)
    jj = lax.broadcasted_iota(jnp.int32, (CHUNK, CHUNK), 1)
    eye = ii == jj
    upper = ii <= jj
    lower = ii >= jj
    ones = jnp.ones((CHUNK, CHUNK), BF16)
    for hh, ln in enumerate(lanes):
        mu = mu_ref[0, :, ln]
        conv = cb_ref[:, ln]
        for j in range(CONV_W):
            conv = conv + cw_ref[j:j + 1, ln] * pltpu.roll(mu, (CONV_W // 2 - j) % lp, 0)
        uc = (conv * _sigmoid(conv)).astype(BF16)
        ks_ref[:, ln] = jnp.where(valid_row, _dot(uc, wk_ref[hh]) * hd ** -0.5, 0.0).astype(BF16)
        qt_ref[ln, :] = jnp.where(valid_col, _dot_nt(wqt_ref[hh], uc), 0.0).astype(BF16)
        v_t = jnp.where(valid_col, _dot_nt(wvt_ref[hh], mu.astype(BF16)), 0.0)
        vt_ref[ln, :] = v_t.astype(BF16)

        for d, tri_sum in enumerate((upper, lower)):
            g_i = gt_ref[0, hh, 2 * d] + gb_ref[hh, 2 * d:2 * d + 1, 0:1]
            g_f = gt_ref[0, hh, 2 * d + 1] + gb_ref[hh, 2 * d + 1:2 * d + 2, 0:1]
            li = jnp.where(valid_pos, g_i, NEG_GATE)
            lf = jnp.where(valid_pos, jnp.minimum(g_f, 0.0) - jnp.log(1.0 + jnp.exp(-jnp.abs(g_f))), 0.0)
            bt, w, m_loc, bt_last, colv = _mlstm_gate_tables(li, lf, jnp.where(tri_sum, 1.0, 0.0))
            slot = 2 * hh + d
            row_ref[4 * slot + 0] = bt
            row_ref[4 * slot + 1] = w
            row_ref[4 * slot + 2] = jnp.broadcast_to(m_loc, (nchunk, CHUNK))
            row_ref[4 * slot + 3] = jnp.broadcast_to(bt_last, (nchunk, CHUNK))
            for n in range(nchunk):
                cols = slice(n * CHUNK, (n + 1) * CHUNK)
                vw_ref[slot, :, cols] = (v_t[:, cols] * w[n:n + 1, :]).astype(BF16)
                d_ref[cols, :] = jnp.where(eye, colv[n:n + 1, :], 0.0)
            diag = d_ref[...]
            d_hi = diag.astype(BF16)
            d_lo = (diag - d_hi.astype(F32)).astype(BF16)
            colb_ref[slot] = _dot(d_hi, ones) + _dot(d_lo, ones)
    cs_ref[...] = jnp.zeros_like(cs_ref)
    ns_ref[...] = jnp.zeros_like(ns_ref)
    ms_ref[...] = jnp.zeros_like(ms_ref)

    def run(n, r, ln, slot, mask_t, out_ref):
        row = lambda kind: row_ref[4 * slot + kind, pl.ds(n, 1), :]
        out_t, c_s, n_s, m_s = _mlstm_chunk_t(
            ks_ref[pl.ds(r, CHUNK), ln], qt_ref[ln, pl.ds(r, CHUNK)], vt_ref[ln, pl.ds(r, CHUNK)],
            vw_ref[slot, :, pl.ds(r, CHUNK)],
            row(0), row(1), row(2)[:, 0:1], row(3)[:, 0:1], colb_ref[slot, pl.ds(r, CHUNK), :], mask_t,
            cs_ref[slot], ns_ref[slot], ms_ref[slot][:, 0:1])
        out_ref[ln, pl.ds(r, CHUNK)] = out_t
        cs_ref[slot] = c_s
        ns_ref[slot] = n_s
        ms_ref[slot] = jnp.broadcast_to(m_s, (1, hd))

    def body(t, carry):
        tb = nchunk - 1 - t
        rf = pl.multiple_of(t * CHUNK, CHUNK)
        rb = pl.multiple_of(tb * CHUNK, CHUNK)
        for hh, ln in enumerate(lanes):
            run(t, rf, ln, 2 * hh, upper, aft_ref)
            run(tb, rb, ln, 2 * hh + 1, lower, abt_ref)
        return carry

    lax.fori_loop(0, nchunk, body, 0)
    for ln in lanes:
        x_t = aft_ref[ln, :] + abt_ref[ln, :]
        xc = x_t - jnp.mean(x_t, axis=0, keepdims=True)
        y = (xc * lax.rsqrt(jnp.mean(xc * xc, axis=0, keepdims=True) + NORM_EPS)).T
        o_ref[0, :, ln] = (y * gn_ref[:, ln] * _sigmoid(mo_ref[0, :, ln])).astype(o_ref.dtype)


def mlstm(proj3, gates, gate_b, conv_w, conv_b, wq, wk, wv, gain, *, pad, mu_col0, mo_col0):
    b, lp, _ = proj3.shape
    hd = HEAD_DIM
    hps = HEADS_PER_STEP
    wd = hps * hd
    nchunk = lp // CHUNK
    gt = gates.reshape(b, lp, 4, ML_HEADS).transpose(0, 3, 2, 1).reshape(b, ML_HEADS, 4, nchunk, CHUNK)
    gb = jnp.broadcast_to(gate_b.T[:, :, None], (ML_HEADS, 4, hd)).astype(F32)
    blk = lambda off: pl.BlockSpec((1, lp, wd), lambda bi, hi: (bi, 0, off // hps + hi))
    per_h = lambda bi, hi: (hi, 0, 0)
    vec = pl.BlockSpec((1, wd), lambda bi, hi: (0, hi))
    sq = pl.BlockSpec((hps, hd, hd), per_h)
    tr = lambda w: jnp.swapaxes(w, 1, 2).astype(BF16)
    return pl.pallas_call(
        functools.partial(_mlstm_kernel, lp=lp, pad=pad, nchunk=nchunk),
        grid=(b, ML_HEADS // hps),
        in_specs=[blk(mu_col0), blk(mo_col0),
                  pl.BlockSpec((1, hps, 4, nchunk, CHUNK), lambda bi, hi: (bi, hi, 0, 0, 0)),
                  pl.BlockSpec((hps, 4, hd), per_h),
                  pl.BlockSpec((CONV_W, wd), lambda bi, hi: (0, hi)),
                  vec, sq, sq, sq, vec],
        out_specs=pl.BlockSpec((1, lp, wd), lambda bi, hi: (bi, 0, hi)),
        out_shape=jax.ShapeDtypeStruct((b, lp, ML_HEADS * hd), BF16),
        scratch_shapes=[pltpu.VMEM((lp, wd), BF16), pltpu.VMEM((wd, lp), BF16), pltpu.VMEM((wd, lp), BF16),
                        pltpu.VMEM((2 * hps, hd, lp), BF16),
                        pltpu.VMEM((8 * hps, nchunk, CHUNK), F32), pltpu.VMEM((lp, CHUNK), F32),
                        pltpu.VMEM((2 * hps, lp, CHUNK), F32),
                        pltpu.VMEM((wd, lp), F32), pltpu.VMEM((wd, lp), F32),
                        pltpu.VMEM((2 * hps, hd, hd), F32), pltpu.VMEM((2 * hps, 1, hd), F32),
                        pltpu.VMEM((2 * hps, 1, hd), F32)],
        compiler_params=_params("parallel", "parallel"),
        name="mlstm",
    )(proj3, proj3, gt, gb, conv_w.astype(F32), conv_b.reshape(1, -1).astype(F32),
      wk.astype(BF16), tr(wq), tr(wv), gain.reshape(1, -1).astype(F32))


def _rope_freqs(dim):
    return ROPE_THETA ** (-jnp.arange(dim // 2, dtype=F32) / (dim // 2))


def _axial_tables(n_tok, pad):
    rows = n_tok // GRID_W
    row = jnp.concatenate([jnp.zeros((pad,), F32), -jnp.ones((N_META,), F32),
                           jnp.repeat(jnp.arange(rows, dtype=F32), GRID_W)])
    col = jnp.concatenate([jnp.zeros((pad,), F32), jnp.arange(N_META, dtype=F32),
                           jnp.tile(jnp.arange(GRID_W, dtype=F32), rows)])
    f = _rope_freqs(HEAD_DIM // 2)
    ang = jnp.concatenate([row[:, None] * f[None, :]] * 2 + [col[:, None] * f[None, :]] * 2, axis=-1)
    first = (jnp.arange(HEAD_DIM) % 64) < 32
    sin = jnp.sin(ang)
    return jnp.cos(ang), jnp.where(first, -sin, 0.0), jnp.where(first, 0.0, sin)


def _linear_tables(l, pad):
    pos = jnp.concatenate([jnp.zeros((pad,), F32), jnp.arange(l, dtype=F32)])
    ang = pos[:, None] * _rope_freqs(HEAD_DIM)[None, :]
    ang = jnp.concatenate([ang, ang], axis=-1)
    sin = jnp.sin(ang)
    return jnp.cos(ang), jnp.where(jnp.arange(HEAD_DIM) < HEAD_DIM // 2, -sin, sin)


def _even_mixer_parts(hn, w_in_all, j, q_norm, k_norm, s5_params, glu_w, glu_b, tabs, *, b, lp, pad):
    att_w = ATT_HEADS * HEAD_DIM
    u0 = att_w + 2 * ATT_KV_HEADS * HEAD_DIM
    qkvu = matmul_wcast(hn, w_in_all, j, w_in_all.shape[2])
    qkvu3 = qkvu.reshape(b, lp, -1)
    att = attention(qkvu3, q_norm, k_norm, tabs, pad=pad).reshape(b * lp, att_w)
    y = s5_scan(qkvu3[:, :, u0:], _s5_matrices(*s5_params))
    ssm = s5_glu(y, glu_w.astype(BF16), glu_b)
    return [att, ssm]


def _odd_mixer_parts(hn, w_in_all, j, ret_log_decay, ret_norm, conv_w, conv_b, wq, wk, wv, gate_b, ml_norm,
                     tabs, *, b, lp, pad):
    ret_w = RET_HEADS * HEAD_DIM
    ml_w = ML_HEADS * HEAD_DIM
    main = 4 * ret_w + 2 * ml_w
    n_gate = w_in_all.shape[2] - main
    proj = matmul_wcast(hn, w_in_all, j, main)
    proj3 = proj.reshape(b, lp, main)
    w_gate = jnp.pad(w_in_all[j, :, main:].astype(BF16), ((0, 0), (0, CHUNK - n_gate)))
    gates = matmul(hn, w_gate)[:, :n_gate].reshape(b, lp, n_gate)
    log_gamma = -jnp.abs(ret_log_decay.astype(F32))
    ret = retention(proj3, log_gamma, ret_norm, *tabs)
    nblk = ret_w // HEAD_DIM
    hm = mlstm(proj3, gates, gate_b, conv_w, conv_b, wq, wk, wv, ml_norm,
               pad=pad, mu_col0=4 * nblk, mo_col0=4 * nblk + ml_w // HEAD_DIM)
    return [ret.reshape(b * lp, ret_w), hm.reshape(b * lp, ml_w)]


def kernel(x, meta_tokens, norm_gains, mlp_w1, mlp_w2, even_w_in, even_w_out, att_q_norm, att_k_norm, s5_lam_re, s5_lam_im, s5_log_dt, s5_b_re, s5_b_im, s5_c_re, s5_c_im, s5_d, s5_glu_w, s5_glu_b, odd_w_in, odd_w_out, ret_log_decay, ret_norm, ml_conv_w, ml_conv_b, ml_wq, ml_wk, ml_wv, ml_gate_b, ml_norm):
    b, n_tok, d_model = x.shape
    l = n_tok + N_META
    pad = (-l) % CHUNK
    lp = l + pad
    depth = norm_gains.shape[0]
    h = jnp.concatenate([jnp.zeros((b, pad, d_model), x.dtype),
                         jnp.broadcast_to(meta_tokens.astype(x.dtype)[None], (b, N_META, d_model)), x], axis=1)
    h = h.reshape(b * lp, d_model)
    axial = _axial_tables(n_tok, pad)
    linear = _linear_tables(l, pad)
    dims = dict(b=b, lp=lp, pad=pad)
    hn = rmsnorm(h, norm_gains[0, 0])
    for i in range(depth):
        j = i // 2
        if i % 2 == 0:
            s5_params = (s5_lam_re[j], s5_lam_im[j], s5_log_dt[j], s5_b_re[j], s5_b_im[j], s5_c_re[j],
                         s5_c_im[j], s5_d[j])
            parts = _even_mixer_parts(hn, even_w_in, j, att_q_norm[j], att_k_norm[j],
                                      s5_params, s5_glu_w[j], s5_glu_b[j], axial, **dims)
            w_out = even_w_out[j]
        else:
            parts = _odd_mixer_parts(hn, odd_w_in, j, ret_log_decay[j], ret_norm[j],
                                     ml_conv_w[j], ml_conv_b[j], ml_wq[j], ml_wk[j], ml_wv[j], ml_gate_b[j],
                                     ml_norm[j], linear, **dims)
            w_out = odd_w_out[j]
        h, hn = matmul_norm_res(parts, w_out.astype(BF16), norm_gains[i, 1], h, norm_gains[i, 2], lp=lp, pad=pad)
        hid = matmul_wcast(hn, mlp_w1, i, mlp_w1.shape[2], relu2=True, out_dtype=BF16)
        next_gain = norm_gains[i + 1, 0] if i + 1 < depth else None
        h, hn = matmul_norm_res([hid], mlp_w2[i].astype(BF16), norm_gains[i, 3], h, next_gain, lp=lp, pad=pad)
    return h.reshape(b, lp, d_model)[:, pad + N_META:]
```

```python
import functools
import math

import jax
import jax.numpy as jnp
from jax import lax
from jax.experimental import pallas as pl
from jax.experimental.pallas import tpu as pltpu

F32 = jnp.float32
BF16 = jnp.bfloat16

N_META = 16
GRID_W = 64
CHUNK = 128
HEAD_DIM = 128
NORM_EPS = 1e-6
ROPE_THETA = 10000.0
ATT_HEADS = 12
ATT_KV_HEADS = 4
ATT_GROUP = ATT_HEADS // ATT_KV_HEADS
S5_GROUP = 16
S5_GROUPS = 32
S5_STATE = 64
S5_T = 16
S5_COLS = S5_T * S5_GROUP
RET_HEADS = 8
ML_HEADS = 8
CONV_W = 5
NEG_GATE = -1e4
HEADS_PER_STEP = 2
BF16_ROWS = 16
VMEM_LIMIT_BYTES = 56 * 1024 * 1024


def _pick(n, cands):
    for c in cands:
        if n % c == 0:
            return c
    raise ValueError(f"no tile for {n} in {cands}")


def _params(*sem):
    return pltpu.CompilerParams(dimension_semantics=sem, vmem_limit_bytes=VMEM_LIMIT_BYTES)


def _dot(a, b):
    return jnp.dot(a, b, preferred_element_type=F32)


def _dot_nt(a, b):
    return lax.dot_general(a, b, (((1,), (1,)), ((), ())), preferred_element_type=F32)


def _dot_tn(a, b):
    return lax.dot_general(a, b, (((0,), (0,)), ((), ())), preferred_element_type=F32)


def _sigmoid(x):
    return 1.0 / (1.0 + jnp.exp(-x))


def _rmsnorm_kernel(x_ref, g_ref, o_ref):
    x = x_ref[...]
    ms = jnp.mean(x * x, axis=-1, keepdims=True)
    o_ref[...] = (x * lax.rsqrt(ms + NORM_EPS) * g_ref[...]).astype(o_ref.dtype)


def rmsnorm(x, gain):
    m, d = x.shape
    tm = _pick(m, (1024, 512, 384, 256, 128))
    return pl.pallas_call(
        _rmsnorm_kernel,
        grid=(m // tm,),
        in_specs=[pl.BlockSpec((tm, d), lambda i: (i, 0)), pl.BlockSpec((1, d), lambda i: (0, 0))],
        out_specs=pl.BlockSpec((tm, d), lambda i: (i, 0)),
        out_shape=jax.ShapeDtypeStruct((m, d), BF16),
        compiler_params=_params("parallel"),
        name="rmsnorm",
    )(x, gain.reshape(1, d).astype(F32))


def _matmul_kernel(a_ref, w_ref, o_ref, *, relu2):
    y = _dot(a_ref[...], w_ref[...])
    if relu2:
        y = jnp.square(jnp.maximum(y, 0.0))
    o_ref[...] = y.astype(o_ref.dtype)


def matmul(a, w, *, relu2=False, out_dtype=F32):
    m, k = a.shape
    n = w.shape[1]
    tm = _pick(m, (1024, 512, 384, 256, 128))
    tn = _pick(n, (2048, 1536, 1280, 1024, 512, 256, 128))
    return pl.pallas_call(
        functools.partial(_matmul_kernel, relu2=relu2),
        grid=(m // tm, n // tn),
        in_specs=[pl.BlockSpec((tm, k), lambda i, j: (i, 0)),
                  pl.BlockSpec((k, tn), lambda i, j: (0, j))],
        out_specs=pl.BlockSpec((tm, tn), lambda i, j: (i, j)),
        out_shape=jax.ShapeDtypeStruct((m, n), out_dtype),
        compiler_params=_params("parallel", "parallel"),
        name="matmul",
    )(a, w)


def _matmul_wcast_kernel(a_ref, w_ref, o_ref, wb_ref, *, relu2):
    @pl.when(pl.program_id(1) == 0)
    def _():
        wb_ref[...] = w_ref[...].astype(BF16)

    y = _dot(a_ref[...], wb_ref[...])
    if relu2:
        y = jnp.square(jnp.maximum(y, 0.0))
    o_ref[...] = y.astype(o_ref.dtype)


def matmul_wcast(a, w3, layer, n_cols, *, relu2=False, out_dtype=F32):
    m, k = a.shape
    tm = _pick(m, (1024, 512, 384, 256, 128))
    tn = _pick(n_cols, (1024, 512, 256, 128))
    return pl.pallas_call(
        functools.partial(_matmul_wcast_kernel, relu2=relu2),
        grid=(n_cols // tn, m // tm),
        in_specs=[pl.BlockSpec((tm, k), lambda j, i: (i, 0)),
                  pl.BlockSpec((None, k, tn), lambda j, i: (layer, 0, j))],
        out_specs=pl.BlockSpec((tm, tn), lambda j, i: (i, j)),
        out_shape=jax.ShapeDtypeStruct((m, n_cols), out_dtype),
        scratch_shapes=[pltpu.VMEM((k, tn), BF16)],
        compiler_params=_params("parallel", "arbitrary"),
        name="matmul_wcast",
    )(a, w3)


def _matmul_norm_res_kernel(*refs, widths, nk, tm, lp, pad, emit_next):
    na = len(widths)
    a_refs = refs[:na]
    w_ref, g_ref, h_ref = refs[na:na + 3]
    rest = refs[na + 3:]
    if emit_next:
        g2_ref, o_ref, n_ref = rest
    else:
        o_ref, = rest
    start = pl.program_id(0) * tm
    row = start + lax.broadcasted_iota(jnp.int32, (tm, 1), 0)
    rel0 = row - (start // lp) * lp
    rel1 = row - ((start + tm - 1) // lp) * lp
    is_pad = ((rel0 >= 0) & (rel0 < pad)) | ((rel1 >= 0) & (rel1 < pad))

    def finish(y):
        ms = jnp.mean(y * y, axis=-1, keepdims=True)
        out = jnp.where(is_pad, 0.0, h_ref[...] + y * lax.rsqrt(ms + NORM_EPS) * g_ref[...])
        o_ref[...] = out
        if emit_next:
            ms2 = jnp.mean(out * out, axis=-1, keepdims=True)
            n_ref[...] = (out * lax.rsqrt(ms2 + NORM_EPS) * g2_ref[...]).astype(n_ref.dtype)

    if nk == 1:
        part = None
        off = 0
        for a_ref, wd in zip(a_refs, widths):
            d = _dot(a_ref[...], w_ref[off:off + wd, :])
            part = d if part is None else part + d
            off += wd
        finish(part)
    else:
        kk = pl.program_id(1)

        @pl.when(kk == 0)
        def _():
            o_ref[...] = jnp.zeros_like(o_ref)

        o_ref[...] += _dot(a_refs[0][...], w_ref[...])

        @pl.when(kk == nk - 1)
        def _():
            finish(o_ref[...])


def matmul_norm_res(parts, w3, layer, gain, h, next_gain, *, lp, pad):
    m, n = h.shape
    widths = tuple(p.shape[1] for p in parts)
    k = sum(widths)
    tm = _pick(m, (512, 384, 256, 128))
    assert tm <= lp
    if len(parts) > 1 or k <= 2048:
        tk, nk = k, 1
    else:
        tk = 2048
        nk = k // tk
        widths = (tk,)
    emit_next = next_gain is not None
    row_blk = pl.BlockSpec((tm, n), lambda i, j: (i, 0))
    vec = pl.BlockSpec((1, n), lambda i, j: (0, 0))
    in_specs = [pl.BlockSpec((tm, wd), lambda i, j: (i, j)) for wd in widths]
    in_specs += [pl.BlockSpec((None, tk, n), lambda i, j: (layer, j, 0)), vec, row_blk]
    args = [*parts, w3, gain.reshape(1, n).astype(F32), h]
    out_specs = [row_blk]
    out_shape = [jax.ShapeDtypeStruct((m, n), F32)]
    if emit_next:
        in_specs.append(vec)
        args.append(next_gain.reshape(1, n).astype(F32))
        out_specs.append(row_blk)
        out_shape.append(jax.ShapeDtypeStruct((m, n), BF16))
    res = pl.pallas_call(
        functools.partial(_matmul_norm_res_kernel, widths=widths, nk=nk, tm=tm, lp=lp, pad=pad,
                          emit_next=emit_next),
        grid=(m // tm, nk),
        in_specs=in_specs,
        out_specs=out_specs,
        out_shape=out_shape,
        compiler_params=_params("parallel", "arbitrary"),
        name="matmul_norm_res",
    )(*args)
    return (res[0], res[1]) if emit_next else (res[0], None)


def _rope_axial(x, c, sa, sb):
    return x * c + pltpu.roll(x, HEAD_DIM - 32, 1) * sa + pltpu.roll(x, 32, 1) * sb


def _attn_kernel(q_ref, k_ref, v_ref, qg_ref, kg_ref, c_ref, sa_ref, sb_ref, o_ref,
                 ks_ref, vt_ref, sta_ref, stb_ref, pa_ref, pb_ref, *, lp, pad, tq):
    k = k_ref[0]
    k = k * lax.rsqrt(jnp.mean(k * k, axis=-1, keepdims=True) + NORM_EPS) * kg_ref[...]
    ks_ref[...] = _rope_axial(k, c_ref[...], sa_ref[...], sb_ref[...]).astype(BF16)
    vt_ref[...] = v_ref[0].T.astype(BF16)
    for p_ref in (pa_ref, pb_ref):
        p_ref[0:pad, :] = jnp.zeros((pad, p_ref.shape[1]), BF16)
    scale = HEAD_DIM ** -0.5 * math.log2(math.e)

    def scores(r0, rows, st_ref):
        c = c_ref[pl.ds(r0, rows), :]
        sa = sa_ref[pl.ds(r0, rows), :]
        sb = sb_ref[pl.ds(r0, rows), :]
        qs = []
        for g in range(ATT_GROUP):
            q = q_ref[0, pl.ds(r0, rows), g * HEAD_DIM:(g + 1) * HEAD_DIM]
            q = q * lax.rsqrt(jnp.mean(q * q, axis=-1, keepdims=True) + NORM_EPS) * qg_ref[...]
            qs.append((_rope_axial(q, c, sa, sb) * scale).astype(BF16))
        qall = jnp.concatenate(qs, axis=0)
        st_ref[:, 0:ATT_GROUP * rows] = _dot_nt(ks_ref[pad:, :], qall)

    def attend(r0, rows, st_ref, p_ref):
        n = ATT_GROUP * rows
        st = st_ref[:, 0:n]
        p = jnp.exp2(st - jnp.max(st, axis=0, keepdims=True))
        l = jnp.sum(p, axis=0, keepdims=True)
        p_ref[pad:, 0:n] = p.astype(BF16)
        o = (_dot(vt_ref[...], p_ref[:, 0:n]) / l).T
        for g in range(ATT_GROUP):
            o_ref[0, pl.ds(r0, rows), g * HEAD_DIM:(g + 1) * HEAD_DIM] = (
                o[g * rows:(g + 1) * rows].astype(o_ref.dtype))

    bufs = ((sta_ref, pa_ref), (stb_ref, pb_ref))
    nbig = lp // tq
    blocks = [(i * tq, tq) for i in range(nbig)]
    if lp % tq:
        blocks.append((nbig * tq, lp % tq))
    npairs = max(0, (nbig - 1) // 2)
    scores(0, blocks[0][1], sta_ref)

    def body(u, carry):
        r, r1, r2 = (pl.multiple_of((2 * u + i) * tq, tq) for i in range(3))
        scores(r1, tq, stb_ref)
        attend(r, tq, sta_ref, pa_ref)
        scores(r2, tq, sta_ref)
        attend(r1, tq, stb_ref, pb_ref)
        return carry

    lax.fori_loop(0, npairs, body, 0)
    for i in range(2 * npairs, len(blocks)):
        if i + 1 < len(blocks):
            scores(*blocks[i + 1], bufs[(i + 1) % 2][0])
        attend(*blocks[i], *bufs[i % 2])


def attention(qkvu3, q_gain, k_gain, tabs, *, pad):
    b, lp, _ = qkvu3.shape
    tq = 2 * CHUNK
    gw = ATT_GROUP * HEAD_DIM
    k_col0 = ATT_HEADS
    v_col0 = ATT_HEADS + ATT_KV_HEADS
    full = lambda bi, hi: (0, 0)
    st = pltpu.VMEM((lp - pad, ATT_GROUP * tq), F32)
    pb = pltpu.VMEM((lp, ATT_GROUP * tq), BF16)
    return pl.pallas_call(
        functools.partial(_attn_kernel, lp=lp, pad=pad, tq=tq),
        grid=(b, ATT_KV_HEADS),
        in_specs=[pl.BlockSpec((1, lp, gw), lambda bi, hi: (bi, 0, hi)),
                  pl.BlockSpec((1, lp, HEAD_DIM), lambda bi, hi: (bi, 0, k_col0 + hi)),
                  pl.BlockSpec((1, lp, HEAD_DIM), lambda bi, hi: (bi, 0, v_col0 + hi)),
                  pl.BlockSpec((1, HEAD_DIM), full),
                  pl.BlockSpec((1, HEAD_DIM), full),
                  pl.BlockSpec((lp, HEAD_DIM), full),
                  pl.BlockSpec((lp, HEAD_DIM), full),
                  pl.BlockSpec((lp, HEAD_DIM), full)],
        out_specs=pl.BlockSpec((1, lp, gw), lambda bi, hi: (bi, 0, hi)),
        out_shape=jax.ShapeDtypeStruct((b, lp, ATT_HEADS * HEAD_DIM), BF16),
        scratch_shapes=[pltpu.VMEM((lp, HEAD_DIM), BF16), pltpu.VMEM((HEAD_DIM, lp), BF16), st, st, pb, pb],
        compiler_params=_params("parallel", "parallel"),
        name="attention",
    )(qkvu3, qkvu3, qkvu3, q_gain.reshape(1, HEAD_DIM).astype(F32), k_gain.reshape(1, HEAD_DIM).astype(F32),
      *tabs)


def _s5_kernel(u_ref, k_ref, w_ref, v_ref, at_ref, y_ref, s_ref, x_ref, *, nchunk, nb):
    u = u_ref[0]
    s_ref[...] = _dot(u, w_ref[0])
    at = at_ref[0]
    afr, afi, abr, abi = (at[:, i * CHUNK:(i + 1) * CHUNK] for i in range(4))

    def body(c, carry):
        xfr, xfi, xbr, xbi = carry
        rf = pl.multiple_of(c * nb, nb)
        rb = pl.multiple_of((nchunk - 1 - c) * nb, nb)
        x_ref[pl.ds(rf, nb), 0:CHUNK] = xfr
        x_ref[pl.ds(rf, nb), CHUNK:2 * CHUNK] = xfi
        x_ref[pl.ds(rb, nb), 2 * CHUNK:3 * CHUNK] = xbr
        x_ref[pl.ds(rb, nb), 3 * CHUNK:4 * CHUNK] = xbi
        sfr = s_ref[pl.ds(rf, nb), 0:CHUNK]
        sfi = s_ref[pl.ds(rf, nb), CHUNK:2 * CHUNK]
        sbr = s_ref[pl.ds(rb, nb), 2 * CHUNK:3 * CHUNK]
        sbi = s_ref[pl.ds(rb, nb), 3 * CHUNK:4 * CHUNK]
        return (afr * xfr - afi * xfi + sfr, afr * xfi + afi * xfr + sfi,
                abr * xbr - abi * xbi + sbr, abr * xbi + abi * xbr + sbi)

    z = jnp.zeros((nb, CHUNK), F32)
    lax.fori_loop(0, nchunk, body, (z, z, z, z))
    x = _dot(u, k_ref[0]) + _dot(x_ref[...].astype(BF16), v_ref[0])
    y = x * (0.5 * (1.0 + jnp.tanh(math.sqrt(2.0 / math.pi) * (x + 0.044715 * (x * x * x)))))
    y_ref[0] = y.astype(y_ref.dtype)


def _s5_matrices(lam_re, lam_im, log_dt, b_re, b_im, c_re, c_im, d_skip):
    hi = lax.Precision.HIGHEST
    t = S5_T
    lr = jnp.minimum(lam_re, -1e-4)
    li = lam_im
    dt = jnp.exp(log_dt)[..., None]
    er = jnp.exp(lr * dt)
    abar_re = er * jnp.cos(li * dt)
    abar_im = er * jnp.sin(li * dt)
    nr = abar_re - 1.0
    den = lr * lr + li * li
    coef_re = (nr * lr + abar_im * li) / den
    coef_im = (abar_im * lr - nr * li) / den
    bb_re = coef_re[..., None] * b_re - coef_im[..., None] * b_im
    bb_im = coef_re[..., None] * b_im + coef_im[..., None] * b_re
    kk = jnp.arange(t + 1, dtype=F32)[:, None, None, None]
    mag = jnp.exp(kk * (lr * dt)[None])
    pw_re = mag * jnp.cos(kk * (li * dt)[None])
    pw_im = mag * jnp.sin(kk * (li * dt)[None])
    g, hh = d_skip.shape
    ct_re = c_re.transpose(0, 1, 3, 2)
    ct_im = c_im.transpose(0, 1, 3, 2)
    flat = lambda a: a.reshape(2, g, S5_STATE, hh * hh)
    bc_re = flat(bb_re[..., :, None] * ct_re[..., None, :] - bb_im[..., :, None] * ct_im[..., None, :])
    bc_im = flat(bb_re[..., :, None] * ct_im[..., None, :] + bb_im[..., :, None] * ct_re[..., None, :])
    lagk = (jnp.einsum('kdgp,dgpn->dgkn', pw_re, bc_re, precision=hi)
            - jnp.einsum('kdgp,dgpn->dgkn', pw_im, bc_im, precision=hi))
    ti = jnp.arange(t)
    lag = ti[None, :] - ti[:, None]
    sel = lambda m: m[None, :, :, None]
    skip = (jnp.eye(hh, dtype=F32)[None] * d_skip[:, None, :]).reshape(g, 1, 1, hh * hh)
    ktot = (jnp.where(sel(lag >= 0), lagk[0][:, jnp.clip(lag, 0, t)], 0.0)
            + jnp.where(sel(lag <= 0), lagk[1][:, jnp.clip(-lag, 0, t)], 0.0)
            + jnp.where(sel(lag == 0), skip, 0.0))
    ktot = ktot.reshape(g, t, t, hh, hh).transpose(0, 1, 3, 2, 4).reshape(g, t * hh, t * hh)

    def bsum(pw_r, pw_i, d):
        wr = pw_r[:, :, :, None] * bb_re[d][None] - pw_i[:, :, :, None] * bb_im[d][None]
        wi = pw_r[:, :, :, None] * bb_im[d][None] + pw_i[:, :, :, None] * bb_re[d][None]
        tr = lambda a: a.transpose(1, 0, 3, 2).reshape(g, t * hh, S5_STATE)
        return tr(wr), tr(wi)

    wf_re, wf_im = bsum(pw_re[:t, 0][::-1], pw_im[:t, 0][::-1], 0)
    wb_re, wb_im = bsum(pw_re[:t, 1], pw_im[:t, 1], 1)
    padl = lambda a: jnp.pad(a, ((0, 0), (0, 0), (0, CHUNK - S5_STATE)))
    wtot = jnp.concatenate([padl(wf_re), padl(wf_im), padl(wb_re), padl(wb_im)], axis=-1)

    def vmat(pw_r, pw_i, d):
        vr = pw_r[:, :, None, :] * c_re[d][None] - pw_i[:, :, None, :] * c_im[d][None]
        vi = pw_r[:, :, None, :] * c_im[d][None] + pw_i[:, :, None, :] * c_re[d][None]
        tr = lambda a: a.transpose(1, 3, 0, 2).reshape(g, S5_STATE, t * hh)
        return tr(vr), tr(-vi)

    vf_re, vf_im = vmat(pw_re[1:, 0], pw_im[1:, 0], 0)
    vb_re, vb_im = vmat(pw_re[1:, 1][::-1], pw_im[1:, 1][::-1], 1)
    padr = lambda a: jnp.pad(a, ((0, 0), (0, CHUNK - S5_STATE), (0, 0)))
    vtot = jnp.concatenate([padr(vf_re), padr(vf_im), padr(vb_re), padr(vb_im)], axis=1)
    padv = lambda a: jnp.pad(a, ((0, 0), (0, CHUNK - S5_STATE)))
    at = jnp.concatenate([padv(pw_re[t, 0]), padv(pw_im[t, 0]), padv(pw_re[t, 1]), padv(pw_im[t, 1])], axis=-1)
    return ktot.astype(BF16), wtot.astype(BF16), vtot.astype(BF16), at[:, None, :]


def s5_scan(u, mats):
    b, lp, _ = u.shape
    nchunk = lp // S5_T
    ktot, wtot, vtot, at = mats
    ug = u.astype(BF16).reshape(b, nchunk, S5_T, S5_GROUPS, S5_GROUP).transpose(3, 1, 0, 2, 4)
    ug = ug.reshape(S5_GROUPS, nchunk * b, S5_COLS)
    rows = nchunk * b
    per_g = lambda g: (g, 0, 0)
    y = pl.pallas_call(
        functools.partial(_s5_kernel, nchunk=nchunk, nb=b),
        grid=(S5_GROUPS,),
        in_specs=[pl.BlockSpec((1, rows, S5_COLS), per_g),
                  pl.BlockSpec((1, S5_COLS, S5_COLS), per_g),
                  pl.BlockSpec((1, S5_COLS, 4 * CHUNK), per_g),
                  pl.BlockSpec((1, 4 * CHUNK, S5_COLS), per_g),
                  pl.BlockSpec((1, 1, 4 * CHUNK), per_g)],
        out_specs=pl.BlockSpec((1, rows, S5_COLS), per_g),
        out_shape=jax.ShapeDtypeStruct((S5_GROUPS, rows, S5_COLS), BF16),
        scratch_shapes=[pltpu.VMEM((rows, 4 * CHUNK), F32), pltpu.VMEM((rows, 4 * CHUNK), F32)],
        compiler_params=_params("parallel"),
        name="s5_scan",
    )(ug, ktot, wtot, vtot, at)
    y = y.reshape(S5_GROUPS, nchunk, b, S5_T, S5_GROUP).transpose(2, 1, 3, 0, 4)
    return y.reshape(b * lp, S5_GROUPS * S5_GROUP)


def _s5_glu_kernel(y_ref, w_ref, b_ref, o_ref):
    y = y_ref[...]
    z = _dot(y, w_ref[...]) + b_ref[...]
    o_ref[...] = (y.astype(F32) * _sigmoid(z)).astype(o_ref.dtype)


def s5_glu(y, w, bias):
    m, n = y.shape
    tm = _pick(m, (1024, 512, 384, 256, 128))
    return pl.pallas_call(
        _s5_glu_kernel,
        grid=(m // tm,),
        in_specs=[pl.BlockSpec((tm, n), lambda i: (i, 0)),
                  pl.BlockSpec((n, n), lambda i: (0, 0)),
                  pl.BlockSpec((1, n), lambda i: (0, 0))],
        out_specs=pl.BlockSpec((tm, n), lambda i: (i, 0)),
        out_shape=jax.ShapeDtypeStruct((m, n), BF16),
        compiler_params=_params("parallel"),
        name="s5_glu",
    )(y, w, bias.reshape(1, n).astype(F32))


def _head_norm(x, gain):
    xc = x - jnp.mean(x, axis=-1, keepdims=True)
    return xc * lax.rsqrt(jnp.mean(xc * xc, axis=-1, keepdims=True) + NORM_EPS) * gain


def _ret_kernel(q_ref, k_ref, v_ref, g_ref, c_ref, s_ref, lg_ref, gn_ref, o_ref,
                qs_ref, ks_ref, vs_ref, vt_ref, af_ref, ab_ref, st_ref, *, nchunk):
    c = c_ref[...]
    s = s_ref[...]
    ii = lax.broadcasted_iota(jnp.int32, (CHUNK, CHUNK), 0).astype(F32)
    jj = lax.broadcasted_iota(jnp.int32, (CHUNK, CHUNK), 1).astype(F32)
    lane = (lax.broadcasted_iota(jnp.int32, (1, nchunk * CHUNK), 1) & (CHUNK - 1)).astype(F32)
    diff = ii - jj
    lanes = [slice(hh * HEAD_DIM, (hh + 1) * HEAD_DIM) for hh in range(HEADS_PER_STEP)]
    consts = []
    for hh, ln in enumerate(lanes):
        q = q_ref[0, :, ln]
        qs_ref[:, ln] = ((q * c + pltpu.roll(q, HEAD_DIM // 2, 1) * s) * HEAD_DIM ** -0.5).astype(BF16)
        k = k_ref[0, :, ln]
        ks_ref[:, ln] = (k * c + pltpu.roll(k, HEAD_DIM // 2, 1) * s).astype(BF16)
        v = v_ref[0, :, ln]
        vs_ref[:, ln] = v.astype(BF16)
        v_t = v.T
        lgf = lg_ref[hh, 0:1, :]
        lgb = lg_ref[hh, 1:2, :]
        vt_ref[2 * hh] = (v_t * jnp.exp((CHUNK - 1 - lane) * lgf[:, 0:1])).astype(BF16)
        vt_ref[2 * hh + 1] = (v_t * jnp.exp(lane * lgb[:, 0:1])).astype(BF16)
        fwd = (jnp.where(diff >= 0, jnp.exp(jnp.where(diff >= 0, diff, 0.0) * lgf), 0.0),
               jnp.exp((ii + 1.0) * lgf), jnp.exp(CHUNK * lgf))
        bwd = (jnp.where(diff < 0, jnp.exp(jnp.where(diff < 0, -diff, 0.0) * lgb), 0.0),
               jnp.exp((CHUNK - ii) * lgb), jnp.exp(CHUNK * lgb))
        consts.append((fwd, bwd))
    st_ref[...] = jnp.zeros_like(st_ref)

    def chunk(r, ln, slot, cst, out_ref):
        dec, xi, gc = cst
        qc = qs_ref[pl.ds(r, CHUNK), ln]
        kc = ks_ref[pl.ds(r, CHUNK), ln]
        vc = vs_ref[pl.ds(r, CHUNK), ln]
        state_t = st_ref[slot]
        both = _dot_nt(qc, jnp.concatenate([kc, state_t.astype(BF16)], axis=0))
        sc = both[:, :CHUNK] * dec
        res = _dot(jnp.concatenate([sc.astype(BF16), vt_ref[slot, :, pl.ds(r, CHUNK)]], axis=0),
                   jnp.concatenate([vc, kc], axis=1))
        out_ref[pl.ds(r, CHUNK), ln] = res[:CHUNK, :HEAD_DIM] + both[:, CHUNK:] * xi
        st_ref[slot] = gc * state_t + res[CHUNK:, HEAD_DIM:]

    def body(t, carry):
        rf = pl.multiple_of(t * CHUNK, CHUNK)
        rb = pl.multiple_of((nchunk - 1 - t) * CHUNK, CHUNK)
        for hh, ln in enumerate(lanes):
            chunk(rf, ln, 2 * hh, consts[hh][0], af_ref)
            chunk(rb, ln, 2 * hh + 1, consts[hh][1], ab_ref)
        return carry

    lax.fori_loop(0, nchunk, body, 0)
    for ln in lanes:
        gate = g_ref[0, :, ln]
        y = _head_norm(af_ref[:, ln] + ab_ref[:, ln], gn_ref[:, ln])
        o_ref[0, :, ln] = (y * (gate * _sigmoid(gate))).astype(o_ref.dtype)


def retention(proj3, log_gamma, gain, cos_t, sin_t):
    b, lp, _ = proj3.shape
    hd = HEAD_DIM
    hps = HEADS_PER_STEP
    wd = hps * hd
    nblk = RET_HEADS // hps
    lg = jnp.broadcast_to(log_gamma.T[:, :, None], (RET_HEADS, 2, hd)).astype(F32)
    blk = lambda off: pl.BlockSpec((1, lp, wd), lambda bi, hi: (bi, 0, off + hi))
    full = lambda bi, hi: (0, 0)
    return pl.pallas_call(
        functools.partial(_ret_kernel, nchunk=lp // CHUNK),
        grid=(b, nblk),
        in_specs=[blk(0), blk(nblk), blk(2 * nblk), blk(3 * nblk),
                  pl.BlockSpec((lp, hd), full), pl.BlockSpec((lp, hd), full),
                  pl.BlockSpec((hps, 2, hd), lambda bi, hi: (hi, 0, 0)),
                  pl.BlockSpec((1, wd), lambda bi, hi: (0, hi))],
        out_specs=pl.BlockSpec((1, lp, wd), lambda bi, hi: (bi, 0, hi)),
        out_shape=jax.ShapeDtypeStruct((b, lp, RET_HEADS * hd), BF16),
        scratch_shapes=[pltpu.VMEM((lp, wd), BF16), pltpu.VMEM((lp, wd), BF16), pltpu.VMEM((lp, wd), BF16),
                        pltpu.VMEM((2 * hps, hd, lp), BF16), pltpu.VMEM((lp, wd), F32), pltpu.VMEM((lp, wd), F32),
                        pltpu.VMEM((2 * hps, hd, hd), F32)],
        compiler_params=_params("parallel", "parallel"),
        name="retention",
    )(proj3, proj3, proj3, proj3, cos_t, sin_t, lg, gain.reshape(1, RET_HEADS * hd).astype(F32))


def _mlstm_gate_tables(li, lf, tri_sum):
    bt = jnp.dot(lf, tri_sum, preferred_element_type=F32, precision=lax.Precision.HIGHEST)
    bt_last = jnp.sum(lf, axis=1, keepdims=True)
    a = bt_last - bt + li
    m_loc = jnp.max(a, axis=1, keepdims=True)
    return bt, jnp.exp(a - m_loc), m_loc, bt_last, li - bt


def _mlstm_chunk_t(kc, qtc, vtc, vwc, bt_row, w_row, m_loc, bt_last, colb, mask_t, ct_prev, n_prev, m_prev):
    d, c, pk = HEAD_DIM, CHUNK, BF16_ROWS
    dlog_t = jnp.where(mask_t, bt_row + colb, -jnp.inf)
    g_row = bt_row + m_prev
    m_t = jnp.maximum(g_row, jnp.max(dlog_t, axis=0, keepdims=True))
    r1 = _dot(jnp.concatenate([kc, ct_prev.astype(BF16), jnp.broadcast_to(n_prev, (pk, d)).astype(BF16)], axis=0),
              qtc)
    s_t = r1[:c] * jnp.exp(dlog_t - m_t)
    w_int = jnp.exp(g_row - m_t)
    r2 = _dot(jnp.concatenate([vtc, vwc, jnp.broadcast_to(w_row, (pk, c)).astype(BF16)], axis=0),
              jnp.concatenate([s_t.astype(BF16), kc], axis=1))
    num_t = r2[:d, :c] + w_int * r1[c:c + d]
    den_t = jnp.sum(s_t, axis=0, keepdims=True) + w_int * r1[c + d:c + d + 1]
    out_t = num_t / jnp.maximum(jnp.abs(den_t), jnp.exp(-m_t))
    m_new = jnp.maximum(bt_last + m_prev, m_loc)
    f_prev = jnp.exp(bt_last + m_prev - m_new)
    f_loc = jnp.exp(m_loc - m_new)
    c_new = f_prev * ct_prev + f_loc * r2[d:2 * d, c:]
    n_new = f_prev * n_prev + f_loc * r2[2 * d:2 * d + 1, c:]
    return out_t, c_new, n_new, m_new


def _mlstm_kernel(mu_ref, mo_ref, gt_ref, gb_ref, cw_ref, cb_ref, wk_ref, wqt_ref, wvt_ref, gn_ref, o_ref,
                  ks_ref, qt_ref, vt_ref, vw_ref, row_ref, d_ref, colb_ref, aft_ref, abt_ref,
                  cs_ref, ns_ref, ms_ref, *, lp, pad, nchunk):
    hd = HEAD_DIM
    lanes = [slice(hh * hd, (hh + 1) * hd) for hh in range(HEADS_PER_STEP)]
    valid_row = lax.broadcasted_iota(jnp.int32, (lp, 1), 0) >= pad
    valid_col = lax.broadcasted_iota(jnp.int32, (1, lp), 1) >= pad
    pos = (lax.broadcasted_iota(jnp.int32, (nchunk, CHUNK), 0) * CHUNK
           + lax.broadcasted_iota(jnp.int32, (nchunk, CHUNK), 1))
    valid_pos = pos >= pad
    ii = lax.broadcasted_iota(jnp.int32, (CHUNK, CHUNK), 0)
    jj = lax.broadcasted_iota(jnp.int32, (CHUNK, CHUNK), 1)
    eye = ii == jj
    upper = ii <= jj
    lower = ii >= jj
    ones = jnp.ones((CHUNK, CHUNK), BF16)
    for hh, ln in enumerate(lanes):
        mu = mu_ref[0, :, ln]
        conv = cb_ref[:, ln]
        for j in range(CONV_W):
            conv = conv + cw_ref[j:j + 1, ln] * pltpu.roll(mu, (CONV_W // 2 - j) % lp, 0)
        uc = (conv * _sigmoid(conv)).astype(BF16)
        ks_ref[:, ln] = jnp.where(valid_row, _dot(uc, wk_ref[hh]) * hd ** -0.5, 0.0).astype(BF16)
        qt_ref[ln, :] = jnp.where(valid_col, _dot_nt(wqt_ref[hh], uc), 0.0).astype(BF16)
        v_t = jnp.where(valid_col, _dot_nt(wvt_ref[hh], mu.astype(BF16)), 0.0)
        vt_ref[ln, :] = v_t.astype(BF16)

        for d, tri_sum in enumerate((upper, lower)):
            g_i = gt_ref[0, hh, 2 * d] + gb_ref[hh, 2 * d:2 * d + 1, 0:1]
            g_f = gt_ref[0, hh, 2 * d + 1] + gb_ref[hh, 2 * d + 1:2 * d + 2, 0:1]
            li = jnp.where(valid_pos, g_i, NEG_GATE)
            lf = jnp.where(valid_pos, jnp.minimum(g_f, 0.0) - jnp.log(1.0 + jnp.exp(-jnp.abs(g_f))), 0.0)
            bt, w, m_loc, bt_last, colv = _mlstm_gate_tables(li, lf, jnp.where(tri_sum, 1.0, 0.0))
            slot = 2 * hh + d
            row_ref[4 * slot + 0] = bt
            row_ref[4 * slot + 1] = w
            row_ref[4 * slot + 2] = jnp.broadcast_to(m_loc, (nchunk, CHUNK))
            row_ref[4 * slot + 3] = jnp.broadcast_to(bt_last, (nchunk, CHUNK))
            for n in range(nchunk):
                cols = slice(n * CHUNK, (n + 1) * CHUNK)
                vw_ref[slot, :, cols] = (v_t[:, cols] * w[n:n + 1, :]).astype(BF16)
                d_ref[cols, :] = jnp.where(eye, colv[n:n + 1, :], 0.0)
            diag = d_ref[...]
            d_hi = diag.astype(BF16)
            d_lo = (diag - d_hi.astype(F32)).astype(BF16)
            colb_ref[slot] = _dot(d_hi, ones) + _dot(d_lo, ones)
    cs_ref[...] = jnp.zeros_like(cs_ref)
    ns_ref[...] = jnp.zeros_like(ns_ref)
    ms_ref[...] = jnp.zeros_like(ms_ref)

    def run(n, r, ln, slot, mask_t, out_ref):
        row = lambda kind: row_ref[4 * slot + kind, pl.ds(n, 1), :]
        out_t, c_s, n_s, m_s = _mlstm_chunk_t(
            ks_ref[pl.ds(r, CHUNK), ln], qt_ref[ln, pl.ds(r, CHUNK)], vt_ref[ln, pl.ds(r, CHUNK)],
            vw_ref[slot, :, pl.ds(r, CHUNK)],
            row(0), row(1), row(2)[:, 0:1], row(3)[:, 0:1], colb_ref[slot, pl.ds(r, CHUNK), :], mask_t,
            cs_ref[slot], ns_ref[slot], ms_ref[slot][:, 0:1])
        out_ref[ln, pl.ds(r, CHUNK)] = out_t
        cs_ref[slot] = c_s
        ns_ref[slot] = n_s
        ms_ref[slot] = jnp.broadcast_to(m_s, (1, hd))

    def body(t, carry):
        tb = nchunk - 1 - t
        rf = pl.multiple_of(t * CHUNK, CHUNK)
        rb = pl.multiple_of(tb * CHUNK, CHUNK)
        for hh, ln in enumerate(lanes):
            run(t, rf, ln, 2 * hh, upper, aft_ref)
            run(tb, rb, ln, 2 * hh + 1, lower, abt_ref)
        return carry

    lax.fori_loop(0, nchunk, body, 0)
    for ln in lanes:
        x_t = aft_ref[ln, :] + abt_ref[ln, :]
        xc = x_t - jnp.mean(x_t, axis=0, keepdims=True)
        y = (xc * lax.rsqrt(jnp.mean(xc * xc, axis=0, keepdims=True) + NORM_EPS)).T
        o_ref[0, :, ln] = (y * gn_ref[:, ln] * _sigmoid(mo_ref[0, :, ln])).astype(o_ref.dtype)


def mlstm(proj3, gates, gate_b, conv_w, conv_b, wq, wk, wv, gain, *, pad, mu_col0, mo_col0):
    b, lp, _ = proj3.shape
    hd = HEAD_DIM
    hps = HEADS_PER_STEP
    wd = hps * hd
    nchunk = lp // CHUNK
    gt = gates.reshape(b, lp, 4, ML_HEADS).transpose(0, 3, 2, 1).reshape(b, ML_HEADS, 4, nchunk, CHUNK)
    gb = jnp.broadcast_to(gate_b.T[:, :, None], (ML_HEADS, 4, hd)).astype(F32)
    blk = lambda off: pl.BlockSpec((1, lp, wd), lambda bi, hi: (bi, 0, off // hps + hi))
    per_h = lambda bi, hi: (hi, 0, 0)
    vec = pl.BlockSpec((1, wd), lambda bi, hi: (0, hi))
    sq = pl.BlockSpec((hps, hd, hd), per_h)
    tr = lambda w: jnp.swapaxes(w, 1, 2).astype(BF16)
    return pl.pallas_call(
        functools.partial(_mlstm_kernel, lp=lp, pad=pad, nchunk=nchunk),
        grid=(b, ML_HEADS // hps),
        in_specs=[blk(mu_col0), blk(mo_col0),
                  pl.BlockSpec((1, hps, 4, nchunk, CHUNK), lambda bi, hi: (bi, hi, 0, 0, 0)),
                  pl.BlockSpec((hps, 4, hd), per_h),
                  pl.BlockSpec((CONV_W, wd), lambda bi, hi: (0, hi)),
                  vec, sq, sq, sq, vec],
        out_specs=pl.BlockSpec((1, lp, wd), lambda bi, hi: (bi, 0, hi)),
        out_shape=jax.ShapeDtypeStruct((b, lp, ML_HEADS * hd), BF16),
        scratch_shapes=[pltpu.VMEM((lp, wd), BF16), pltpu.VMEM((wd, lp), BF16), pltpu.VMEM((wd, lp), BF16),
                        pltpu.VMEM((2 * hps, hd, lp), BF16),
                        pltpu.VMEM((8 * hps, nchunk, CHUNK), F32), pltpu.VMEM((lp, CHUNK), F32),
                        pltpu.VMEM((2 * hps, lp, CHUNK), F32),
                        pltpu.VMEM((wd, lp), F32), pltpu.VMEM((wd, lp), F32),
                        pltpu.VMEM((2 * hps, hd, hd), F32), pltpu.VMEM((2 * hps, 1, hd), F32),
                        pltpu.VMEM((2 * hps, 1, hd), F32)],
        compiler_params=_params("parallel", "parallel"),
        name="mlstm",
    )(proj3, proj3, gt, gb, conv_w.astype(F32), conv_b.reshape(1, -1).astype(F32),
      wk.astype(BF16), tr(wq), tr(wv), gain.reshape(1, -1).astype(F32))


def _rope_freqs(dim):
    return ROPE_THETA ** (-jnp.arange(dim // 2, dtype=F32) / (dim // 2))


def _axial_tables(n_tok, pad):
    rows = n_tok // GRID_W
    row = jnp.concatenate([jnp.zeros((pad,), F32), -jnp.ones((N_META,), F32),
                           jnp.repeat(jnp.arange(rows, dtype=F32), GRID_W)])
    col = jnp.concatenate([jnp.zeros((pad,), F32), jnp.arange(N_META, dtype=F32),
                           jnp.tile(jnp.arange(GRID_W, dtype=F32), rows)])
    f = _rope_freqs(HEAD_DIM // 2)
    ang = jnp.concatenate([row[:, None] * f[None, :]] * 2 + [col[:, None] * f[None, :]] * 2, axis=-1)
    first = (jnp.arange(HEAD_DIM) % 64) < 32
    sin = jnp.sin(ang)
    return jnp.cos(ang), jnp.where(first, -sin, 0.0), jnp.where(first, 0.0, sin)


def _linear_tables(l, pad):
    pos = jnp.concatenate([jnp.zeros((pad,), F32), jnp.arange(l, dtype=F32)])
    ang = pos[:, None] * _rope_freqs(HEAD_DIM)[None, :]
    ang = jnp.concatenate([ang, ang], axis=-1)
    sin = jnp.sin(ang)
    return jnp.cos(ang), jnp.where(jnp.arange(HEAD_DIM) < HEAD_DIM // 2, -sin, sin)


def _even_mixer_parts(hn, w_in_all, j, q_norm, k_norm, s5_params, glu_w, glu_b, tabs, *, b, lp, pad):
    att_w = ATT_HEADS * HEAD_DIM
    u0 = att_w + 2 * ATT_KV_HEADS * HEAD_DIM
    qkvu = matmul_wcast(hn, w_in_all, j, w_in_all.shape[2])
    qkvu3 = qkvu.reshape(b, lp, -1)
    att = attention(qkvu3, q_norm, k_norm, tabs, pad=pad).reshape(b * lp, att_w)
    y = s5_scan(qkvu3[:, :, u0:], _s5_matrices(*s5_params))
    ssm = s5_glu(y, glu_w.astype(BF16), glu_b)
    return [att, ssm]


def _odd_mixer_parts(hn, w_in_all, j, ret_log_decay, ret_norm, conv_w, conv_b, wq, wk, wv, gate_b, ml_norm,
                     tabs, *, b, lp, pad):
    ret_w = RET_HEADS * HEAD_DIM
    ml_w = ML_HEADS * HEAD_DIM
    main = 4 * ret_w + 2 * ml_w
    n_gate = w_in_all.shape[2] - main
    proj = matmul_wcast(hn, w_in_all[:, :, :main], j, main)
    proj3 = proj.reshape(b, lp, main)
    w_gate = jnp.pad(w_in_all[j, :, main:].astype(BF16), ((0, 0), (0, CHUNK - n_gate)))
    gates = matmul(hn, w_gate)[:, :n_gate].reshape(b, lp, n_gate)
    log_gamma = -jnp.abs(ret_log_decay.astype(F32))
    ret = retention(proj3, log_gamma, ret_norm, *tabs)
    nblk = ret_w // HEAD_DIM
    hm = mlstm(proj3, gates, gate_b, conv_w, conv_b, wq, wk, wv, ml_norm,
               pad=pad, mu_col0=4 * nblk, mo_col0=4 * nblk + ml_w // HEAD_DIM)
    return [ret.reshape(b * lp, ret_w), hm.reshape(b * lp, ml_w)]


def kernel(x, meta_tokens, norm_gains, mlp_w1, mlp_w2, even_w_in, even_w_out, att_q_norm, att_k_norm, s5_lam_re, s5_lam_im, s5_log_dt, s5_b_re, s5_b_im, s5_c_re, s5_c_im, s5_d, s5_glu_w, s5_glu_b, odd_w_in, odd_w_out, ret_log_decay, ret_norm, ml_conv_w, ml_conv_b, ml_wq, ml_wk, ml_wv, ml_gate_b, ml_norm):
    b, n_tok, d_model = x.shape
    l = n_tok + N_META
    pad = (-l) % CHUNK
    lp = l + pad
    depth = norm_gains.shape[0]
    h = jnp.concatenate([jnp.zeros((b, pad, d_model), x.dtype),
                         jnp.broadcast_to(meta_tokens.astype(x.dtype)[None], (b, N_META, d_model)), x], axis=1)
    h = h.reshape(b * lp, d_model)
    axial = _axial_tables(n_tok, pad)
    linear = _linear_tables(l, pad)
    dims = dict(b=b, lp=lp, pad=pad)
    hn = rmsnorm(h, norm_gains[0, 0])
    w2_bf16 = mlp_w2.astype(BF16)
    even_out_bf16 = even_w_out.astype(BF16)
    odd_out_bf16 = odd_w_out.astype(BF16)
    for i in range(depth):
        j = i // 2
        if i % 2 == 0:
            s5_params = (s5_lam_re[j], s5_lam_im[j], s5_log_dt[j], s5_b_re[j], s5_b_im[j], s5_c_re[j],
                         s5_c_im[j], s5_d[j])
            parts = _even_mixer_parts(hn, even_w_in, j, att_q_norm[j], att_k_norm[j],
                                      s5_params, s5_glu_w[j], s5_glu_b[j], axial, **dims)
            w_out = even_out_bf16
        else:
            parts = _odd_mixer_parts(hn, odd_w_in, j, ret_log_decay[j], ret_norm[j],
                                     ml_conv_w[j], ml_conv_b[j], ml_wq[j], ml_wk[j], ml_wv[j], ml_gate_b[j],
                                     ml_norm[j], linear, **dims)
            w_out = odd_out_bf16
        h, hn = matmul_norm_res(parts, w_out, j, norm_gains[i, 1], h, norm_gains[i, 2], lp=lp, pad=pad)
        hid = matmul_wcast(hn, mlp_w1, i, mlp_w1.shape[2], relu2=True, out_dtype=BF16)
        next_gain = norm_gains[i + 1, 0] if i + 1 < depth else None
        h, hn = matmul_norm_res([hid], w2_bf16, i, norm_gains[i, 3], h, next_gain, lp=lp, pad=pad)
    return h.reshape(b, lp, d_model)[:, pad + N_META:]
```

```python
import functools
import math

import jax
import jax.numpy as jnp
from jax import lax
from jax.experimental import pallas as pl
from jax.experimental.pallas import tpu as pltpu

F32 = jnp.float32
BF16 = jnp.bfloat16

N_META = 16
GRID_W = 64
CHUNK = 128
HEAD_DIM = 128
NORM_EPS = 1e-6
ROPE_THETA = 10000.0
ATT_HEADS = 12
ATT_KV_HEADS = 4
ATT_GROUP = ATT_HEADS // ATT_KV_HEADS
S5_GROUP = 16
S5_GROUPS = 32
S5_STATE = 64
S5_T = 16
S5_COLS = S5_T * S5_GROUP
RET_HEADS = 8
ML_HEADS = 8
CONV_W = 5
NEG_GATE = -1e4
HEADS_PER_STEP = 2
BF16_ROWS = 16
VMEM_LIMIT_BYTES = 56 * 1024 * 1024


def _pick(n, cands):
    for c in cands:
        if n % c == 0:
            return c
    raise ValueError(f"no tile for {n} in {cands}")


def _params(*sem):
    return pltpu.CompilerParams(dimension_semantics=sem, vmem_limit_bytes=VMEM_LIMIT_BYTES)


def _dot(a, b):
    return jnp.dot(a, b, preferred_element_type=F32)


def _dot_nt(a, b):
    return lax.dot_general(a, b, (((1,), (1,)), ((), ())), preferred_element_type=F32)


def _dot_tn(a, b):
    return lax.dot_general(a, b, (((0,), (0,)), ((), ())), preferred_element_type=F32)


def _sigmoid(x):
    return 1.0 / (1.0 + jnp.exp(-x))


def _rmsnorm_kernel(x_ref, g_ref, o_ref):
    x = x_ref[...]
    ms = jnp.mean(x * x, axis=-1, keepdims=True)
    o_ref[...] = (x * lax.rsqrt(ms + NORM_EPS) * g_ref[...]).astype(o_ref.dtype)


def rmsnorm(x, gain):
    m, d = x.shape
    tm = _pick(m, (1024, 512, 384, 256, 128))
    return pl.pallas_call(
        _rmsnorm_kernel,
        grid=(m // tm,),
        in_specs=[pl.BlockSpec((tm, d), lambda i: (i, 0)), pl.BlockSpec((1, d), lambda i: (0, 0))],
        out_specs=pl.BlockSpec((tm, d), lambda i: (i, 0)),
        out_shape=jax.ShapeDtypeStruct((m, d), BF16),
        compiler_params=_params("parallel"),
        name="rmsnorm",
    )(x, gain.reshape(1, d).astype(F32))


def _matmul_kernel(a_ref, w_ref, o_ref, *, relu2):
    y = _dot(a_ref[...], w_ref[...])
    if relu2:
        y = jnp.square(jnp.maximum(y, 0.0))
    o_ref[...] = y.astype(o_ref.dtype)


def matmul(a, w, *, relu2=False, out_dtype=F32):
    m, k = a.shape
    n = w.shape[1]
    tm = _pick(m, (1024, 512, 384, 256, 128))
    tn = _pick(n, (2048, 1536, 1280, 1024, 512, 256, 128))
    return pl.pallas_call(
        functools.partial(_matmul_kernel, relu2=relu2),
        grid=(m // tm, n // tn),
        in_specs=[pl.BlockSpec((tm, k), lambda i, j: (i, 0)),
                  pl.BlockSpec((k, tn), lambda i, j: (0, j))],
        out_specs=pl.BlockSpec((tm, tn), lambda i, j: (i, j)),
        out_shape=jax.ShapeDtypeStruct((m, n), out_dtype),
        compiler_params=_params("parallel", "parallel"),
        name="matmul",
    )(a, w)


def _matmul_wcast_kernel(a_ref, w_ref, o_ref, wb_ref, *, relu2):
    @pl.when(pl.program_id(1) == 0)
    def _():
        wb_ref[...] = w_ref[...].astype(BF16)

    y = _dot(a_ref[...], wb_ref[...])
    if relu2:
        y = jnp.square(jnp.maximum(y, 0.0))
    o_ref[...] = y.astype(o_ref.dtype)


def matmul_wcast(a, w3, layer, n_cols, *, relu2=False, out_dtype=F32):
    m, k = a.shape
    tm = _pick(m, (1024, 512, 384, 256, 128))
    tn = _pick(n_cols, (1024, 512, 256, 128))
    return pl.pallas_call(
        functools.partial(_matmul_wcast_kernel, relu2=relu2),
        grid=(n_cols // tn, m // tm),
        in_specs=[pl.BlockSpec((tm, k), lambda j, i: (i, 0)),
                  pl.BlockSpec((None, k, tn), lambda j, i: (layer, 0, j))],
        out_specs=pl.BlockSpec((tm, tn), lambda j, i: (i, j)),
        out_shape=jax.ShapeDtypeStruct((m, n_cols), out_dtype),
        scratch_shapes=[pltpu.VMEM((k, tn), BF16)],
        compiler_params=_params("parallel", "arbitrary"),
        name="matmul_wcast",
    )(a, w3)


def _matmul_norm_res_kernel(*refs, widths, nk, tm, lp, pad, emit_next, mask_pad):
    na = len(widths)
    a_refs = refs[:na]
    w_ref, g_ref, h_ref = refs[na:na + 3]
    rest = refs[na + 3:]
    if emit_next:
        g2_ref, o_ref, n_ref = rest
    else:
        o_ref, = rest
    halves = [slice(0, tm // 2), slice(tm // 2, tm)]

    def finish(y, rows):
        ms = jnp.mean(y * y, axis=-1, keepdims=True)
        out = h_ref[rows, :] + y * lax.rsqrt(ms + NORM_EPS) * g_ref[...]
        if mask_pad:
            start = pl.program_id(0) * tm
            row = start + rows.start + lax.broadcasted_iota(jnp.int32, (rows.stop - rows.start, 1), 0)
            rel0 = row - (start // lp) * lp
            rel1 = row - ((start + tm - 1) // lp) * lp
            out = jnp.where(((rel0 >= 0) & (rel0 < pad)) | ((rel1 >= 0) & (rel1 < pad)), 0.0, out)
        o_ref[rows, :] = out
        if emit_next:
            ms2 = jnp.mean(out * out, axis=-1, keepdims=True)
            n_ref[rows, :] = (out * lax.rsqrt(ms2 + NORM_EPS) * g2_ref[...]).astype(n_ref.dtype)

    if nk == 1:
        for rows in halves:
            part = None
            off = 0
            for a_ref, wd in zip(a_refs, widths):
                d = _dot(a_ref[rows, :], w_ref[off:off + wd, :])
                part = d if part is None else part + d
                off += wd
            finish(part, rows)
    else:
        kk = pl.program_id(1)

        @pl.when(kk == 0)
        def _():
            o_ref[...] = _dot(a_refs[0][...], w_ref[...])

        @pl.when(kk > 0)
        def _():
            o_ref[...] += _dot(a_refs[0][...], w_ref[...])

        @pl.when(kk == nk - 1)
        def _():
            for rows in halves:
                finish(o_ref[rows, :], rows)


def matmul_norm_res(parts, w3, layer, gain, h, next_gain, *, lp, pad, drop_head=0):
    m, n = h.shape
    widths = tuple(p.shape[1] for p in parts)
    k = sum(widths)
    if len(parts) > 1 or k <= 2048:
        tk, nk = k, 1
    else:
        tk = 2048
        nk = k // tk
        widths = (tk,)
    emit_next = next_gain is not None
    vec = pl.BlockSpec((1, n), lambda i, j: (0, 0))
    if drop_head:
        assert not emit_next and pad <= drop_head
        keep = lp - drop_head
        tm = _pick(keep, (512, 384, 256, 128))
        per_b = keep // tm
        n_row_blocks = (m // lp) * per_b
        row0 = lambda i: pl.multiple_of((i // per_b) * lp + drop_head + (i % per_b) * tm, CHUNK)
        in_specs = [pl.BlockSpec((pl.Element(tm), pl.Element(wd)), functools.partial(
            lambda i, j, wd: (row0(i), j * wd), wd=wd)) for wd in widths]
        h_spec = pl.BlockSpec((pl.Element(tm), pl.Element(n)), lambda i, j: (row0(i), 0))
    else:
        tm = _pick(m, (512, 384, 256, 128))
        n_row_blocks = m // tm
        in_specs = [pl.BlockSpec((tm, wd), lambda i, j: (i, j)) for wd in widths]
        h_spec = pl.BlockSpec((tm, n), lambda i, j: (i, 0))
    assert tm <= lp
    row_blk = pl.BlockSpec((tm, n), lambda i, j: (i, 0))
    in_specs += [pl.BlockSpec((None, tk, n), lambda i, j: (layer, j, 0)), vec, h_spec]
    args = [*parts, w3, gain.reshape(1, n).astype(F32), h]
    out_specs = [row_blk]
    out_shape = [jax.ShapeDtypeStruct((n_row_blocks * tm, n), F32)]
    if emit_next:
        in_specs.append(vec)
        args.append(next_gain.reshape(1, n).astype(F32))
        out_specs.append(row_blk)
        out_shape.append(jax.ShapeDtypeStruct((n_row_blocks * tm, n), BF16))
    res = pl.pallas_call(
        functools.partial(_matmul_norm_res_kernel, widths=widths, nk=nk, tm=tm, lp=lp, pad=pad,
                          emit_next=emit_next, mask_pad=not drop_head),
        grid=(n_row_blocks, nk),
        in_specs=in_specs,
        out_specs=out_specs,
        out_shape=out_shape,
        compiler_params=_params("parallel", "arbitrary"),
        name="matmul_norm_res",
    )(*args)
    return (res[0], res[1]) if emit_next else (res[0], None)


def _rope_axial(x, c, sa, sb):
    return x * c + pltpu.roll(x, HEAD_DIM - 32, 1) * sa + pltpu.roll(x, 32, 1) * sb


def _attn_kernel(q_ref, k_ref, v_ref, qg_ref, kg_ref, c_ref, sa_ref, sb_ref, o_ref,
                 ks_ref, vt_ref, sta_ref, stb_ref, pa_ref, pb_ref, *, lp, pad, tq):
    k = k_ref[0]
    k = k * lax.rsqrt(jnp.mean(k * k, axis=-1, keepdims=True) + NORM_EPS) * kg_ref[...]
    ks_ref[...] = _rope_axial(k, c_ref[...], sa_ref[...], sb_ref[...]).astype(BF16)
    vt_ref[...] = v_ref[0].T.astype(BF16)
    for p_ref in (pa_ref, pb_ref):
        p_ref[0:pad, :] = jnp.zeros((pad, p_ref.shape[1]), BF16)
    scale = HEAD_DIM ** -0.5 * math.log2(math.e)

    def scores(r0, rows, st_ref):
        c = c_ref[pl.ds(r0, rows), :]
        sa = sa_ref[pl.ds(r0, rows), :]
        sb = sb_ref[pl.ds(r0, rows), :]
        qs = []
        for g in range(ATT_GROUP):
            q = q_ref[0, pl.ds(r0, rows), g * HEAD_DIM:(g + 1) * HEAD_DIM]
            q = q * lax.rsqrt(jnp.mean(q * q, axis=-1, keepdims=True) + NORM_EPS) * qg_ref[...]
            qs.append((_rope_axial(q, c, sa, sb) * scale).astype(BF16))
        qall = jnp.concatenate(qs, axis=0)
        st_ref[:, 0:ATT_GROUP * rows] = _dot_nt(ks_ref[pad:, :], qall)

    def attend(r0, rows, st_ref, p_ref):
        n = ATT_GROUP * rows
        st = st_ref[:, 0:n]
        p = jnp.exp2(st - jnp.max(st, axis=0, keepdims=True))
        l = jnp.sum(p, axis=0, keepdims=True)
        p_ref[pad:, 0:n] = p.astype(BF16)
        o = (_dot(vt_ref[...], p_ref[:, 0:n]) / l).T
        for g in range(ATT_GROUP):
            o_ref[0, pl.ds(r0, rows), g * HEAD_DIM:(g + 1) * HEAD_DIM] = (
                o[g * rows:(g + 1) * rows].astype(o_ref.dtype))

    bufs = ((sta_ref, pa_ref), (stb_ref, pb_ref))
    nbig = lp // tq
    blocks = [(i * tq, tq) for i in range(nbig)]
    if lp % tq:
        blocks.append((nbig * tq, lp % tq))
    npairs = max(0, (nbig - 1) // 2)
    scores(0, blocks[0][1], sta_ref)

    def body(u, carry):
        r, r1, r2 = (pl.multiple_of((2 * u + i) * tq, tq) for i in range(3))
        scores(r1, tq, stb_ref)
        attend(r, tq, sta_ref, pa_ref)
        scores(r2, tq, sta_ref)
        attend(r1, tq, stb_ref, pb_ref)
        return carry

    lax.fori_loop(0, npairs, body, 0)
    for i in range(2 * npairs, len(blocks)):
        if i + 1 < len(blocks):
            scores(*blocks[i + 1], bufs[(i + 1) % 2][0])
        attend(*blocks[i], *bufs[i % 2])


def attention(qkvu3, q_gain, k_gain, tabs, *, pad):
    b, lp, _ = qkvu3.shape
    tq = 2 * CHUNK
    gw = ATT_GROUP * HEAD_DIM
    k_col0 = ATT_HEADS
    v_col0 = ATT_HEADS + ATT_KV_HEADS
    full = lambda bi, hi: (0, 0)
    st = pltpu.VMEM((lp - pad, ATT_GROUP * tq), F32)
    pb = pltpu.VMEM((lp, ATT_GROUP * tq), BF16)
    return pl.pallas_call(
        functools.partial(_attn_kernel, lp=lp, pad=pad, tq=tq),
        grid=(b, ATT_KV_HEADS),
        in_specs=[pl.BlockSpec((1, lp, gw), lambda bi, hi: (bi, 0, hi)),
                  pl.BlockSpec((1, lp, HEAD_DIM), lambda bi, hi: (bi, 0, k_col0 + hi)),
                  pl.BlockSpec((1, lp, HEAD_DIM), lambda bi, hi: (bi, 0, v_col0 + hi)),
                  pl.BlockSpec((1, HEAD_DIM), full),
                  pl.BlockSpec((1, HEAD_DIM), full),
                  pl.BlockSpec((lp, HEAD_DIM), full),
                  pl.BlockSpec((lp, HEAD_DIM), full),
                  pl.BlockSpec((lp, HEAD_DIM), full)],
        out_specs=pl.BlockSpec((1, lp, gw), lambda bi, hi: (bi, 0, hi)),
        out_shape=jax.ShapeDtypeStruct((b, lp, ATT_HEADS * HEAD_DIM), BF16),
        scratch_shapes=[pltpu.VMEM((lp, HEAD_DIM), BF16), pltpu.VMEM((HEAD_DIM, lp), BF16), st, st, pb, pb],
        compiler_params=_params("parallel", "parallel"),
        name="attention",
    )(qkvu3, qkvu3, qkvu3, q_gain.reshape(1, HEAD_DIM).astype(F32), k_gain.reshape(1, HEAD_DIM).astype(F32),
      *tabs)


def _s5_kernel(u_ref, k_ref, w_ref, v_ref, at_ref, y_ref, s_ref, x_ref, *, nchunk, nb):
    u = u_ref[0]
    s_ref[...] = _dot(u, w_ref[0])
    at = at_ref[0]
    afr, afi, abr, abi = (at[:, i * CHUNK:(i + 1) * CHUNK] for i in range(4))

    def body(c, carry):
        xfr, xfi, xbr, xbi = carry
        rf = pl.multiple_of(c * nb, nb)
        rb = pl.multiple_of((nchunk - 1 - c) * nb, nb)
        x_ref[pl.ds(rf, nb), 0:CHUNK] = xfr
        x_ref[pl.ds(rf, nb), CHUNK:2 * CHUNK] = xfi
        x_ref[pl.ds(rb, nb), 2 * CHUNK:3 * CHUNK] = xbr
        x_ref[pl.ds(rb, nb), 3 * CHUNK:4 * CHUNK] = xbi
        sfr = s_ref[pl.ds(rf, nb), 0:CHUNK]
        sfi = s_ref[pl.ds(rf, nb), CHUNK:2 * CHUNK]
        sbr = s_ref[pl.ds(rb, nb), 2 * CHUNK:3 * CHUNK]
        sbi = s_ref[pl.ds(rb, nb), 3 * CHUNK:4 * CHUNK]
        return (afr * xfr - afi * xfi + sfr, afr * xfi + afi * xfr + sfi,
                abr * xbr - abi * xbi + sbr, abr * xbi + abi * xbr + sbi)

    z = jnp.zeros((nb, CHUNK), F32)
    lax.fori_loop(0, nchunk, body, (z, z, z, z))
    x = _dot(u, k_ref[0]) + _dot(x_ref[...].astype(BF16), v_ref[0])
    y = x * (0.5 * (1.0 + jnp.tanh(math.sqrt(2.0 / math.pi) * (x + 0.044715 * (x * x * x)))))
    y_ref[0] = y.astype(y_ref.dtype)


def _s5_matrices(lam_re, lam_im, log_dt, b_re, b_im, c_re, c_im, d_skip):
    hi = lax.Precision.HIGHEST
    t = S5_T
    lr = jnp.minimum(lam_re, -1e-4)
    li = lam_im
    dt = jnp.exp(log_dt)[..., None]
    er = jnp.exp(lr * dt)
    abar_re = er * jnp.cos(li * dt)
    abar_im = er * jnp.sin(li * dt)
    nr = abar_re - 1.0
    den = lr * lr + li * li
    coef_re = (nr * lr + abar_im * li) / den
    coef_im = (abar_im * lr - nr * li) / den
    bb_re = coef_re[..., None] * b_re - coef_im[..., None] * b_im
    bb_im = coef_re[..., None] * b_im + coef_im[..., None] * b_re
    kk = jnp.arange(t + 1, dtype=F32)[:, None, None, None]
    mag = jnp.exp(kk * (lr * dt)[None])
    pw_re = mag * jnp.cos(kk * (li * dt)[None])
    pw_im = mag * jnp.sin(kk * (li * dt)[None])
    g, hh = d_skip.shape
    ct_re = c_re.transpose(0, 1, 3, 2)
    ct_im = c_im.transpose(0, 1, 3, 2)
    flat = lambda a: a.reshape(2, g, S5_STATE, hh * hh)
    bc_re = flat(bb_re[..., :, None] * ct_re[..., None, :] - bb_im[..., :, None] * ct_im[..., None, :])
    bc_im = flat(bb_re[..., :, None] * ct_im[..., None, :] + bb_im[..., :, None] * ct_re[..., None, :])
    lagk = (jnp.einsum('kdgp,dgpn->dgkn', pw_re, bc_re, precision=hi)
            - jnp.einsum('kdgp,dgpn->dgkn', pw_im, bc_im, precision=hi))
    ti = jnp.arange(t)
    lag = ti[None, :] - ti[:, None]
    sel = lambda m: m[None, :, :, None]
    skip = (jnp.eye(hh, dtype=F32)[None] * d_skip[:, None, :]).reshape(g, 1, 1, hh * hh)
    ktot = (jnp.where(sel(lag >= 0), lagk[0][:, jnp.clip(lag, 0, t)], 0.0)
            + jnp.where(sel(lag <= 0), lagk[1][:, jnp.clip(-lag, 0, t)], 0.0)
            + jnp.where(sel(lag == 0), skip, 0.0))
    ktot = ktot.reshape(g, t, t, hh, hh).transpose(0, 1, 3, 2, 4).reshape(g, t * hh, t * hh)

    def bsum(pw_r, pw_i, d):
        wr = pw_r[:, :, :, None] * bb_re[d][None] - pw_i[:, :, :, None] * bb_im[d][None]
        wi = pw_r[:, :, :, None] * bb_im[d][None] + pw_i[:, :, :, None] * bb_re[d][None]
        tr = lambda a: a.transpose(1, 0, 3, 2).reshape(g, t * hh, S5_STATE)
        return tr(wr), tr(wi)

    wf_re, wf_im = bsum(pw_re[:t, 0][::-1], pw_im[:t, 0][::-1], 0)
    wb_re, wb_im = bsum(pw_re[:t, 1], pw_im[:t, 1], 1)
    padl = lambda a: jnp.pad(a, ((0, 0), (0, 0), (0, CHUNK - S5_STATE)))
    wtot = jnp.concatenate([padl(wf_re), padl(wf_im), padl(wb_re), padl(wb_im)], axis=-1)

    def vmat(pw_r, pw_i, d):
        vr = pw_r[:, :, None, :] * c_re[d][None] - pw_i[:, :, None, :] * c_im[d][None]
        vi = pw_r[:, :, None, :] * c_im[d][None] + pw_i[:, :, None, :] * c_re[d][None]
        tr = lambda a: a.transpose(1, 3, 0, 2).reshape(g, S5_STATE, t * hh)
        return tr(vr), tr(-vi)

    vf_re, vf_im = vmat(pw_re[1:, 0], pw_im[1:, 0], 0)
    vb_re, vb_im = vmat(pw_re[1:, 1][::-1], pw_im[1:, 1][::-1], 1)
    padr = lambda a: jnp.pad(a, ((0, 0), (0, CHUNK - S5_STATE), (0, 0)))
    vtot = jnp.concatenate([padr(vf_re), padr(vf_im), padr(vb_re), padr(vb_im)], axis=1)
    padv = lambda a: jnp.pad(a, ((0, 0), (0, CHUNK - S5_STATE)))
    at = jnp.concatenate([padv(pw_re[t, 0]), padv(pw_im[t, 0]), padv(pw_re[t, 1]), padv(pw_im[t, 1])], axis=-1)
    return ktot.astype(BF16), wtot.astype(BF16), vtot.astype(BF16), at[:, None, :]


def s5_scan(u, mats):
    b, lp, _ = u.shape
    nchunk = lp // S5_T
    ktot, wtot, vtot, at = mats
    ug = u.astype(BF16).reshape(b, nchunk, S5_T, S5_GROUPS, S5_GROUP).transpose(3, 1, 0, 2, 4)
    ug = ug.reshape(S5_GROUPS, nchunk * b, S5_COLS)
    rows = nchunk * b
    per_g = lambda g: (g, 0, 0)
    y = pl.pallas_call(
        functools.partial(_s5_kernel, nchunk=nchunk, nb=b),
        grid=(S5_GROUPS,),
        in_specs=[pl.BlockSpec((1, rows, S5_COLS), per_g),
                  pl.BlockSpec((1, S5_COLS, S5_COLS), per_g),
                  pl.BlockSpec((1, S5_COLS, 4 * CHUNK), per_g),
                  pl.BlockSpec((1, 4 * CHUNK, S5_COLS), per_g),
                  pl.BlockSpec((1, 1, 4 * CHUNK), per_g)],
        out_specs=pl.BlockSpec((1, rows, S5_COLS), per_g),
        out_shape=jax.ShapeDtypeStruct((S5_GROUPS, rows, S5_COLS), BF16),
        scratch_shapes=[pltpu.VMEM((rows, 4 * CHUNK), F32), pltpu.VMEM((rows, 4 * CHUNK), F32)],
        compiler_params=_params("parallel"),
        name="s5_scan",
    )(ug, ktot, wtot, vtot, at)
    y = y.reshape(S5_GROUPS, nchunk, b, S5_T, S5_GROUP).transpose(2, 1, 3, 0, 4)
    return y.reshape(b * lp, S5_GROUPS * S5_GROUP)


def _s5_glu_kernel(y_ref, w_ref, b_ref, o_ref):
    y = y_ref[...]
    z = _dot(y, w_ref[...]) + b_ref[...]
    o_ref[...] = (y.astype(F32) * _sigmoid(z)).astype(o_ref.dtype)


def s5_glu(y, w, bias):
    m, n = y.shape
    tm = _pick(m, (1024, 512, 384, 256, 128))
    return pl.pallas_call(
        _s5_glu_kernel,
        grid=(m // tm,),
        in_specs=[pl.BlockSpec((tm, n), lambda i: (i, 0)),
                  pl.BlockSpec((n, n), lambda i: (0, 0)),
                  pl.BlockSpec((1, n), lambda i: (0, 0))],
        out_specs=pl.BlockSpec((tm, n), lambda i: (i, 0)),
        out_shape=jax.ShapeDtypeStruct((m, n), BF16),
        compiler_params=_params("parallel"),
        name="s5_glu",
    )(y, w, bias.reshape(1, n).astype(F32))


def _head_norm(x, gain):
    xc = x - jnp.mean(x, axis=-1, keepdims=True)
    return xc * lax.rsqrt(jnp.mean(xc * xc, axis=-1, keepdims=True) + NORM_EPS) * gain


def _ret_kernel(q_ref, k_ref, v_ref, g_ref, c_ref, s_ref, lg_ref, gn_ref, o_ref,
                qs_ref, ks_ref, vs_ref, vt_ref, af_ref, ab_ref, st_ref, *, nchunk):
    c = c_ref[...]
    s = s_ref[...]
    ii = lax.broadcasted_iota(jnp.int32, (CHUNK, CHUNK), 0).astype(F32)
    jj = lax.broadcasted_iota(jnp.int32, (CHUNK, CHUNK), 1).astype(F32)
    lane = (lax.broadcasted_iota(jnp.int32, (1, nchunk * CHUNK), 1) & (CHUNK - 1)).astype(F32)
    diff = ii - jj
    lanes = [slice(hh * HEAD_DIM, (hh + 1) * HEAD_DIM) for hh in range(HEADS_PER_STEP)]
    consts = []
    for hh, ln in enumerate(lanes):
        q = q_ref[0, :, ln]
        qs_ref[:, ln] = ((q * c + pltpu.roll(q, HEAD_DIM // 2, 1) * s) * HEAD_DIM ** -0.5).astype(BF16)
        k = k_ref[0, :, ln]
        ks_ref[:, ln] = (k * c + pltpu.roll(k, HEAD_DIM // 2, 1) * s).astype(BF16)
        v = v_ref[0, :, ln]
        vs_ref[:, ln] = v.astype(BF16)
        v_t = v.T
        lgf = lg_ref[hh, 0:1, :]
        lgb = lg_ref[hh, 1:2, :]
        vt_ref[2 * hh] = (v_t * jnp.exp((CHUNK - 1 - lane) * lgf[:, 0:1])).astype(BF16)
        vt_ref[2 * hh + 1] = (v_t * jnp.exp(lane * lgb[:, 0:1])).astype(BF16)
        fwd = (jnp.where(diff >= 0, jnp.exp(jnp.where(diff >= 0, diff, 0.0) * lgf), 0.0),
               jnp.exp((ii + 1.0) * lgf), jnp.exp(CHUNK * lgf))
        bwd = (jnp.where(diff < 0, jnp.exp(jnp.where(diff < 0, -diff, 0.0) * lgb), 0.0),
               jnp.exp((CHUNK - ii) * lgb), jnp.exp(CHUNK * lgb))
        consts.append((fwd, bwd))
    st_ref[...] = jnp.zeros_like(st_ref)

    def chunk(r, ln, slot, cst, out_ref):
        dec, xi, gc = cst
        qc = qs_ref[pl.ds(r, CHUNK), ln]
        kc = ks_ref[pl.ds(r, CHUNK), ln]
        vc = vs_ref[pl.ds(r, CHUNK), ln]
        state_t = st_ref[slot]
        both = _dot_nt(qc, jnp.concatenate([kc, state_t.astype(BF16)], axis=0))
        sc = both[:, :CHUNK] * dec
        res = _dot(jnp.concatenate([sc.astype(BF16), vt_ref[slot, :, pl.ds(r, CHUNK)]], axis=0),
                   jnp.concatenate([vc, kc], axis=1))
        out_ref[pl.ds(r, CHUNK), ln] = res[:CHUNK, :HEAD_DIM] + both[:, CHUNK:] * xi
        st_ref[slot] = gc * state_t + res[CHUNK:, HEAD_DIM:]

    def body(t, carry):
        rf = pl.multiple_of(t * CHUNK, CHUNK)
        rb = pl.multiple_of((nchunk - 1 - t) * CHUNK, CHUNK)
        for hh, ln in enumerate(lanes):
            chunk(rf, ln, 2 * hh, consts[hh][0], af_ref)
            chunk(rb, ln, 2 * hh + 1, consts[hh][1], ab_ref)
        return carry

    lax.fori_loop(0, nchunk, body, 0)
    for ln in lanes:
        gate = g_ref[0, :, ln]
        y = _head_norm(af_ref[:, ln] + ab_ref[:, ln], gn_ref[:, ln])
        o_ref[0, :, ln] = (y * (gate * _sigmoid(gate))).astype(o_ref.dtype)


def retention(proj3, log_gamma, gain, cos_t, sin_t):
    b, lp, _ = proj3.shape
    hd = HEAD_DIM
    hps = HEADS_PER_STEP
    wd = hps * hd
    nblk = RET_HEADS // hps
    lg = jnp.broadcast_to(log_gamma.T[:, :, None], (RET_HEADS, 2, hd)).astype(F32)
    blk = lambda off: pl.BlockSpec((1, lp, wd), lambda bi, hi: (bi, 0, off + hi))
    full = lambda bi, hi: (0, 0)
    return pl.pallas_call(
        functools.partial(_ret_kernel, nchunk=lp // CHUNK),
        grid=(b, nblk),
        in_specs=[blk(0), blk(nblk), blk(2 * nblk), blk(3 * nblk),
                  pl.BlockSpec((lp, hd), full), pl.BlockSpec((lp, hd), full),
                  pl.BlockSpec((hps, 2, hd), lambda bi, hi: (hi, 0, 0)),
                  pl.BlockSpec((1, wd), lambda bi, hi: (0, hi))],
        out_specs=pl.BlockSpec((1, lp, wd), lambda bi, hi: (bi, 0, hi)),
        out_shape=jax.ShapeDtypeStruct((b, lp, RET_HEADS * hd), BF16),
        scratch_shapes=[pltpu.VMEM((lp, wd), BF16), pltpu.VMEM((lp, wd), BF16), pltpu.VMEM((lp, wd), BF16),
                        pltpu.VMEM((2 * hps, hd, lp), BF16), pltpu.VMEM((lp, wd), F32), pltpu.VMEM((lp, wd), F32),
                        pltpu.VMEM((2 * hps, hd, hd), F32)],
        compiler_params=_params("parallel", "parallel"),
        name="retention",
    )(proj3, proj3, proj3, proj3, cos_t, sin_t, lg, gain.reshape(1, RET_HEADS * hd).astype(F32))


def _mlstm_gate_tables(li, lf, tri_sum):
    bt = jnp.dot(lf, tri_sum, preferred_element_type=F32, precision=lax.Precision.HIGHEST)
    bt_last = jnp.sum(lf, axis=1, keepdims=True)
    a = bt_last - bt + li
    m_loc = jnp.max(a, axis=1, keepdims=True)
    return bt, jnp.exp(a - m_loc), m_loc, bt_last, li - bt


def _mlstm_chunk_t(kc, qtc, vtc, vwc, bt_row, w_row, m_loc, bt_last, colb, mask_t, ct_prev, n_prev, m_prev):
    d, c, pk = HEAD_DIM, CHUNK, BF16_ROWS
    dlog_t = jnp.where(mask_t, bt_row + colb, -jnp.inf)
    g_row = bt_row + m_prev
    m_t = jnp.maximum(g_row, jnp.max(dlog_t, axis=0, keepdims=True))
    r1 = _dot(jnp.concatenate([kc, ct_prev.astype(BF16), jnp.broadcast_to(n_prev, (pk, d)).astype(BF16)], axis=0),
              qtc)
    s_t = r1[:c] * jnp.exp(dlog_t - m_t)
    w_int = jnp.exp(g_row - m_t)
    r2 = _dot(jnp.concatenate([vtc, vwc, jnp.broadcast_to(w_row, (pk, c)).astype(BF16)], axis=0),
              jnp.concatenate([s_t.astype(BF16), kc], axis=1))
    num_t = r2[:d, :c] + w_int * r1[c:c + d]
    den_t = jnp.sum(s_t, axis=0, keepdims=True) + w_int * r1[c + d:c + d + 1]
    out_t = num_t / jnp.maximum(jnp.abs(den_t), jnp.exp(-m_t))
    m_new = jnp.maximum(bt_last + m_prev, m_loc)
    f_prev = jnp.exp(bt_last + m_prev - m_new)
    f_loc = jnp.exp(m_loc - m_new)
    c_new = f_prev * ct_prev + f_loc * r2[d:2 * d, c:]
    n_new = f_prev * n_prev + f_loc * r2[2 * d:2 * d + 1, c:]
    return out_t, c_new, n_new, m_new


def _mlstm_kernel(mu_ref, mo_ref, gt_ref, gb_ref, cw_ref, cb_ref, wk_ref, wqt_ref, wvt_ref, gn_ref, o_ref,
                  ks_ref, qt_ref, vt_ref, vw_ref, row_ref, d_ref, colb_ref, aft_ref, abt_ref,
                  cs_ref, ns_ref, ms_ref, *, lp, pad, nchunk):
    hd = HEAD_DIM
    lanes = [slice(hh * hd, (hh + 1) * hd) for hh in range(HEADS_PER_STEP)]
    valid_row = lax.broadcasted_iota(jnp.int32, (lp, 1), 0) >= pad
    valid_col = lax.broadcasted_iota(jnp.int32, (1, lp), 1) >= pad
    pos = (lax.broadcasted_iota(jnp.int32, (nchunk, CHUNK), 0) * CHUNK
           + lax.broadcasted_iota(jnp.int32, (nchunk, CHUNK), 1))
    valid_pos = pos >= pad
    ii = lax.broadcasted_iota(jnp.int32, (CHUNK, CHUNK), 0)
    jj = lax.broadcasted_iota(jnp.int32, (CHUNK, CHUNK), 1)
    eye = ii == jj
    upper = ii <= jj
    lower = ii >= jj
    ones = jnp.ones((CHUNK, CHUNK), BF16)
    for hh, ln in enumerate(lanes):
        mu = mu_ref[0, :, ln]
        conv = cb_ref[:, ln]
        for j in range(CONV_W):
            conv = conv + cw_ref[j:j + 1, ln] * pltpu.roll(mu, (CONV_W // 2 - j) % lp, 0)
        uc = (conv * _sigmoid(conv)).astype(BF16)
        ks_ref[:, ln] = jnp.where(valid_row, _dot(uc, wk_ref[hh]) * hd ** -0.5, 0.0).astype(BF16)
        qt_ref[ln, :] = jnp.where(valid_col, _dot_nt(wqt_ref[hh], uc), 0.0).astype(BF16)
        v_t = jnp.where(valid_col, _dot_nt(wvt_ref[hh], mu.astype(BF16)), 0.0)
        vt_ref[ln, :] = v_t.astype(BF16)

        for d, tri_sum in enumerate((upper, lower)):
            g_i = gt_ref[0, hh, 2 * d] + gb_ref[hh, 2 * d:2 * d + 1, 0:1]
            g_f = gt_ref[0, hh, 2 * d + 1] + gb_ref[hh, 2 * d + 1:2 * d + 2, 0:1]
            li = jnp.where(valid_pos, g_i, NEG_GATE)
            lf = jnp.where(valid_pos, jnp.minimum(g_f, 0.0) - jnp.log(1.0 + jnp.exp(-jnp.abs(g_f))), 0.0)
            bt, w, m_loc, bt_last, colv = _mlstm_gate_tables(li, lf, jnp.where(tri_sum, 1.0, 0.0))
            slot = 2 * hh + d
            row_ref[4 * slot + 0] = bt
            row_ref[4 * slot + 1] = w
            row_ref[4 * slot + 2] = jnp.broadcast_to(m_loc, (nchunk, CHUNK))
            row_ref[4 * slot + 3] = jnp.broadcast_to(bt_last, (nchunk, CHUNK))
            for n in range(nchunk):
                cols = slice(n * CHUNK, (n + 1) * CHUNK)
                vw_ref[slot, :, cols] = (v_t[:, cols] * w[n:n + 1, :]).astype(BF16)
                d_ref[cols, :] = jnp.where(eye, colv[n:n + 1, :], 0.0)
            diag = d_ref[...]
            d_hi = diag.astype(BF16)
            d_lo = (diag - d_hi.astype(F32)).astype(BF16)
            colb_ref[slot] = _dot(d_hi, ones) + _dot(d_lo, ones)
    cs_ref[...] = jnp.zeros_like(cs_ref)
    ns_ref[...] = jnp.zeros_like(ns_ref)
    ms_ref[...] = jnp.zeros_like(ms_ref)

    def run(n, r, ln, slot, mask_t, out_ref):
        row = lambda kind: row_ref[4 * slot + kind, pl.ds(n, 1), :]
        out_t, c_s, n_s, m_s = _mlstm_chunk_t(
            ks_ref[pl.ds(r, CHUNK), ln], qt_ref[ln, pl.ds(r, CHUNK)], vt_ref[ln, pl.ds(r, CHUNK)],
            vw_ref[slot, :, pl.ds(r, CHUNK)],
            row(0), row(1), row(2)[:, 0:1], row(3)[:, 0:1], colb_ref[slot, pl.ds(r, CHUNK), :], mask_t,
            cs_ref[slot], ns_ref[slot], ms_ref[slot][:, 0:1])
        out_ref[ln, pl.ds(r, CHUNK)] = out_t
        cs_ref[slot] = c_s
        ns_ref[slot] = n_s
        ms_ref[slot] = jnp.broadcast_to(m_s, (1, hd))

    def body(t, carry):
        tb = nchunk - 1 - t
        rf = pl.multiple_of(t * CHUNK, CHUNK)
        rb = pl.multiple_of(tb * CHUNK, CHUNK)
        for hh, ln in enumerate(lanes):
            run(t, rf, ln, 2 * hh, upper, aft_ref)
            run(tb, rb, ln, 2 * hh + 1, lower, abt_ref)
        return carry

    lax.fori_loop(0, nchunk, body, 0)
    for ln in lanes:
        x_t = aft_ref[ln, :] + abt_ref[ln, :]
        xc = x_t - jnp.mean(x_t, axis=0, keepdims=True)
        y = (xc * lax.rsqrt(jnp.mean(xc * xc, axis=0, keepdims=True) + NORM_EPS)).T
        o_ref[0, :, ln] = (y * gn_ref[:, ln] * _sigmoid(mo_ref[0, :, ln])).astype(o_ref.dtype)


def mlstm(proj3, gates, gate_b, conv_w, conv_b, wq, wk, wv, gain, *, pad, mu_col0, mo_col0):
    b, lp, _ = proj3.shape
    hd = HEAD_DIM
    hps = HEADS_PER_STEP
    wd = hps * hd
    nchunk = lp // CHUNK
    gt = gates.reshape(b, lp, 4, ML_HEADS).transpose(0, 3, 2, 1).reshape(b, ML_HEADS, 4, nchunk, CHUNK)
    gb = jnp.broadcast_to(gate_b.T[:, :, None], (ML_HEADS, 4, hd)).astype(F32)
    blk = lambda off: pl.BlockSpec((1, lp, wd), lambda bi, hi: (bi, 0, off // hps + hi))
    per_h = lambda bi, hi: (hi, 0, 0)
    vec = pl.BlockSpec((1, wd), lambda bi, hi: (0, hi))
    sq = pl.BlockSpec((hps, hd, hd), per_h)
    tr = lambda w: jnp.swapaxes(w, 1, 2).astype(BF16)
    return pl.pallas_call(
        functools.partial(_mlstm_kernel, lp=lp, pad=pad, nchunk=nchunk),
        grid=(b, ML_HEADS // hps),
        in_specs=[blk(mu_col0), blk(mo_col0),
                  pl.BlockSpec((1, hps, 4, nchunk, CHUNK), lambda bi, hi: (bi, hi, 0, 0, 0)),
                  pl.BlockSpec((hps, 4, hd), per_h),
                  pl.BlockSpec((CONV_W, wd), lambda bi, hi: (0, hi)),
                  vec, sq, sq, sq, vec],
        out_specs=pl.BlockSpec((1, lp, wd), lambda bi, hi: (bi, 0, hi)),
        out_shape=jax.ShapeDtypeStruct((b, lp, ML_HEADS * hd), BF16),
        scratch_shapes=[pltpu.VMEM((lp, wd), BF16), pltpu.VMEM((wd, lp), BF16), pltpu.VMEM((wd, lp), BF16),
                        pltpu.VMEM((2 * hps, hd, lp), BF16),
                        pltpu.VMEM((8 * hps, nchunk, CHUNK), F32), pltpu.VMEM((lp, CHUNK), F32),
                        pltpu.VMEM((2 * hps, lp, CHUNK), F32),
                        pltpu.VMEM((wd, lp), F32), pltpu.VMEM((wd, lp), F32),
                        pltpu.VMEM((2 * hps, hd, hd), F32), pltpu.VMEM((2 * hps, 1, hd), F32),
                        pltpu.VMEM((2 * hps, 1, hd), F32)],
        compiler_params=_params("parallel", "parallel"),
        name="mlstm",
    )(proj3, proj3, gt, gb, conv_w.astype(F32), conv_b.reshape(1, -1).astype(F32),
      wk.astype(BF16), tr(wq), tr(wv), gain.reshape(1, -1).astype(F32))


def _rope_freqs(dim):
    return ROPE_THETA ** (-jnp.arange(dim // 2, dtype=F32) / (dim // 2))


def _axial_tables(n_tok, pad):
    rows = n_tok // GRID_W
    row = jnp.concatenate([jnp.zeros((pad,), F32), -jnp.ones((N_META,), F32),
                           jnp.repeat(jnp.arange(rows, dtype=F32), GRID_W)])
    col = jnp.concatenate([jnp.zeros((pad,), F32), jnp.arange(N_META, dtype=F32),
                           jnp.tile(jnp.arange(GRID_W, dtype=F32), rows)])
    f = _rope_freqs(HEAD_DIM // 2)
    ang = jnp.concatenate([row[:, None] * f[None, :]] * 2 + [col[:, None] * f[None, :]] * 2, axis=-1)
    first = (jnp.arange(HEAD_DIM) % 64) < 32
    sin = jnp.sin(ang)
    return jnp.cos(ang), jnp.where(first, -sin, 0.0), jnp.where(first, 0.0, sin)


def _linear_tables(l, pad):
    pos = jnp.concatenate([jnp.zeros((pad,), F32), jnp.arange(l, dtype=F32)])
    ang = pos[:, None] * _rope_freqs(HEAD_DIM)[None, :]
    ang = jnp.concatenate([ang, ang], axis=-1)
    sin = jnp.sin(ang)
    return jnp.cos(ang), jnp.where(jnp.arange(HEAD_DIM) < HEAD_DIM // 2, -sin, sin)


def _even_mixer_parts(hn, w_in_all, j, q_norm, k_norm, s5_params, glu_w, glu_b, tabs, *, b, lp, pad):
    att_w = ATT_HEADS * HEAD_DIM
    u0 = att_w + 2 * ATT_KV_HEADS * HEAD_DIM
    qkvu = matmul_wcast(hn, w_in_all, j, w_in_all.shape[2])
    qkvu3 = qkvu.reshape(b, lp, -1)
    att = attention(qkvu3, q_norm, k_norm, tabs, pad=pad).reshape(b * lp, att_w)
    y = s5_scan(qkvu3[:, :, u0:], _s5_matrices(*s5_params))
    ssm = s5_glu(y, glu_w.astype(BF16), glu_b)
    return [att, ssm]


def _odd_mixer_parts(hn, w_in_all, j, ret_log_decay, ret_norm, conv_w, conv_b, wq, wk, wv, gate_b, ml_norm,
                     tabs, *, b, lp, pad):
    ret_w = RET_HEADS * HEAD_DIM
    ml_w = ML_HEADS * HEAD_DIM
    main = 4 * ret_w + 2 * ml_w
    n_gate = w_in_all.shape[2] - main
    proj = matmul_wcast(hn, w_in_all[:, :, :main], j, main)
    proj3 = proj.reshape(b, lp, main)
    w_gate = jnp.pad(w_in_all[j, :, main:].astype(BF16), ((0, 0), (0, CHUNK - n_gate)))
    gates = matmul(hn, w_gate)[:, :n_gate].reshape(b, lp, n_gate)
    log_gamma = -jnp.abs(ret_log_decay.astype(F32))
    ret = retention(proj3, log_gamma, ret_norm, *tabs)
    nblk = ret_w // HEAD_DIM
    hm = mlstm(proj3, gates, gate_b, conv_w, conv_b, wq, wk, wv, ml_norm,
               pad=pad, mu_col0=4 * nblk, mo_col0=4 * nblk + ml_w // HEAD_DIM)
    return [ret.reshape(b * lp, ret_w), hm.reshape(b * lp, ml_w)]


def kernel(x, meta_tokens, norm_gains, mlp_w1, mlp_w2, even_w_in, even_w_out, att_q_norm, att_k_norm, s5_lam_re, s5_lam_im, s5_log_dt, s5_b_re, s5_b_im, s5_c_re, s5_c_im, s5_d, s5_glu_w, s5_glu_b, odd_w_in, odd_w_out, ret_log_decay, ret_norm, ml_conv_w, ml_conv_b, ml_wq, ml_wk, ml_wv, ml_gate_b, ml_norm):
    b, n_tok, d_model = x.shape
    l = n_tok + N_META
    pad = (-l) % CHUNK
    lp = l + pad
    depth = norm_gains.shape[0]
    h = jnp.concatenate([jnp.zeros((b, pad, d_model), x.dtype),
                         jnp.broadcast_to(meta_tokens.astype(x.dtype)[None], (b, N_META, d_model)), x], axis=1)
    h = h.reshape(b * lp, d_model)
    axial = _axial_tables(n_tok, pad)
    linear = _linear_tables(l, pad)
    dims = dict(b=b, lp=lp, pad=pad)
    hn = rmsnorm(h, norm_gains[0, 0])
    w2_bf16 = mlp_w2.astype(BF16)
    even_out_bf16 = even_w_out.astype(BF16)
    odd_out_bf16 = odd_w_out.astype(BF16)
    for i in range(depth):
        j = i // 2
        if i % 2 == 0:
            s5_params = (s5_lam_re[j], s5_lam_im[j], s5_log_dt[j], s5_b_re[j], s5_b_im[j], s5_c_re[j],
                         s5_c_im[j], s5_d[j])
            parts = _even_mixer_parts(hn, even_w_in, j, att_q_norm[j], att_k_norm[j],
                                      s5_params, s5_glu_w[j], s5_glu_b[j], axial, **dims)
            w_out = even_out_bf16
        else:
            parts = _odd_mixer_parts(hn, odd_w_in, j, ret_log_decay[j], ret_norm[j],
                                     ml_conv_w[j], ml_conv_b[j], ml_wq[j], ml_wk[j], ml_wv[j], ml_gate_b[j],
                                     ml_norm[j], linear, **dims)
            w_out = odd_out_bf16
        h, hn = matmul_norm_res(parts, w_out, j, norm_gains[i, 1], h, norm_gains[i, 2], lp=lp, pad=pad)
        hid = matmul_wcast(hn, mlp_w1, i, mlp_w1.shape[2], relu2=True, out_dtype=BF16)
        last = i + 1 == depth
        next_gain = None if last else norm_gains[i + 1, 0]
        drop = pad + N_META if last and n_tok % CHUNK == 0 else 0
        h, hn = matmul_norm_res([hid], w2_bf16, i, norm_gains[i, 3], h, next_gain, lp=lp, pad=pad, drop_head=drop)
    if drop:
        return h.reshape(b, n_tok, d_model)
    return h.reshape(b, lp, d_model)[:, pad + N_META:]
```

```python
import functools
import math

import jax
import jax.numpy as jnp
from jax import lax
from jax.experimental import pallas as pl
from jax.experimental.pallas import tpu as pltpu

F32 = jnp.float32
BF16 = jnp.bfloat16

N_META = 16
GRID_W = 64
CHUNK = 128
HEAD_DIM = 128
NORM_EPS = 1e-6
ROPE_THETA = 10000.0
ATT_HEADS = 12
ATT_KV_HEADS = 4
ATT_GROUP = ATT_HEADS // ATT_KV_HEADS
S5_GROUP = 16
S5_GROUPS = 32
S5_STATE = 64
S5_T = 16
S5_COLS = S5_T * S5_GROUP
RET_HEADS = 8
ML_HEADS = 8
CONV_W = 5
NEG_GATE = -1e4
HEADS_PER_STEP = 2
BF16_ROWS = 16
VMEM_LIMIT_BYTES = 56 * 1024 * 1024


def _pick(n, cands):
    for c in cands:
        if n % c == 0:
            return c
    raise ValueError(f"no tile for {n} in {cands}")


def _params(*sem):
    return pltpu.CompilerParams(dimension_semantics=sem, vmem_limit_bytes=VMEM_LIMIT_BYTES)


def _dot(a, b):
    return jnp.dot(a, b, preferred_element_type=F32)


def _dot_nt(a, b):
    return lax.dot_general(a, b, (((1,), (1,)), ((), ())), preferred_element_type=F32)


def _dot_tn(a, b):
    return lax.dot_general(a, b, (((0,), (0,)), ((), ())), preferred_element_type=F32)


def _sigmoid(x):
    return 1.0 / (1.0 + jnp.exp(-x))


def _rmsnorm_kernel(x_ref, g_ref, o_ref):
    x = x_ref[...]
    ms = jnp.mean(x * x, axis=-1, keepdims=True)
    o_ref[...] = (x * lax.rsqrt(ms + NORM_EPS) * g_ref[...]).astype(o_ref.dtype)


def rmsnorm(x, gain):
    m, d = x.shape
    tm = _pick(m, (1024, 512, 384, 256, 128))
    return pl.pallas_call(
        _rmsnorm_kernel,
        grid=(m // tm,),
        in_specs=[pl.BlockSpec((tm, d), lambda i: (i, 0)), pl.BlockSpec((1, d), lambda i: (0, 0))],
        out_specs=pl.BlockSpec((tm, d), lambda i: (i, 0)),
        out_shape=jax.ShapeDtypeStruct((m, d), BF16),
        compiler_params=_params("parallel"),
        name="rmsnorm",
    )(x, gain.reshape(1, d).astype(F32))


def _matmul_kernel(a_ref, w_ref, o_ref, *, relu2):
    y = _dot(a_ref[...], w_ref[...])
    if relu2:
        y = jnp.square(jnp.maximum(y, 0.0))
    o_ref[...] = y.astype(o_ref.dtype)


def matmul(a, w, *, relu2=False, out_dtype=F32):
    m, k = a.shape
    n = w.shape[1]
    tm = _pick(m, (1024, 512, 384, 256, 128))
    tn = _pick(n, (2048, 1536, 1280, 1024, 512, 256, 128))
    return pl.pallas_call(
        functools.partial(_matmul_kernel, relu2=relu2),
        grid=(m // tm, n // tn),
        in_specs=[pl.BlockSpec((tm, k), lambda i, j: (i, 0)),
                  pl.BlockSpec((k, tn), lambda i, j: (0, j))],
        out_specs=pl.BlockSpec((tm, tn), lambda i, j: (i, j)),
        out_shape=jax.ShapeDtypeStruct((m, n), out_dtype),
        compiler_params=_params("parallel", "parallel"),
        name="matmul",
    )(a, w)


def _matmul_wcast_kernel(a_ref, w_ref, o_ref, wb_ref, *, relu2):
    @pl.when(pl.program_id(1) == 0)
    def _():
        wb_ref[...] = w_ref[...].astype(BF16)

    y = _dot(a_ref[...], wb_ref[...])
    if relu2:
        y = jnp.square(jnp.maximum(y, 0.0))
    o_ref[...] = y.astype(o_ref.dtype)


def matmul_wcast(a, w3, layer, n_cols, *, relu2=False, out_dtype=F32):
    m, k = a.shape
    tm = _pick(m, (1024, 512, 384, 256, 128))
    tn = _pick(n_cols, (1024, 512, 256, 128))
    return pl.pallas_call(
        functools.partial(_matmul_wcast_kernel, relu2=relu2),
        grid=(n_cols // tn, m // tm),
        in_specs=[pl.BlockSpec((tm, k), lambda j, i: (i, 0)),
                  pl.BlockSpec((None, k, tn), lambda j, i: (layer, 0, j))],
        out_specs=pl.BlockSpec((tm, tn), lambda j, i: (i, j)),
        out_shape=jax.ShapeDtypeStruct((m, n_cols), out_dtype),
        scratch_shapes=[pltpu.VMEM((k, tn), BF16)],
        compiler_params=_params("parallel", "arbitrary"),
        name="matmul_wcast",
    )(a, w3)


def _matmul_norm_res_kernel(*refs, widths, nk, tm, lp, pad, emit_next, mask_pad):
    na = len(widths)
    a_refs = refs[:na]
    w_ref, g_ref, h_ref = refs[na:na + 3]
    rest = refs[na + 3:]
    if emit_next:
        g2_ref, o_ref, n_ref = rest
    else:
        o_ref, = rest
    halves = [slice(0, tm // 2), slice(tm // 2, tm)]

    def finish(y, rows):
        ms = jnp.mean(y * y, axis=-1, keepdims=True)
        out = h_ref[rows, :] + y * lax.rsqrt(ms + NORM_EPS) * g_ref[...]
        if mask_pad:
            start = pl.program_id(0) * tm
            row = start + rows.start + lax.broadcasted_iota(jnp.int32, (rows.stop - rows.start, 1), 0)
            rel0 = row - (start // lp) * lp
            rel1 = row - ((start + tm - 1) // lp) * lp
            out = jnp.where(((rel0 >= 0) & (rel0 < pad)) | ((rel1 >= 0) & (rel1 < pad)), 0.0, out)
        o_ref[rows, :] = out
        if emit_next:
            ms2 = jnp.mean(out * out, axis=-1, keepdims=True)
            n_ref[rows, :] = (out * lax.rsqrt(ms2 + NORM_EPS) * g2_ref[...]).astype(n_ref.dtype)

    if nk == 1:
        for rows in halves:
            part = None
            off = 0
            for a_ref, wd in zip(a_refs, widths):
                d = _dot(a_ref[rows, :], w_ref[off:off + wd, :])
                part = d if part is None else part + d
                off += wd
            finish(part, rows)
    else:
        kk = pl.program_id(1)

        @pl.when(kk == 0)
        def _():
            o_ref[...] = _dot(a_refs[0][...], w_ref[...])

        @pl.when(kk > 0)
        def _():
            o_ref[...] += _dot(a_refs[0][...], w_ref[...])

        @pl.when(kk == nk - 1)
        def _():
            for rows in halves:
                finish(o_ref[rows, :], rows)


def matmul_norm_res(parts, w3, layer, gain, h, next_gain, *, lp, pad, drop_head=0):
    m, n = h.shape
    widths = tuple(p.shape[1] for p in parts)
    k = sum(widths)
    if len(parts) > 1 or k <= 2048:
        tk, nk = k, 1
    else:
        tk = 2048
        nk = k // tk
        widths = (tk,)
    emit_next = next_gain is not None
    vec = pl.BlockSpec((1, n), lambda i, j: (0, 0))
    if drop_head:
        assert not emit_next and pad <= drop_head
        keep = lp - drop_head
        tm = _pick(keep, (512, 384, 256, 128))
        per_b = keep // tm
        n_row_blocks = (m // lp) * per_b
        row0 = lambda i: pl.multiple_of((i // per_b) * lp + drop_head + (i % per_b) * tm, CHUNK)
        in_specs = [pl.BlockSpec((pl.Element(tm), pl.Element(wd)), functools.partial(
            lambda i, j, wd: (row0(i), j * wd), wd=wd)) for wd in widths]
        h_spec = pl.BlockSpec((pl.Element(tm), pl.Element(n)), lambda i, j: (row0(i), 0))
    else:
        tm = _pick(m, (512, 384, 256, 128))
        n_row_blocks = m // tm
        in_specs = [pl.BlockSpec((tm, wd), lambda i, j: (i, j)) for wd in widths]
        h_spec = pl.BlockSpec((tm, n), lambda i, j: (i, 0))
    assert tm <= lp
    row_blk = pl.BlockSpec((tm, n), lambda i, j: (i, 0))
    in_specs += [pl.BlockSpec((None, tk, n), lambda i, j: (layer, j, 0)), vec, h_spec]
    args = [*parts, w3, gain.reshape(1, n).astype(F32), h]
    out_specs = [row_blk]
    out_shape = [jax.ShapeDtypeStruct((n_row_blocks * tm, n), F32)]
    if emit_next:
        in_specs.append(vec)
        args.append(next_gain.reshape(1, n).astype(F32))
        out_specs.append(row_blk)
        out_shape.append(jax.ShapeDtypeStruct((n_row_blocks * tm, n), BF16))
    res = pl.pallas_call(
        functools.partial(_matmul_norm_res_kernel, widths=widths, nk=nk, tm=tm, lp=lp, pad=pad,
                          emit_next=emit_next, mask_pad=not drop_head),
        grid=(n_row_blocks, nk),
        in_specs=in_specs,
        out_specs=out_specs,
        out_shape=out_shape,
        compiler_params=_params("parallel", "arbitrary"),
        name="matmul_norm_res",
    )(*args)
    return (res[0], res[1]) if emit_next else (res[0], None)


def _rope_axial(x, c, sa, sb):
    return x * c + pltpu.roll(x, HEAD_DIM - 32, 1) * sa + pltpu.roll(x, 32, 1) * sb


def _attn_kernel(q_ref, k_ref, v_ref, qg_ref, kg_ref, c_ref, sa_ref, sb_ref, o_ref,
                 ks_ref, vt_ref, sta_ref, stb_ref, pa_ref, pb_ref, *, lp, pad, tq):
    k = k_ref[0]
    k = k * lax.rsqrt(jnp.mean(k * k, axis=-1, keepdims=True) + NORM_EPS) * kg_ref[...]
    ks_ref[...] = _rope_axial(k, c_ref[...], sa_ref[...], sb_ref[...]).astype(BF16)
    vt_ref[...] = v_ref[0].T.astype(BF16)
    for p_ref in (pa_ref, pb_ref):
        p_ref[0:pad, :] = jnp.zeros((pad, p_ref.shape[1]), BF16)
    scale = HEAD_DIM ** -0.5 * math.log2(math.e)

    def scores(r0, rows, st_ref):
        c = c_ref[pl.ds(r0, rows), :]
        sa = sa_ref[pl.ds(r0, rows), :]
        sb = sb_ref[pl.ds(r0, rows), :]
        qs = []
        for g in range(ATT_GROUP):
            q = q_ref[0, pl.ds(r0, rows), g * HEAD_DIM:(g + 1) * HEAD_DIM]
            q = q * lax.rsqrt(jnp.mean(q * q, axis=-1, keepdims=True) + NORM_EPS) * qg_ref[...]
            qs.append((_rope_axial(q, c, sa, sb) * scale).astype(BF16))
        qall = jnp.concatenate(qs, axis=0)
        st_ref[:, 0:ATT_GROUP * rows] = _dot_nt(ks_ref[pad:, :], qall)

    def attend(r0, rows, st_ref, p_ref):
        n = ATT_GROUP * rows
        st = st_ref[:, 0:n]
        p = jnp.exp2(st - jnp.max(st, axis=0, keepdims=True))
        l = jnp.sum(p, axis=0, keepdims=True)
        p_ref[pad:, 0:n] = p.astype(BF16)
        o = (_dot(vt_ref[...], p_ref[:, 0:n]) / l).T
        for g in range(ATT_GROUP):
            o_ref[0, pl.ds(r0, rows), g * HEAD_DIM:(g + 1) * HEAD_DIM] = (
                o[g * rows:(g + 1) * rows].astype(o_ref.dtype))

    bufs = ((sta_ref, pa_ref), (stb_ref, pb_ref))
    nbig = lp // tq
    blocks = [(i * tq, tq) for i in range(nbig)]
    if lp % tq:
        blocks.append((nbig * tq, lp % tq))
    npairs = max(0, (nbig - 1) // 2)
    scores(0, blocks[0][1], sta_ref)

    def body(u, carry):
        r, r1, r2 = (pl.multiple_of((2 * u + i) * tq, tq) for i in range(3))
        scores(r1, tq, stb_ref)
        attend(r, tq, sta_ref, pa_ref)
        scores(r2, tq, sta_ref)
        attend(r1, tq, stb_ref, pb_ref)
        return carry

    lax.fori_loop(0, npairs, body, 0)
    for i in range(2 * npairs, len(blocks)):
        if i + 1 < len(blocks):
            scores(*blocks[i + 1], bufs[(i + 1) % 2][0])
        attend(*blocks[i], *bufs[i % 2])


def attention(qkvu3, q_gain, k_gain, tabs, *, pad):
    b, lp, _ = qkvu3.shape
    tq = 2 * CHUNK
    gw = ATT_GROUP * HEAD_DIM
    k_col0 = ATT_HEADS
    v_col0 = ATT_HEADS + ATT_KV_HEADS
    full = lambda bi, hi: (0, 0)
    st = pltpu.VMEM((lp - pad, ATT_GROUP * tq), F32)
    pb = pltpu.VMEM((lp, ATT_GROUP * tq), BF16)
    return pl.pallas_call(
        functools.partial(_attn_kernel, lp=lp, pad=pad, tq=tq),
        grid=(b, ATT_KV_HEADS),
        in_specs=[pl.BlockSpec((1, lp, gw), lambda bi, hi: (bi, 0, hi)),
                  pl.BlockSpec((1, lp, HEAD_DIM), lambda bi, hi: (bi, 0, k_col0 + hi)),
                  pl.BlockSpec((1, lp, HEAD_DIM), lambda bi, hi: (bi, 0, v_col0 + hi)),
                  pl.BlockSpec((1, HEAD_DIM), full),
                  pl.BlockSpec((1, HEAD_DIM), full),
                  pl.BlockSpec((lp, HEAD_DIM), full),
                  pl.BlockSpec((lp, HEAD_DIM), full),
                  pl.BlockSpec((lp, HEAD_DIM), full)],
        out_specs=pl.BlockSpec((1, lp, gw), lambda bi, hi: (bi, 0, hi)),
        out_shape=jax.ShapeDtypeStruct((b, lp, ATT_HEADS * HEAD_DIM), BF16),
        scratch_shapes=[pltpu.VMEM((lp, HEAD_DIM), BF16), pltpu.VMEM((HEAD_DIM, lp), BF16), st, st, pb, pb],
        compiler_params=_params("parallel", "parallel"),
        name="attention",
    )(qkvu3, qkvu3, qkvu3, q_gain.reshape(1, HEAD_DIM).astype(F32), k_gain.reshape(1, HEAD_DIM).astype(F32),
      *tabs)


def _s5_kernel(u_ref, k_ref, w_ref, v_ref, at_ref, y_ref, s_ref, x_ref, *, nchunk, nb):
    u = u_ref[0]
    s_ref[...] = _dot(u, w_ref[0])
    at = at_ref[0]
    afr, afi, abr, abi = (at[:, i * CHUNK:(i + 1) * CHUNK] for i in range(4))

    def body(c, carry):
        xfr, xfi, xbr, xbi = carry
        rf = pl.multiple_of(c * nb, nb)
        rb = pl.multiple_of((nchunk - 1 - c) * nb, nb)
        x_ref[pl.ds(rf, nb), 0:CHUNK] = xfr
        x_ref[pl.ds(rf, nb), CHUNK:2 * CHUNK] = xfi
        x_ref[pl.ds(rb, nb), 2 * CHUNK:3 * CHUNK] = xbr
        x_ref[pl.ds(rb, nb), 3 * CHUNK:4 * CHUNK] = xbi
        sfr = s_ref[pl.ds(rf, nb), 0:CHUNK]
        sfi = s_ref[pl.ds(rf, nb), CHUNK:2 * CHUNK]
        sbr = s_ref[pl.ds(rb, nb), 2 * CHUNK:3 * CHUNK]
        sbi = s_ref[pl.ds(rb, nb), 3 * CHUNK:4 * CHUNK]
        return (afr * xfr - afi * xfi + sfr, afr * xfi + afi * xfr + sfi,
                abr * xbr - abi * xbi + sbr, abr * xbi + abi * xbr + sbi)

    z = jnp.zeros((nb, CHUNK), F32)
    lax.fori_loop(0, nchunk, body, (z, z, z, z))
    x = _dot(u, k_ref[0]) + _dot(x_ref[...].astype(BF16), v_ref[0])
    y = x * (0.5 * (1.0 + jnp.tanh(math.sqrt(2.0 / math.pi) * (x + 0.044715 * (x * x * x)))))
    y_ref[0] = y.astype(y_ref.dtype)


def _s5_matrices(lam_re, lam_im, log_dt, b_re, b_im, c_re, c_im, d_skip):
    hi = lax.Precision.HIGHEST
    t = S5_T
    lr = jnp.minimum(lam_re, -1e-4)
    li = lam_im
    dt = jnp.exp(log_dt)[..., None]
    er = jnp.exp(lr * dt)
    abar_re = er * jnp.cos(li * dt)
    abar_im = er * jnp.sin(li * dt)
    nr = abar_re - 1.0
    den = lr * lr + li * li
    coef_re = (nr * lr + abar_im * li) / den
    coef_im = (abar_im * lr - nr * li) / den
    bb_re = coef_re[..., None] * b_re - coef_im[..., None] * b_im
    bb_im = coef_re[..., None] * b_im + coef_im[..., None] * b_re
    kk = jnp.arange(t + 1, dtype=F32)[:, None, None, None]
    mag = jnp.exp(kk * (lr * dt)[None])
    pw_re = mag * jnp.cos(kk * (li * dt)[None])
    pw_im = mag * jnp.sin(kk * (li * dt)[None])
    g, hh = d_skip.shape
    ct_re = c_re.transpose(0, 1, 3, 2)
    ct_im = c_im.transpose(0, 1, 3, 2)
    flat = lambda a: a.reshape(2, g, S5_STATE, hh * hh)
    bc_re = flat(bb_re[..., :, None] * ct_re[..., None, :] - bb_im[..., :, None] * ct_im[..., None, :])
    bc_im = flat(bb_re[..., :, None] * ct_im[..., None, :] + bb_im[..., :, None] * ct_re[..., None, :])
    lagk = (jnp.einsum('kdgp,dgpn->dgkn', pw_re, bc_re, precision=hi)
            - jnp.einsum('kdgp,dgpn->dgkn', pw_im, bc_im, precision=hi))
    ti = jnp.arange(t)
    lag = ti[None, :] - ti[:, None]
    sel = lambda m: m[None, :, :, None]
    skip = (jnp.eye(hh, dtype=F32)[None] * d_skip[:, None, :]).reshape(g, 1, 1, hh * hh)
    ktot = (jnp.where(sel(lag >= 0), lagk[0][:, jnp.clip(lag, 0, t)], 0.0)
            + jnp.where(sel(lag <= 0), lagk[1][:, jnp.clip(-lag, 0, t)], 0.0)
            + jnp.where(sel(lag == 0), skip, 0.0))
    ktot = ktot.reshape(g, t, t, hh, hh).transpose(0, 1, 3, 2, 4).reshape(g, t * hh, t * hh)

    def bsum(pw_r, pw_i, d):
        wr = pw_r[:, :, :, None] * bb_re[d][None] - pw_i[:, :, :, None] * bb_im[d][None]
        wi = pw_r[:, :, :, None] * bb_im[d][None] + pw_i[:, :, :, None] * bb_re[d][None]
        tr = lambda a: a.transpose(1, 0, 3, 2).reshape(g, t * hh, S5_STATE)
        return tr(wr), tr(wi)

    wf_re, wf_im = bsum(pw_re[:t, 0][::-1], pw_im[:t, 0][::-1], 0)
    wb_re, wb_im = bsum(pw_re[:t, 1], pw_im[:t, 1], 1)
    padl = lambda a: jnp.pad(a, ((0, 0), (0, 0), (0, CHUNK - S5_STATE)))
    wtot = jnp.concatenate([padl(wf_re), padl(wf_im), padl(wb_re), padl(wb_im)], axis=-1)

    def vmat(pw_r, pw_i, d):
        vr = pw_r[:, :, None, :] * c_re[d][None] - pw_i[:, :, None, :] * c_im[d][None]
        vi = pw_r[:, :, None, :] * c_im[d][None] + pw_i[:, :, None, :] * c_re[d][None]
        tr = lambda a: a.transpose(1, 3, 0, 2).reshape(g, S5_STATE, t * hh)
        return tr(vr), tr(-vi)

    vf_re, vf_im = vmat(pw_re[1:, 0], pw_im[1:, 0], 0)
    vb_re, vb_im = vmat(pw_re[1:, 1][::-1], pw_im[1:, 1][::-1], 1)
    padr = lambda a: jnp.pad(a, ((0, 0), (0, CHUNK - S5_STATE), (0, 0)))
    vtot = jnp.concatenate([padr(vf_re), padr(vf_im), padr(vb_re), padr(vb_im)], axis=1)
    padv = lambda a: jnp.pad(a, ((0, 0), (0, CHUNK - S5_STATE)))
    at = jnp.concatenate([padv(pw_re[t, 0]), padv(pw_im[t, 0]), padv(pw_re[t, 1]), padv(pw_im[t, 1])], axis=-1)
    return ktot.astype(BF16), wtot.astype(BF16), vtot.astype(BF16), at[:, None, :]


def _pack_pairs(a):
    if a.dtype.itemsize != 2:
        return a
    return lax.bitcast_convert_type(a.reshape(*a.shape[:-1], a.shape[-1] // 2, 2), jnp.uint32)


def _unpack_pairs(a):
    if a.dtype != jnp.uint32:
        return a
    out = lax.bitcast_convert_type(a, BF16)
    return out.reshape(*a.shape[:-1], 2 * a.shape[-1])


def s5_scan(u, mats):
    b, lp, _ = u.shape
    nchunk = lp // S5_T
    ktot, wtot, vtot, at = mats
    ug = _pack_pairs(u.astype(BF16).reshape(b, nchunk, S5_T, S5_GROUPS, S5_GROUP)).transpose(3, 1, 0, 2, 4)
    ug = _unpack_pairs(ug).reshape(S5_GROUPS, nchunk * b, S5_COLS)
    rows = nchunk * b
    per_g = lambda g: (g, 0, 0)
    y = pl.pallas_call(
        functools.partial(_s5_kernel, nchunk=nchunk, nb=b),
        grid=(S5_GROUPS,),
        in_specs=[pl.BlockSpec((1, rows, S5_COLS), per_g),
                  pl.BlockSpec((1, S5_COLS, S5_COLS), per_g),
                  pl.BlockSpec((1, S5_COLS, 4 * CHUNK), per_g),
                  pl.BlockSpec((1, 4 * CHUNK, S5_COLS), per_g),
                  pl.BlockSpec((1, 1, 4 * CHUNK), per_g)],
        out_specs=pl.BlockSpec((1, rows, S5_COLS), per_g),
        out_shape=jax.ShapeDtypeStruct((S5_GROUPS, rows, S5_COLS), BF16),
        scratch_shapes=[pltpu.VMEM((rows, 4 * CHUNK), F32), pltpu.VMEM((rows, 4 * CHUNK), F32)],
        compiler_params=_params("parallel"),
        name="s5_scan",
    )(ug, ktot, wtot, vtot, at)
    y = _pack_pairs(y.reshape(S5_GROUPS, nchunk, b, S5_T, S5_GROUP)).transpose(2, 1, 3, 0, 4)
    return _unpack_pairs(y).reshape(b * lp, S5_GROUPS * S5_GROUP)


def _s5_glu_kernel(y_ref, w_ref, b_ref, o_ref):
    y = y_ref[...]
    z = _dot(y, w_ref[...]) + b_ref[...]
    o_ref[...] = (y.astype(F32) * _sigmoid(z)).astype(o_ref.dtype)


def s5_glu(y, w, bias):
    m, n = y.shape
    tm = _pick(m, (1024, 512, 384, 256, 128))
    return pl.pallas_call(
        _s5_glu_kernel,
        grid=(m // tm,),
        in_specs=[pl.BlockSpec((tm, n), lambda i: (i, 0)),
                  pl.BlockSpec((n, n), lambda i: (0, 0)),
                  pl.BlockSpec((1, n), lambda i: (0, 0))],
        out_specs=pl.BlockSpec((tm, n), lambda i: (i, 0)),
        out_shape=jax.ShapeDtypeStruct((m, n), BF16),
        compiler_params=_params("parallel"),
        name="s5_glu",
    )(y, w, bias.reshape(1, n).astype(F32))


def _head_norm(x, gain):
    xc = x - jnp.mean(x, axis=-1, keepdims=True)
    return xc * lax.rsqrt(jnp.mean(xc * xc, axis=-1, keepdims=True) + NORM_EPS) * gain


def _ret_kernel(q_ref, k_ref, v_ref, g_ref, c_ref, s_ref, lg_ref, gn_ref, o_ref,
                qs_ref, ks_ref, vs_ref, vt_ref, af_ref, ab_ref, st_ref, *, nchunk):
    c = c_ref[...]
    s = s_ref[...]
    ii = lax.broadcasted_iota(jnp.int32, (CHUNK, CHUNK), 0).astype(F32)
    jj = lax.broadcasted_iota(jnp.int32, (CHUNK, CHUNK), 1).astype(F32)
    lane = (lax.broadcasted_iota(jnp.int32, (1, nchunk * CHUNK), 1) & (CHUNK - 1)).astype(F32)
    diff = ii - jj
    lanes = [slice(hh * HEAD_DIM, (hh + 1) * HEAD_DIM) for hh in range(HEADS_PER_STEP)]
    consts = []
    for hh, ln in enumerate(lanes):
        q = q_ref[0, :, ln]
        qs_ref[:, ln] = ((q * c + pltpu.roll(q, HEAD_DIM // 2, 1) * s) * HEAD_DIM ** -0.5).astype(BF16)
        k = k_ref[0, :, ln]
        ks_ref[:, ln] = (k * c + pltpu.roll(k, HEAD_DIM // 2, 1) * s).astype(BF16)
        v = v_ref[0, :, ln]
        vs_ref[:, ln] = v.astype(BF16)
        v_t = v.T
        lgf = lg_ref[hh, 0:1, :]
        lgb = lg_ref[hh, 1:2, :]
        vt_ref[2 * hh] = (v_t * jnp.exp((CHUNK - 1 - lane) * lgf[:, 0:1])).astype(BF16)
        vt_ref[2 * hh + 1] = (v_t * jnp.exp(lane * lgb[:, 0:1])).astype(BF16)
        fwd = (jnp.where(diff >= 0, jnp.exp(jnp.where(diff >= 0, diff, 0.0) * lgf), 0.0),
               jnp.exp((ii + 1.0) * lgf), jnp.exp(CHUNK * lgf))
        bwd = (jnp.where(diff < 0, jnp.exp(jnp.where(diff < 0, -diff, 0.0) * lgb), 0.0),
               jnp.exp((CHUNK - ii) * lgb), jnp.exp(CHUNK * lgb))
        consts.append((fwd, bwd))
    st_ref[...] = jnp.zeros_like(st_ref)

    def chunk(r, ln, slot, cst, out_ref):
        dec, xi, gc = cst
        qc = qs_ref[pl.ds(r, CHUNK), ln]
        kc = ks_ref[pl.ds(r, CHUNK), ln]
        vc = vs_ref[pl.ds(r, CHUNK), ln]
        state_t = st_ref[slot]
        both = _dot_nt(qc, jnp.concatenate([kc, state_t.astype(BF16)], axis=0))
        sc = both[:, :CHUNK] * dec
        res = _dot(jnp.concatenate([sc.astype(BF16), vt_ref[slot, :, pl.ds(r, CHUNK)]], axis=0),
                   jnp.concatenate([vc, kc], axis=1))
        out_ref[pl.ds(r, CHUNK), ln] = res[:CHUNK, :HEAD_DIM] + both[:, CHUNK:] * xi
        st_ref[slot] = gc * state_t + res[CHUNK:, HEAD_DIM:]

    def body(t, carry):
        rf = pl.multiple_of(t * CHUNK, CHUNK)
        rb = pl.multiple_of((nchunk - 1 - t) * CHUNK, CHUNK)
        for hh, ln in enumerate(lanes):
            chunk(rf, ln, 2 * hh, consts[hh][0], af_ref)
            chunk(rb, ln, 2 * hh + 1, consts[hh][1], ab_ref)
        return carry

    lax.fori_loop(0, nchunk, body, 0)
    for ln in lanes:
        gate = g_ref[0, :, ln]
        y = _head_norm(af_ref[:, ln] + ab_ref[:, ln], gn_ref[:, ln])
        o_ref[0, :, ln] = (y * (gate * _sigmoid(gate))).astype(o_ref.dtype)


def retention(proj3, log_gamma, gain, cos_t, sin_t):
    b, lp, _ = proj3.shape
    hd = HEAD_DIM
    hps = HEADS_PER_STEP
    wd = hps * hd
    nblk = RET_HEADS // hps
    lg = jnp.broadcast_to(log_gamma.T[:, :, None], (RET_HEADS, 2, hd)).astype(F32)
    blk = lambda off: pl.BlockSpec((1, lp, wd), lambda bi, hi: (bi, 0, off + hi))
    full = lambda bi, hi: (0, 0)
    return pl.pallas_call(
        functools.partial(_ret_kernel, nchunk=lp // CHUNK),
        grid=(b, nblk),
        in_specs=[blk(0), blk(nblk), blk(2 * nblk), blk(3 * nblk),
                  pl.BlockSpec((lp, hd), full), pl.BlockSpec((lp, hd), full),
                  pl.BlockSpec((hps, 2, hd), lambda bi, hi: (hi, 0, 0)),
                  pl.BlockSpec((1, wd), lambda bi, hi: (0, hi))],
        out_specs=pl.BlockSpec((1, lp, wd), lambda bi, hi: (bi, 0, hi)),
        out_shape=jax.ShapeDtypeStruct((b, lp, RET_HEADS * hd), BF16),
        scratch_shapes=[pltpu.VMEM((lp, wd), BF16), pltpu.VMEM((lp, wd), BF16), pltpu.VMEM((lp, wd), BF16),
                        pltpu.VMEM((2 * hps, hd, lp), BF16), pltpu.VMEM((lp, wd), F32), pltpu.VMEM((lp, wd), F32),
                        pltpu.VMEM((2 * hps, hd, hd), F32)],
        compiler_params=_params("parallel", "parallel"),
        name="retention",
    )(proj3, proj3, proj3, proj3, cos_t, sin_t, lg, gain.reshape(1, RET_HEADS * hd).astype(F32))


def _mlstm_gate_tables(li, lf, tri_sum):
    bt = jnp.dot(lf, tri_sum, preferred_element_type=F32, precision=lax.Precision.HIGHEST)
    bt_last = jnp.sum(lf, axis=1, keepdims=True)
    a = bt_last - bt + li
    m_loc = jnp.max(a, axis=1, keepdims=True)
    return bt, jnp.exp(a - m_loc), m_loc, bt_last, li - bt


def _mlstm_chunk_t(kc, qtc, vtc, vwc, bt_row, w_row, m_loc, bt_last, colb, mask_t, ct_prev, n_prev, m_prev):
    d, c, pk = HEAD_DIM, CHUNK, BF16_ROWS
    dlog_t = jnp.where(mask_t, bt_row + colb, -jnp.inf)
    g_row = bt_row + m_prev
    m_t = jnp.maximum(g_row, jnp.max(dlog_t, axis=0, keepdims=True))
    r1 = _dot(jnp.concatenate([kc, ct_prev.astype(BF16), jnp.broadcast_to(n_prev, (pk, d)).astype(BF16)], axis=0),
              qtc)
    s_t = r1[:c] * jnp.exp(dlog_t - m_t)
    w_int = jnp.exp(g_row - m_t)
    r2 = _dot(jnp.concatenate([vtc, vwc, jnp.broadcast_to(w_row, (pk, c)).astype(BF16)], axis=0),
              jnp.concatenate([s_t.astype(BF16), kc], axis=1))
    num_t = r2[:d, :c] + w_int * r1[c:c + d]
    den_t = jnp.sum(s_t, axis=0, keepdims=True) + w_int * r1[c + d:c + d + 1]
    out_t = num_t / jnp.maximum(jnp.abs(den_t), jnp.exp(-m_t))
    m_new = jnp.maximum(bt_last + m_prev, m_loc)
    f_prev = jnp.exp(bt_last + m_prev - m_new)
    f_loc = jnp.exp(m_loc - m_new)
    c_new = f_prev * ct_prev + f_loc * r2[d:2 * d, c:]
    n_new = f_prev * n_prev + f_loc * r2[2 * d:2 * d + 1, c:]
    return out_t, c_new, n_new, m_new


def _mlstm_kernel(mu_ref, mo_ref, gt_ref, gb_ref, cw_ref, cb_ref, wk_ref, wqt_ref, wvt_ref, gn_ref, o_ref,
                  ks_ref, qt_ref, vt_ref, vw_ref, row_ref, d_ref, colb_ref, aft_ref, abt_ref,
                  cs_ref, ns_ref, ms_ref, *, lp, pad, nchunk):
    hd = HEAD_DIM
    lanes = [slice(hh * hd, (hh + 1) * hd) for hh in range(HEADS_PER_STEP)]
    valid_row = lax.broadcasted_iota(jnp.int32, (lp, 1), 0) >= pad
    valid_col = lax.broadcasted_iota(jnp.int32, (1, lp), 1) >= pad
    pos = (lax.broadcasted_iota(jnp.int32, (nchunk, CHUNK), 0) * CHUNK
           + lax.broadcasted_iota(jnp.int32, (nchunk, CHUNK), 1))
    valid_pos = pos >= pad
    ii = lax.broadcasted_iota(jnp.int32, (CHUNK, CHUNK), 0)
    jj = lax.broadcasted_iota(jnp.int32, (CHUNK, CHUNK), 1)
    eye = ii == jj
    upper = ii <= jj
    lower = ii >= jj
    ones = jnp.ones((CHUNK, CHUNK), BF16)
    for hh, ln in enumerate(lanes):
        mu = mu_ref[0, :, ln]
        conv = cb_ref[:, ln]
        for j in range(CONV_W):
            conv = conv + cw_ref[j:j + 1, ln] * pltpu.roll(mu, (CONV_W // 2 - j) % lp, 0)
        uc = (conv * _sigmoid(conv)).astype(BF16)
        ks_ref[:, ln] = jnp.where(valid_row, _dot(uc, wk_ref[hh]) * hd ** -0.5, 0.0).astype(BF16)
        qt_ref[ln, :] = jnp.where(valid_col, _dot_nt(wqt_ref[hh], uc), 0.0).astype(BF16)
        v_t = jnp.where(valid_col, _dot_nt(wvt_ref[hh], mu.astype(BF16)), 0.0)
        vt_ref[ln, :] = v_t.astype(BF16)

        for d, tri_sum in enumerate((upper, lower)):
            g_i = gt_ref[0, hh, 2 * d] + gb_ref[hh, 2 * d:2 * d + 1, 0:1]
            g_f = gt_ref[0, hh, 2 * d + 1] + gb_ref[hh, 2 * d + 1:2 * d + 2, 0:1]
            li = jnp.where(valid_pos, g_i, NEG_GATE)
            lf = jnp.where(valid_pos, jnp.minimum(g_f, 0.0) - jnp.log(1.0 + jnp.exp(-jnp.abs(g_f))), 0.0)
            bt, w, m_loc, bt_last, colv = _mlstm_gate_tables(li, lf, jnp.where(tri_sum, 1.0, 0.0))
            slot = 2 * hh + d
            row_ref[4 * slot + 0] = bt
            row_ref[4 * slot + 1] = w
            row_ref[4 * slot + 2] = jnp.broadcast_to(m_loc, (nchunk, CHUNK))
            row_ref[4 * slot + 3] = jnp.broadcast_to(bt_last, (nchunk, CHUNK))
            for n in range(nchunk):
                cols = slice(n * CHUNK, (n + 1) * CHUNK)
                vw_ref[slot, :, cols] = (v_t[:, cols] * w[n:n + 1, :]).astype(BF16)
                d_ref[cols, :] = jnp.where(eye, colv[n:n + 1, :], 0.0)
            diag = d_ref[...]
            d_hi = diag.astype(BF16)
            d_lo = (diag - d_hi.astype(F32)).astype(BF16)
            colb_ref[slot] = _dot(d_hi, ones) + _dot(d_lo, ones)
    cs_ref[...] = jnp.zeros_like(cs_ref)
    ns_ref[...] = jnp.zeros_like(ns_ref)
    ms_ref[...] = jnp.zeros_like(ms_ref)

    def run(n, r, ln, slot, mask_t, out_ref):
        row = lambda kind: row_ref[4 * slot + kind, pl.ds(n, 1), :]
        out_t, c_s, n_s, m_s = _mlstm_chunk_t(
            ks_ref[pl.ds(r, CHUNK), ln], qt_ref[ln, pl.ds(r, CHUNK)], vt_ref[ln, pl.ds(r, CHUNK)],
            vw_ref[slot, :, pl.ds(r, CHUNK)],
            row(0), row(1), row(2)[:, 0:1], row(3)[:, 0:1], colb_ref[slot, pl.ds(r, CHUNK), :], mask_t,
            cs_ref[slot], ns_ref[slot], ms_ref[slot][:, 0:1])
        out_ref[ln, pl.ds(r, CHUNK)] = out_t
        cs_ref[slot] = c_s
        ns_ref[slot] = n_s
        ms_ref[slot] = jnp.broadcast_to(m_s, (1, hd))

    def body(t, carry):
        tb = nchunk - 1 - t
        rf = pl.multiple_of(t * CHUNK, CHUNK)
        rb = pl.multiple_of(tb * CHUNK, CHUNK)
        for hh, ln in enumerate(lanes):
            run(t, rf, ln, 2 * hh, upper, aft_ref)
            run(tb, rb, ln, 2 * hh + 1, lower, abt_ref)
        return carry

    lax.fori_loop(0, nchunk, body, 0)
    for ln in lanes:
        x_t = aft_ref[ln, :] + abt_ref[ln, :]
        xc = x_t - jnp.mean(x_t, axis=0, keepdims=True)
        y = (xc * lax.rsqrt(jnp.mean(xc * xc, axis=0, keepdims=True) + NORM_EPS)).T
        o_ref[0, :, ln] = (y * gn_ref[:, ln] * _sigmoid(mo_ref[0, :, ln])).astype(o_ref.dtype)


def mlstm(proj3, gates, gate_b, conv_w, conv_b, wq, wk, wv, gain, *, pad, mu_col0, mo_col0):
    b, lp, _ = proj3.shape
    hd = HEAD_DIM
    hps = HEADS_PER_STEP
    wd = hps * hd
    nchunk = lp // CHUNK
    gt = gates.reshape(b, lp, 4, ML_HEADS).transpose(0, 3, 2, 1).reshape(b, ML_HEADS, 4, nchunk, CHUNK)
    gb = jnp.broadcast_to(gate_b.T[:, :, None], (ML_HEADS, 4, hd)).astype(F32)
    blk = lambda off: pl.BlockSpec((1, lp, wd), lambda bi, hi: (bi, 0, off // hps + hi))
    per_h = lambda bi, hi: (hi, 0, 0)
    vec = pl.BlockSpec((1, wd), lambda bi, hi: (0, hi))
    sq = pl.BlockSpec((hps, hd, hd), per_h)
    tr = lambda w: jnp.swapaxes(w, 1, 2).astype(BF16)
    return pl.pallas_call(
        functools.partial(_mlstm_kernel, lp=lp, pad=pad, nchunk=nchunk),
        grid=(b, ML_HEADS // hps),
        in_specs=[blk(mu_col0), blk(mo_col0),
                  pl.BlockSpec((1, hps, 4, nchunk, CHUNK), lambda bi, hi: (bi, hi, 0, 0, 0)),
                  pl.BlockSpec((hps, 4, hd), per_h),
                  pl.BlockSpec((CONV_W, wd), lambda bi, hi: (0, hi)),
                  vec, sq, sq, sq, vec],
        out_specs=pl.BlockSpec((1, lp, wd), lambda bi, hi: (bi, 0, hi)),
        out_shape=jax.ShapeDtypeStruct((b, lp, ML_HEADS * hd), BF16),
        scratch_shapes=[pltpu.VMEM((lp, wd), BF16), pltpu.VMEM((wd, lp), BF16), pltpu.VMEM((wd, lp), BF16),
                        pltpu.VMEM((2 * hps, hd, lp), BF16),
                        pltpu.VMEM((8 * hps, nchunk, CHUNK), F32), pltpu.VMEM((lp, CHUNK), F32),
                        pltpu.VMEM((2 * hps, lp, CHUNK), F32),
                        pltpu.VMEM((wd, lp), F32), pltpu.VMEM((wd, lp), F32),
                        pltpu.VMEM((2 * hps, hd, hd), F32), pltpu.VMEM((2 * hps, 1, hd), F32),
                        pltpu.VMEM((2 * hps, 1, hd), F32)],
        compiler_params=_params("parallel", "parallel"),
        name="mlstm",
    )(proj3, proj3, gt, gb, conv_w.astype(F32), conv_b.reshape(1, -1).astype(F32),
      wk.astype(BF16), tr(wq), tr(wv), gain.reshape(1, -1).astype(F32))


def _rope_freqs(dim):
    return ROPE_THETA ** (-jnp.arange(dim // 2, dtype=F32) / (dim // 2))


def _axial_tables(n_tok, pad):
    rows = n_tok // GRID_W
    row = jnp.concatenate([jnp.zeros((pad,), F32), -jnp.ones((N_META,), F32),
                           jnp.repeat(jnp.arange(rows, dtype=F32), GRID_W)])
    col = jnp.concatenate([jnp.zeros((pad,), F32), jnp.arange(N_META, dtype=F32),
                           jnp.tile(jnp.arange(GRID_W, dtype=F32), rows)])
    f = _rope_freqs(HEAD_DIM // 2)
    ang = jnp.concatenate([row[:, None] * f[None, :]] * 2 + [col[:, None] * f[None, :]] * 2, axis=-1)
    first = (jnp.arange(HEAD_DIM) % 64) < 32
    sin = jnp.sin(ang)
    return jnp.cos(ang), jnp.where(first, -sin, 0.0), jnp.where(first, 0.0, sin)


def _linear_tables(l, pad):
    pos = jnp.concatenate([jnp.zeros((pad,), F32), jnp.arange(l, dtype=F32)])
    ang = pos[:, None] * _rope_freqs(HEAD_DIM)[None, :]
    ang = jnp.concatenate([ang, ang], axis=-1)
    sin = jnp.sin(ang)
    return jnp.cos(ang), jnp.where(jnp.arange(HEAD_DIM) < HEAD_DIM // 2, -sin, sin)


def _even_mixer_parts(hn, w_in_all, j, q_norm, k_norm, s5_params, glu_w, glu_b, tabs, *, b, lp, pad):
    att_w = ATT_HEADS * HEAD_DIM
    u0 = att_w + 2 * ATT_KV_HEADS * HEAD_DIM
    qkvu = matmul_wcast(hn, w_in_all, j, w_in_all.shape[2])
    qkvu3 = qkvu.reshape(b, lp, -1)
    att = attention(qkvu3, q_norm, k_norm, tabs, pad=pad).reshape(b * lp, att_w)
    y = s5_scan(qkvu3[:, :, u0:], _s5_matrices(*s5_params))
    ssm = s5_glu(y, glu_w.astype(BF16), glu_b)
    return [att, ssm]


def _odd_mixer_parts(hn, w_in_all, j, ret_log_decay, ret_norm, conv_w, conv_b, wq, wk, wv, gate_b, ml_norm,
                     tabs, *, b, lp, pad):
    ret_w = RET_HEADS * HEAD_DIM
    ml_w = ML_HEADS * HEAD_DIM
    main = 4 * ret_w + 2 * ml_w
    n_gate = w_in_all.shape[2] - main
    proj = matmul_wcast(hn, w_in_all, j, main)
    proj3 = proj.reshape(b, lp, main)
    w_gate = jnp.pad(w_in_all[j, :, main:].astype(BF16), ((0, 0), (0, CHUNK - n_gate)))
    gates = matmul(hn, w_gate)[:, :n_gate].reshape(b, lp, n_gate)
    log_gamma = -jnp.abs(ret_log_decay.astype(F32))
    ret = retention(proj3, log_gamma, ret_norm, *tabs)
    nblk = ret_w // HEAD_DIM
    hm = mlstm(proj3, gates, gate_b, conv_w, conv_b, wq, wk, wv, ml_norm,
               pad=pad, mu_col0=4 * nblk, mo_col0=4 * nblk + ml_w // HEAD_DIM)
    return [ret.reshape(b * lp, ret_w), hm.reshape(b * lp, ml_w)]


def kernel(x, meta_tokens, norm_gains, mlp_w1, mlp_w2, even_w_in, even_w_out, att_q_norm, att_k_norm, s5_lam_re, s5_lam_im, s5_log_dt, s5_b_re, s5_b_im, s5_c_re, s5_c_im, s5_d, s5_glu_w, s5_glu_b, odd_w_in, odd_w_out, ret_log_decay, ret_norm, ml_conv_w, ml_conv_b, ml_wq, ml_wk, ml_wv, ml_gate_b, ml_norm):
    b, n_tok, d_model = x.shape
    l = n_tok + N_META
    pad = (-l) % CHUNK
    lp = l + pad
    depth = norm_gains.shape[0]
    h = jnp.concatenate([jnp.zeros((b, pad, d_model), x.dtype),
                         jnp.broadcast_to(meta_tokens.astype(x.dtype)[None], (b, N_META, d_model)), x], axis=1)
    h = h.reshape(b * lp, d_model)
    axial = _axial_tables(n_tok, pad)
    linear = _linear_tables(l, pad)
    dims = dict(b=b, lp=lp, pad=pad)
    hn = rmsnorm(h, norm_gains[0, 0])
    w2_bf16 = mlp_w2.astype(BF16)
    even_out_bf16 = even_w_out.astype(BF16)
    odd_out_bf16 = odd_w_out.astype(BF16)
    for i in range(depth):
        j = i // 2
        if i % 2 == 0:
            s5_params = (s5_lam_re[j], s5_lam_im[j], s5_log_dt[j], s5_b_re[j], s5_b_im[j], s5_c_re[j],
                         s5_c_im[j], s5_d[j])
            parts = _even_mixer_parts(hn, even_w_in, j, att_q_norm[j], att_k_norm[j],
                                      s5_params, s5_glu_w[j], s5_glu_b[j], axial, **dims)
            w_out = even_out_bf16
        else:
            parts = _odd_mixer_parts(hn, odd_w_in, j, ret_log_decay[j], ret_norm[j],
                                     ml_conv_w[j], ml_conv_b[j], ml_wq[j], ml_wk[j], ml_wv[j], ml_gate_b[j],
                                     ml_norm[j], linear, **dims)
            w_out = odd_out_bf16
        h, hn = matmul_norm_res(parts, w_out, j, norm_gains[i, 1], h, norm_gains[i, 2], lp=lp, pad=pad)
        hid = matmul_wcast(hn, mlp_w1, i, mlp_w1.shape[2], relu2=True, out_dtype=BF16)
        last = i + 1 == depth
        next_gain = None if last else norm_gains[i + 1, 0]
        drop = pad + N_META if last and n_tok % CHUNK == 0 else 0
        h, hn = matmul_norm_res([hid], w2_bf16, i, norm_gains[i, 3], h, next_gain, lp=lp, pad=pad, drop_head=drop)
    if drop:
        return h.reshape(b, n_tok, d_model)
    return h.reshape(b, lp, d_model)[:, pad + N_META:]
```

```python
import functools
import math

import jax
import jax.numpy as jnp
from jax import lax
from jax.experimental import pallas as pl
from jax.experimental.pallas import tpu as pltpu

F32 = jnp.float32
BF16 = jnp.bfloat16

N_META = 16
GRID_W = 64
CHUNK = 128
HEAD_DIM = 128
NORM_EPS = 1e-6
ROPE_THETA = 10000.0
ATT_HEADS = 12
ATT_KV_HEADS = 4
ATT_GROUP = ATT_HEADS // ATT_KV_HEADS
S5_GROUP = 16
S5_GROUPS = 32
S5_STATE = 64
S5_T = 16
S5_COLS = S5_T * S5_GROUP
RET_HEADS = 8
ML_HEADS = 8
CONV_W = 5
NEG_GATE = -1e4
HEADS_PER_STEP = 4
BF16_ROWS = 16
VMEM_LIMIT_BYTES = 56 * 1024 * 1024


def _pick(n, cands):
    for c in cands:
        if n % c == 0:
            return c
    raise ValueError(f"no tile for {n} in {cands}")


def _params(*sem):
    return pltpu.CompilerParams(dimension_semantics=sem, vmem_limit_bytes=VMEM_LIMIT_BYTES)


def _dot(a, b):
    return jnp.dot(a, b, preferred_element_type=F32)


def _dot_nt(a, b):
    return lax.dot_general(a, b, (((1,), (1,)), ((), ())), preferred_element_type=F32)


def _dot_tn(a, b):
    return lax.dot_general(a, b, (((0,), (0,)), ((), ())), preferred_element_type=F32)


def _sigmoid(x):
    return 1.0 / (1.0 + jnp.exp(-x))


def _rmsnorm_kernel(x_ref, g_ref, o_ref):
    x = x_ref[...]
    ms = jnp.mean(x * x, axis=-1, keepdims=True)
    o_ref[...] = (x * lax.rsqrt(ms + NORM_EPS) * g_ref[...]).astype(o_ref.dtype)


def rmsnorm(x, gain):
    m, d = x.shape
    tm = _pick(m, (1024, 512, 384, 256, 128))
    return pl.pallas_call(
        _rmsnorm_kernel,
        grid=(m // tm,),
        in_specs=[pl.BlockSpec((tm, d), lambda i: (i, 0)), pl.BlockSpec((1, d), lambda i: (0, 0))],
        out_specs=pl.BlockSpec((tm, d), lambda i: (i, 0)),
        out_shape=jax.ShapeDtypeStruct((m, d), BF16),
        compiler_params=_params("parallel"),
        name="rmsnorm",
    )(x, gain.reshape(1, d).astype(F32))


def _matmul_kernel(a_ref, w_ref, o_ref, *, relu2):
    y = _dot(a_ref[...], w_ref[...])
    if relu2:
        y = jnp.square(jnp.maximum(y, 0.0))
    o_ref[...] = y.astype(o_ref.dtype)


def matmul(a, w, *, relu2=False, out_dtype=F32):
    m, k = a.shape
    n = w.shape[1]
    tm = _pick(m, (1024, 512, 384, 256, 128))
    tn = _pick(n, (2048, 1536, 1280, 1024, 512, 256, 128))
    return pl.pallas_call(
        functools.partial(_matmul_kernel, relu2=relu2),
        grid=(m // tm, n // tn),
        in_specs=[pl.BlockSpec((tm, k), lambda i, j: (i, 0)),
                  pl.BlockSpec((k, tn), lambda i, j: (0, j))],
        out_specs=pl.BlockSpec((tm, tn), lambda i, j: (i, j)),
        out_shape=jax.ShapeDtypeStruct((m, n), out_dtype),
        compiler_params=_params("parallel", "parallel"),
        name="matmul",
    )(a, w)


def _matmul_wcast_kernel(a_ref, w_ref, o_ref, wb_ref, *, relu2):
    @pl.when(pl.program_id(1) == 0)
    def _():
        wb_ref[...] = w_ref[...].astype(BF16)

    y = _dot(a_ref[...], wb_ref[...])
    if relu2:
        y = jnp.square(jnp.maximum(y, 0.0))
    o_ref[...] = y.astype(o_ref.dtype)


def matmul_wcast(a, w3, layer, n_cols, *, relu2=False, out_dtype=F32):
    m, k = a.shape
    tm = _pick(m, (1024, 512, 384, 256, 128))
    tn = _pick(n_cols, (1024, 512, 256, 128))
    return pl.pallas_call(
        functools.partial(_matmul_wcast_kernel, relu2=relu2),
        grid=(n_cols // tn, m // tm),
        in_specs=[pl.BlockSpec((tm, k), lambda j, i: (i, 0)),
                  pl.BlockSpec((None, k, tn), lambda j, i: (layer, 0, j))],
        out_specs=pl.BlockSpec((tm, tn), lambda j, i: (i, j)),
        out_shape=jax.ShapeDtypeStruct((m, n_cols), out_dtype),
        scratch_shapes=[pltpu.VMEM((k, tn), BF16)],
        compiler_params=_params("parallel", "arbitrary"),
        name="matmul_wcast",
    )(a, w3)


def _matmul_norm_res_kernel(*refs, widths, nk, tm, lp, pad, emit_next, mask_pad):
    na = len(widths)
    a_refs = refs[:na]
    w_ref, g_ref, h_ref = refs[na:na + 3]
    rest = refs[na + 3:]
    if emit_next:
        g2_ref, o_ref, n_ref = rest
    else:
        o_ref, = rest
    halves = [slice(0, tm // 2), slice(tm // 2, tm)]

    def finish(y, rows):
        ms = jnp.mean(y * y, axis=-1, keepdims=True)
        out = h_ref[rows, :] + y * lax.rsqrt(ms + NORM_EPS) * g_ref[...]
        if mask_pad:
            start = pl.program_id(0) * tm
            row = start + rows.start + lax.broadcasted_iota(jnp.int32, (rows.stop - rows.start, 1), 0)
            rel0 = row - (start // lp) * lp
            rel1 = row - ((start + tm - 1) // lp) * lp
            out = jnp.where(((rel0 >= 0) & (rel0 < pad)) | ((rel1 >= 0) & (rel1 < pad)), 0.0, out)
        o_ref[rows, :] = out
        if emit_next:
            ms2 = jnp.mean(out * out, axis=-1, keepdims=True)
            n_ref[rows, :] = (out * lax.rsqrt(ms2 + NORM_EPS) * g2_ref[...]).astype(n_ref.dtype)

    if nk == 1:
        for rows in halves:
            part = None
            off = 0
            for a_ref, wd in zip(a_refs, widths):
                d = _dot(a_ref[rows, :], w_ref[off:off + wd, :])
                part = d if part is None else part + d
                off += wd
            finish(part, rows)
    else:
        kk = pl.program_id(1)

        @pl.when(kk == 0)
        def _():
            o_ref[...] = _dot(a_refs[0][...], w_ref[...])

        @pl.when(kk > 0)
        def _():
            o_ref[...] += _dot(a_refs[0][...], w_ref[...])

        @pl.when(kk == nk - 1)
        def _():
            for rows in halves:
                finish(o_ref[rows, :], rows)


def matmul_norm_res(parts, w3, layer, gain, h, next_gain, *, lp, pad, drop_head=0):
    m, n = h.shape
    widths = tuple(p.shape[1] for p in parts)
    k = sum(widths)
    if len(parts) > 1 or k <= 2048:
        tk, nk = k, 1
    else:
        tk = 2048
        nk = k // tk
        widths = (tk,)
    emit_next = next_gain is not None
    vec = pl.BlockSpec((1, n), lambda i, j: (0, 0))
    if drop_head:
        assert not emit_next and pad <= drop_head
        keep = lp - drop_head
        tm = _pick(keep, (512, 384, 256, 128))
        per_b = keep // tm
        n_row_blocks = (m // lp) * per_b
        row0 = lambda i: pl.multiple_of((i // per_b) * lp + drop_head + (i % per_b) * tm, CHUNK)
        in_specs = [pl.BlockSpec((pl.Element(tm), pl.Element(wd)), functools.partial(
            lambda i, j, wd: (row0(i), j * wd), wd=wd)) for wd in widths]
        h_spec = pl.BlockSpec((pl.Element(tm), pl.Element(n)), lambda i, j: (row0(i), 0))
    else:
        tm = _pick(m, (512, 384, 256, 128))
        n_row_blocks = m // tm
        in_specs = [pl.BlockSpec((tm, wd), lambda i, j: (i, j)) for wd in widths]
        h_spec = pl.BlockSpec((tm, n), lambda i, j: (i, 0))
    assert tm <= lp
    row_blk = pl.BlockSpec((tm, n), lambda i, j: (i, 0))
    in_specs += [pl.BlockSpec((None, tk, n), lambda i, j: (layer, j, 0)), vec, h_spec]
    args = [*parts, w3, gain.reshape(1, n).astype(F32), h]
    out_specs = [row_blk]
    out_shape = [jax.ShapeDtypeStruct((n_row_blocks * tm, n), F32)]
    if emit_next:
        in_specs.append(vec)
        args.append(next_gain.reshape(1, n).astype(F32))
        out_specs.append(row_blk)
        out_shape.append(jax.ShapeDtypeStruct((n_row_blocks * tm, n), BF16))
    res = pl.pallas_call(
        functools.partial(_matmul_norm_res_kernel, widths=widths, nk=nk, tm=tm, lp=lp, pad=pad,
                          emit_next=emit_next, mask_pad=not drop_head),
        grid=(n_row_blocks, nk),
        in_specs=in_specs,
        out_specs=out_specs,
        out_shape=out_shape,
        compiler_params=_params("parallel", "arbitrary"),
        name="matmul_norm_res",
    )(*args)
    return (res[0], res[1]) if emit_next else (res[0], None)


def _rope_axial(x, c, sa, sb):
    return x * c + pltpu.roll(x, HEAD_DIM - 32, 1) * sa + pltpu.roll(x, 32, 1) * sb


def _attn_kernel(q_ref, k_ref, v_ref, qg_ref, kg_ref, c_ref, sa_ref, sb_ref, o_ref,
                 ks_ref, vt_ref, sta_ref, stb_ref, pa_ref, pb_ref, *, lp, pad, tq):
    k = k_ref[0]
    k = k * lax.rsqrt(jnp.mean(k * k, axis=-1, keepdims=True) + NORM_EPS) * kg_ref[...]
    ks_ref[...] = _rope_axial(k, c_ref[...], sa_ref[...], sb_ref[...]).astype(BF16)
    vt_ref[...] = v_ref[0].T.astype(BF16)
    for p_ref in (pa_ref, pb_ref):
        p_ref[0:pad, :] = jnp.zeros((pad, p_ref.shape[1]), BF16)
    scale = HEAD_DIM ** -0.5 * math.log2(math.e)

    def scores(r0, rows, st_ref):
        c = c_ref[pl.ds(r0, rows), :]
        sa = sa_ref[pl.ds(r0, rows), :]
        sb = sb_ref[pl.ds(r0, rows), :]
        qs = []
        for g in range(ATT_GROUP):
            q = q_ref[0, pl.ds(r0, rows), g * HEAD_DIM:(g + 1) * HEAD_DIM]
            q = q * lax.rsqrt(jnp.mean(q * q, axis=-1, keepdims=True) + NORM_EPS) * qg_ref[...]
            qs.append((_rope_axial(q, c, sa, sb) * scale).astype(BF16))
        qall = jnp.concatenate(qs, axis=0)
        st_ref[:, 0:ATT_GROUP * rows] = _dot_nt(ks_ref[pad:, :], qall)

    def attend(r0, rows, st_ref, p_ref):
        n = ATT_GROUP * rows
        st = st_ref[:, 0:n]
        p = jnp.exp2(st - jnp.max(st, axis=0, keepdims=True))
        l = jnp.sum(p, axis=0, keepdims=True)
        p_ref[pad:, 0:n] = p.astype(BF16)
        o = (_dot(vt_ref[...], p_ref[:, 0:n]) / l).T
        for g in range(ATT_GROUP):
            o_ref[0, pl.ds(r0, rows), g * HEAD_DIM:(g + 1) * HEAD_DIM] = (
                o[g * rows:(g + 1) * rows].astype(o_ref.dtype))

    bufs = ((sta_ref, pa_ref), (stb_ref, pb_ref))
    nbig = lp // tq
    blocks = [(i * tq, tq) for i in range(nbig)]
    if lp % tq:
        blocks.append((nbig * tq, lp % tq))
    npairs = max(0, (nbig - 1) // 2)
    scores(0, blocks[0][1], sta_ref)

    def body(u, carry):
        r, r1, r2 = (pl.multiple_of((2 * u + i) * tq, tq) for i in range(3))
        scores(r1, tq, stb_ref)
        attend(r, tq, sta_ref, pa_ref)
        scores(r2, tq, sta_ref)
        attend(r1, tq, stb_ref, pb_ref)
        return carry

    lax.fori_loop(0, npairs, body, 0)
    for i in range(2 * npairs, len(blocks)):
        if i + 1 < len(blocks):
            scores(*blocks[i + 1], bufs[(i + 1) % 2][0])
        attend(*blocks[i], *bufs[i % 2])


def attention(qkvu3, q_gain, k_gain, tabs, *, pad):
    b, lp, _ = qkvu3.shape
    tq = 2 * CHUNK
    gw = ATT_GROUP * HEAD_DIM
    k_col0 = ATT_HEADS
    v_col0 = ATT_HEADS + ATT_KV_HEADS
    full = lambda bi, hi: (0, 0)
    st = pltpu.VMEM((lp - pad, ATT_GROUP * tq), F32)
    pb = pltpu.VMEM((lp, ATT_GROUP * tq), BF16)
    return pl.pallas_call(
        functools.partial(_attn_kernel, lp=lp, pad=pad, tq=tq),
        grid=(b, ATT_KV_HEADS),
        in_specs=[pl.BlockSpec((1, lp, gw), lambda bi, hi: (bi, 0, hi)),
                  pl.BlockSpec((1, lp, HEAD_DIM), lambda bi, hi: (bi, 0, k_col0 + hi)),
                  pl.BlockSpec((1, lp, HEAD_DIM), lambda bi, hi: (bi, 0, v_col0 + hi)),
                  pl.BlockSpec((1, HEAD_DIM), full),
                  pl.BlockSpec((1, HEAD_DIM), full),
                  pl.BlockSpec((lp, HEAD_DIM), full),
                  pl.BlockSpec((lp, HEAD_DIM), full),
                  pl.BlockSpec((lp, HEAD_DIM), full)],
        out_specs=pl.BlockSpec((1, lp, gw), lambda bi, hi: (bi, 0, hi)),
        out_shape=jax.ShapeDtypeStruct((b, lp, ATT_HEADS * HEAD_DIM), BF16),
        scratch_shapes=[pltpu.VMEM((lp, HEAD_DIM), BF16), pltpu.VMEM((HEAD_DIM, lp), BF16), st, st, pb, pb],
        compiler_params=_params("parallel", "parallel"),
        name="attention",
    )(qkvu3, qkvu3, qkvu3, q_gain.reshape(1, HEAD_DIM).astype(F32), k_gain.reshape(1, HEAD_DIM).astype(F32),
      *tabs)


def _s5_kernel(u_ref, k_ref, w_ref, v_ref, at_ref, y_ref, s_ref, x_ref, *, nchunk, nb):
    u = u_ref[0]
    s_ref[...] = _dot(u, w_ref[0])
    at = at_ref[0]
    afr, afi, abr, abi = (at[:, i * CHUNK:(i + 1) * CHUNK] for i in range(4))

    def body(c, carry):
        xfr, xfi, xbr, xbi = carry
        rf = pl.multiple_of(c * nb, nb)
        rb = pl.multiple_of((nchunk - 1 - c) * nb, nb)
        x_ref[pl.ds(rf, nb), 0:CHUNK] = xfr
        x_ref[pl.ds(rf, nb), CHUNK:2 * CHUNK] = xfi
        x_ref[pl.ds(rb, nb), 2 * CHUNK:3 * CHUNK] = xbr
        x_ref[pl.ds(rb, nb), 3 * CHUNK:4 * CHUNK] = xbi
        sfr = s_ref[pl.ds(rf, nb), 0:CHUNK]
        sfi = s_ref[pl.ds(rf, nb), CHUNK:2 * CHUNK]
        sbr = s_ref[pl.ds(rb, nb), 2 * CHUNK:3 * CHUNK]
        sbi = s_ref[pl.ds(rb, nb), 3 * CHUNK:4 * CHUNK]
        return (afr * xfr - afi * xfi + sfr, afr * xfi + afi * xfr + sfi,
                abr * xbr - abi * xbi + sbr, abr * xbi + abi * xbr + sbi)

    z = jnp.zeros((nb, CHUNK), F32)
    lax.fori_loop(0, nchunk, body, (z, z, z, z))
    x = _dot(u, k_ref[0]) + _dot(x_ref[...].astype(BF16), v_ref[0])
    y = x * (0.5 * (1.0 + jnp.tanh(math.sqrt(2.0 / math.pi) * (x + 0.044715 * (x * x * x)))))
    y_ref[0] = y.astype(y_ref.dtype)


def _s5_matrices(lam_re, lam_im, log_dt, b_re, b_im, c_re, c_im, d_skip):
    hi = lax.Precision.HIGHEST
    t = S5_T
    lr = jnp.minimum(lam_re, -1e-4)
    li = lam_im
    dt = jnp.exp(log_dt)[..., None]
    er = jnp.exp(lr * dt)
    abar_re = er * jnp.cos(li * dt)
    abar_im = er * jnp.sin(li * dt)
    nr = abar_re - 1.0
    den = lr * lr + li * li
    coef_re = (nr * lr + abar_im * li) / den
    coef_im = (abar_im * lr - nr * li) / den
    bb_re = coef_re[..., None] * b_re - coef_im[..., None] * b_im
    bb_im = coef_re[..., None] * b_im + coef_im[..., None] * b_re
    kk = jnp.arange(t + 1, dtype=F32)[:, None, None, None]
    mag = jnp.exp(kk * (lr * dt)[None])
    pw_re = mag * jnp.cos(kk * (li * dt)[None])
    pw_im = mag * jnp.sin(kk * (li * dt)[None])
    g, hh = d_skip.shape
    ct_re = c_re.transpose(0, 1, 3, 2)
    ct_im = c_im.transpose(0, 1, 3, 2)
    flat = lambda a: a.reshape(2, g, S5_STATE, hh * hh)
    bc_re = flat(bb_re[..., :, None] * ct_re[..., None, :] - bb_im[..., :, None] * ct_im[..., None, :])
    bc_im = flat(bb_re[..., :, None] * ct_im[..., None, :] + bb_im[..., :, None] * ct_re[..., None, :])
    lagk = (jnp.einsum('kdgp,dgpn->dgkn', pw_re, bc_re, precision=hi)
            - jnp.einsum('kdgp,dgpn->dgkn', pw_im, bc_im, precision=hi))
    ti = jnp.arange(t)
    lag = ti[None, :] - ti[:, None]
    sel = lambda m: m[None, :, :, None]
    skip = (jnp.eye(hh, dtype=F32)[None] * d_skip[:, None, :]).reshape(g, 1, 1, hh * hh)
    ktot = (jnp.where(sel(lag >= 0), lagk[0][:, jnp.clip(lag, 0, t)], 0.0)
            + jnp.where(sel(lag <= 0), lagk[1][:, jnp.clip(-lag, 0, t)], 0.0)
            + jnp.where(sel(lag == 0), skip, 0.0))
    ktot = ktot.reshape(g, t, t, hh, hh).transpose(0, 1, 3, 2, 4).reshape(g, t * hh, t * hh)

    def bsum(pw_r, pw_i, d):
        wr = pw_r[:, :, :, None] * bb_re[d][None] - pw_i[:, :, :, None] * bb_im[d][None]
        wi = pw_r[:, :, :, None] * bb_im[d][None] + pw_i[:, :, :, None] * bb_re[d][None]
        tr = lambda a: a.transpose(1, 0, 3, 2).reshape(g, t * hh, S5_STATE)
        return tr(wr), tr(wi)

    wf_re, wf_im = bsum(pw_re[:t, 0][::-1], pw_im[:t, 0][::-1], 0)
    wb_re, wb_im = bsum(pw_re[:t, 1], pw_im[:t, 1], 1)
    padl = lambda a: jnp.pad(a, ((0, 0), (0, 0), (0, CHUNK - S5_STATE)))
    wtot = jnp.concatenate([padl(wf_re), padl(wf_im), padl(wb_re), padl(wb_im)], axis=-1)

    def vmat(pw_r, pw_i, d):
        vr = pw_r[:, :, None, :] * c_re[d][None] - pw_i[:, :, None, :] * c_im[d][None]
        vi = pw_r[:, :, None, :] * c_im[d][None] + pw_i[:, :, None, :] * c_re[d][None]
        tr = lambda a: a.transpose(1, 3, 0, 2).reshape(g, S5_STATE, t * hh)
        return tr(vr), tr(-vi)

    vf_re, vf_im = vmat(pw_re[1:, 0], pw_im[1:, 0], 0)
    vb_re, vb_im = vmat(pw_re[1:, 1][::-1], pw_im[1:, 1][::-1], 1)
    padr = lambda a: jnp.pad(a, ((0, 0), (0, CHUNK - S5_STATE), (0, 0)))
    vtot = jnp.concatenate([padr(vf_re), padr(vf_im), padr(vb_re), padr(vb_im)], axis=1)
    padv = lambda a: jnp.pad(a, ((0, 0), (0, CHUNK - S5_STATE)))
    at = jnp.concatenate([padv(pw_re[t, 0]), padv(pw_im[t, 0]), padv(pw_re[t, 1]), padv(pw_im[t, 1])], axis=-1)
    return ktot.astype(BF16), wtot.astype(BF16), vtot.astype(BF16), at[:, None, :]


def s5_scan(u, mats):
    b, lp, _ = u.shape
    nchunk = lp // S5_T
    ktot, wtot, vtot, at = mats
    ug = u.astype(BF16).reshape(b, nchunk, S5_T, S5_GROUPS, S5_GROUP).transpose(3, 1, 0, 2, 4)
    ug = ug.reshape(S5_GROUPS, nchunk * b, S5_COLS)
    rows = nchunk * b
    per_g = lambda g: (g, 0, 0)
    y = pl.pallas_call(
        functools.partial(_s5_kernel, nchunk=nchunk, nb=b),
        grid=(S5_GROUPS,),
        in_specs=[pl.BlockSpec((1, rows, S5_COLS), per_g),
                  pl.BlockSpec((1, S5_COLS, S5_COLS), per_g),
                  pl.BlockSpec((1, S5_COLS, 4 * CHUNK), per_g),
                  pl.BlockSpec((1, 4 * CHUNK, S5_COLS), per_g),
                  pl.BlockSpec((1, 1, 4 * CHUNK), per_g)],
        out_specs=pl.BlockSpec((1, rows, S5_COLS), per_g),
        out_shape=jax.ShapeDtypeStruct((S5_GROUPS, rows, S5_COLS), BF16),
        scratch_shapes=[pltpu.VMEM((rows, 4 * CHUNK), F32), pltpu.VMEM((rows, 4 * CHUNK), F32)],
        compiler_params=_params("parallel"),
        name="s5_scan",
    )(ug, ktot, wtot, vtot, at)
    y = y.reshape(S5_GROUPS, nchunk, b, S5_T, S5_GROUP).transpose(2, 1, 3, 0, 4)
    return y.reshape(b * lp, S5_GROUPS * S5_GROUP)


def _s5_glu_kernel(y_ref, w_ref, b_ref, o_ref):
    y = y_ref[...]
    z = _dot(y, w_ref[...]) + b_ref[...]
    o_ref[...] = (y.astype(F32) * _sigmoid(z)).astype(o_ref.dtype)


def s5_glu(y, w, bias):
    m, n = y.shape
    tm = _pick(m, (1024, 512, 384, 256, 128))
    return pl.pallas_call(
        _s5_glu_kernel,
        grid=(m // tm,),
        in_specs=[pl.BlockSpec((tm, n), lambda i: (i, 0)),
                  pl.BlockSpec((n, n), lambda i: (0, 0)),
                  pl.BlockSpec((1, n), lambda i: (0, 0))],
        out_specs=pl.BlockSpec((tm, n), lambda i: (i, 0)),
        out_shape=jax.ShapeDtypeStruct((m, n), BF16),
        compiler_params=_params("parallel"),
        name="s5_glu",
    )(y, w, bias.reshape(1, n).astype(F32))


def _head_norm(x, gain):
    xc = x - jnp.mean(x, axis=-1, keepdims=True)
    return xc * lax.rsqrt(jnp.mean(xc * xc, axis=-1, keepdims=True) + NORM_EPS) * gain


def _ret_kernel(q_ref, k_ref, v_ref, g_ref, c_ref, s_ref, lg_ref, gn_ref, o_ref,
                qs_ref, ks_ref, vs_ref, vt_ref, af_ref, ab_ref, st_ref, *, nchunk):
    c = c_ref[...]
    s = s_ref[...]
    ii = lax.broadcasted_iota(jnp.int32, (CHUNK, CHUNK), 0).astype(F32)
    jj = lax.broadcasted_iota(jnp.int32, (CHUNK, CHUNK), 1).astype(F32)
    lane = (lax.broadcasted_iota(jnp.int32, (1, nchunk * CHUNK), 1) & (CHUNK - 1)).astype(F32)
    diff = ii - jj
    lanes = [slice(hh * HEAD_DIM, (hh + 1) * HEAD_DIM) for hh in range(HEADS_PER_STEP)]
    consts = []
    for hh, ln in enumerate(lanes):
        q = q_ref[0, :, ln].astype(F32)
        qs_ref[:, ln] = ((q * c + pltpu.roll(q, HEAD_DIM // 2, 1) * s) * HEAD_DIM ** -0.5).astype(BF16)
        k = k_ref[0, :, ln].astype(F32)
        ks_ref[:, ln] = (k * c + pltpu.roll(k, HEAD_DIM // 2, 1) * s).astype(BF16)
        v = v_ref[0, :, ln].astype(F32)
        vs_ref[:, ln] = v.astype(BF16)
        v_t = v.T
        lgf = lg_ref[hh, 0:1, :]
        lgb = lg_ref[hh, 1:2, :]
        vt_ref[2 * hh] = (v_t * jnp.exp((CHUNK - 1 - lane) * lgf[:, 0:1])).astype(BF16)
        vt_ref[2 * hh + 1] = (v_t * jnp.exp(lane * lgb[:, 0:1])).astype(BF16)
        fwd = (jnp.where(diff >= 0, jnp.exp(jnp.where(diff >= 0, diff, 0.0) * lgf), 0.0),
               jnp.exp((ii + 1.0) * lgf), jnp.exp(CHUNK * lgf))
        bwd = (jnp.where(diff < 0, jnp.exp(jnp.where(diff < 0, -diff, 0.0) * lgb), 0.0),
               jnp.exp((CHUNK - ii) * lgb), jnp.exp(CHUNK * lgb))
        consts.append((fwd, bwd))
    st_ref[...] = jnp.zeros_like(st_ref)

    def chunk(r, ln, slot, cst, out_ref):
        dec, xi, gc = cst
        qc = qs_ref[pl.ds(r, CHUNK), ln]
        kc = ks_ref[pl.ds(r, CHUNK), ln]
        vc = vs_ref[pl.ds(r, CHUNK), ln]
        state_t = st_ref[slot]
        both = _dot_nt(qc, jnp.concatenate([kc, state_t.astype(BF16)], axis=0))
        sc = both[:, :CHUNK] * dec
        res = _dot(jnp.concatenate([sc.astype(BF16), vt_ref[slot, :, pl.ds(r, CHUNK)]], axis=0),
                   jnp.concatenate([vc, kc], axis=1))
        out_ref[pl.ds(r, CHUNK), ln] = res[:CHUNK, :HEAD_DIM] + both[:, CHUNK:] * xi
        st_ref[slot] = gc * state_t + res[CHUNK:, HEAD_DIM:]

    def body(t, carry):
        rf = pl.multiple_of(t * CHUNK, CHUNK)
        rb = pl.multiple_of((nchunk - 1 - t) * CHUNK, CHUNK)
        for hh, ln in enumerate(lanes):
            chunk(rf, ln, 2 * hh, consts[hh][0], af_ref)
            chunk(rb, ln, 2 * hh + 1, consts[hh][1], ab_ref)
        return carry

    lax.fori_loop(0, nchunk, body, 0)
    for ln in lanes:
        gate = g_ref[0, :, ln].astype(F32)
        y = _head_norm(af_ref[:, ln] + ab_ref[:, ln], gn_ref[:, ln])
        o_ref[0, :, ln] = (y * (gate * _sigmoid(gate))).astype(o_ref.dtype)


def retention(proj3, log_gamma, gain, cos_t, sin_t):
    b, lp, _ = proj3.shape
    hd = HEAD_DIM
    hps = HEADS_PER_STEP
    wd = hps * hd
    nblk = RET_HEADS // hps
    lg = jnp.broadcast_to(log_gamma.T[:, :, None], (RET_HEADS, 2, hd)).astype(F32)
    blk = lambda off: pl.BlockSpec((1, lp, wd), lambda bi, hi: (bi, 0, off + hi))
    full = lambda bi, hi: (0, 0)
    return pl.pallas_call(
        functools.partial(_ret_kernel, nchunk=lp // CHUNK),
        grid=(b, nblk),
        in_specs=[blk(0), blk(nblk), blk(2 * nblk), blk(3 * nblk),
                  pl.BlockSpec((lp, hd), full), pl.BlockSpec((lp, hd), full),
                  pl.BlockSpec((hps, 2, hd), lambda bi, hi: (hi, 0, 0)),
                  pl.BlockSpec((1, wd), lambda bi, hi: (0, hi))],
        out_specs=pl.BlockSpec((1, lp, wd), lambda bi, hi: (bi, 0, hi)),
        out_shape=jax.ShapeDtypeStruct((b, lp, RET_HEADS * hd), BF16),
        scratch_shapes=[pltpu.VMEM((lp, wd), BF16), pltpu.VMEM((lp, wd), BF16), pltpu.VMEM((lp, wd), BF16),
                        pltpu.VMEM((2 * hps, hd, lp), BF16), pltpu.VMEM((lp, wd), F32), pltpu.VMEM((lp, wd), F32),
                        pltpu.VMEM((2 * hps, hd, hd), F32)],
        compiler_params=_params("parallel", "parallel"),
        name="retention",
    )(proj3, proj3, proj3, proj3, cos_t, sin_t, lg, gain.reshape(1, RET_HEADS * hd).astype(F32))


def _mlstm_gate_tables(li, lf, tri_sum):
    bt = jnp.dot(lf, tri_sum, preferred_element_type=F32, precision=lax.Precision.HIGHEST)
    bt_last = jnp.sum(lf, axis=1, keepdims=True)
    a = bt_last - bt + li
    m_loc = jnp.max(a, axis=1, keepdims=True)
    return bt, jnp.exp(a - m_loc), m_loc, bt_last, li - bt


def _mlstm_chunk_t(kc, qtc, vtc, vwc, bt_row, w_row, m_loc, bt_last, colb, mask_t, ct_prev, n_prev, m_prev):
    d, c, pk = HEAD_DIM, CHUNK, BF16_ROWS
    dlog_t = jnp.where(mask_t, bt_row + colb, -jnp.inf)
    g_row = bt_row + m_prev
    m_t = jnp.maximum(g_row, jnp.max(dlog_t, axis=0, keepdims=True))
    r1 = _dot(jnp.concatenate([kc, ct_prev.astype(BF16), jnp.broadcast_to(n_prev, (pk, d)).astype(BF16)], axis=0),
              qtc)
    s_t = r1[:c] * jnp.exp(dlog_t - m_t)
    w_int = jnp.exp(g_row - m_t)
    r2 = _dot(jnp.concatenate([vtc, vwc, jnp.broadcast_to(w_row, (pk, c)).astype(BF16)], axis=0),
              jnp.concatenate([s_t.astype(BF16), kc], axis=1))
    num_t = r2[:d, :c] + w_int * r1[c:c + d]
    den_t = jnp.sum(s_t, axis=0, keepdims=True) + w_int * r1[c + d:c + d + 1]
    out_t = num_t / jnp.maximum(jnp.abs(den_t), jnp.exp(-m_t))
    m_new = jnp.maximum(bt_last + m_prev, m_loc)
    f_prev = jnp.exp(bt_last + m_prev - m_new)
    f_loc = jnp.exp(m_loc - m_new)
    c_new = f_prev * ct_prev + f_loc * r2[d:2 * d, c:]
    n_new = f_prev * n_prev + f_loc * r2[2 * d:2 * d + 1, c:]
    return out_t, c_new, n_new, m_new


def _mlstm_kernel(mu_ref, mo_ref, gt_ref, gb_ref, cw_ref, cb_ref, wk_ref, wqt_ref, wvt_ref, gn_ref, o_ref,
                  ks_ref, qt_ref, vt_ref, vw_ref, row_ref, d_ref, colb_ref, aft_ref, abt_ref,
                  cs_ref, ns_ref, ms_ref, *, lp, pad, nchunk):
    hd = HEAD_DIM
    lanes = [slice(hh * hd, (hh + 1) * hd) for hh in range(HEADS_PER_STEP)]
    valid_row = lax.broadcasted_iota(jnp.int32, (lp, 1), 0) >= pad
    valid_col = lax.broadcasted_iota(jnp.int32, (1, lp), 1) >= pad
    pos = (lax.broadcasted_iota(jnp.int32, (nchunk, CHUNK), 0) * CHUNK
           + lax.broadcasted_iota(jnp.int32, (nchunk, CHUNK), 1))
    valid_pos = pos >= pad
    ii = lax.broadcasted_iota(jnp.int32, (CHUNK, CHUNK), 0)
    jj = lax.broadcasted_iota(jnp.int32, (CHUNK, CHUNK), 1)
    eye = ii == jj
    upper = ii <= jj
    lower = ii >= jj
    ones = jnp.ones((CHUNK, CHUNK), BF16)
    for hh, ln in enumerate(lanes):
        mu = mu_ref[0, :, ln].astype(F32)
        conv = cb_ref[:, ln]
        for j in range(CONV_W):
            conv = conv + cw_ref[j:j + 1, ln] * pltpu.roll(mu, (CONV_W // 2 - j) % lp, 0)
        uc = (conv * _sigmoid(conv)).astype(BF16)
        ks_ref[:, ln] = jnp.where(valid_row, _dot(uc, wk_ref[hh]) * hd ** -0.5, 0.0).astype(BF16)
        qt_ref[ln, :] = jnp.where(valid_col, _dot_nt(wqt_ref[hh], uc), 0.0).astype(BF16)
        v_t = jnp.where(valid_col, _dot_nt(wvt_ref[hh], mu.astype(BF16)), 0.0)
        vt_ref[ln, :] = v_t.astype(BF16)

        for d, tri_sum in enumerate((upper, lower)):
            g_i = gt_ref[0, hh, 2 * d] + gb_ref[hh, 2 * d:2 * d + 1, 0:1]
            g_f = gt_ref[0, hh, 2 * d + 1] + gb_ref[hh, 2 * d + 1:2 * d + 2, 0:1]
            li = jnp.where(valid_pos, g_i, NEG_GATE)
            lf = jnp.where(valid_pos, jnp.minimum(g_f, 0.0) - jnp.log(1.0 + jnp.exp(-jnp.abs(g_f))), 0.0)
            bt, w, m_loc, bt_last, colv = _mlstm_gate_tables(li, lf, jnp.where(tri_sum, 1.0, 0.0))
            slot = 2 * hh + d
            row_ref[4 * slot + 0] = bt
            row_ref[4 * slot + 1] = w
            row_ref[4 * slot + 2] = jnp.broadcast_to(m_loc, (nchunk, CHUNK))
            row_ref[4 * slot + 3] = jnp.broadcast_to(bt_last, (nchunk, CHUNK))
            for n in range(nchunk):
                cols = slice(n * CHUNK, (n + 1) * CHUNK)
                vw_ref[slot, :, cols] = (v_t[:, cols] * w[n:n + 1, :]).astype(BF16)
                d_ref[cols, :] = jnp.where(eye, colv[n:n + 1, :], 0.0)
            diag = d_ref[...]
            d_hi = diag.astype(BF16)
            d_lo = (diag - d_hi.astype(F32)).astype(BF16)
            colb_ref[slot] = _dot(d_hi, ones) + _dot(d_lo, ones)
    cs_ref[...] = jnp.zeros_like(cs_ref)
    ns_ref[...] = jnp.zeros_like(ns_ref)
    ms_ref[...] = jnp.zeros_like(ms_ref)

    def run(n, r, ln, slot, mask_t, out_ref):
        row = lambda kind: row_ref[4 * slot + kind, pl.ds(n, 1), :]
        out_t, c_s, n_s, m_s = _mlstm_chunk_t(
            ks_ref[pl.ds(r, CHUNK), ln], qt_ref[ln, pl.ds(r, CHUNK)], vt_ref[ln, pl.ds(r, CHUNK)],
            vw_ref[slot, :, pl.ds(r, CHUNK)],
            row(0), row(1), row(2)[:, 0:1], row(3)[:, 0:1], colb_ref[slot, pl.ds(r, CHUNK), :], mask_t,
            cs_ref[slot], ns_ref[slot], ms_ref[slot][:, 0:1])
        out_ref[ln, pl.ds(r, CHUNK)] = out_t
        cs_ref[slot] = c_s
        ns_ref[slot] = n_s
        ms_ref[slot] = jnp.broadcast_to(m_s, (1, hd))

    def body(t, carry):
        tb = nchunk - 1 - t
        rf = pl.multiple_of(t * CHUNK, CHUNK)
        rb = pl.multiple_of(tb * CHUNK, CHUNK)
        for hh, ln in enumerate(lanes):
            run(t, rf, ln, 2 * hh, upper, aft_ref)
            run(tb, rb, ln, 2 * hh + 1, lower, abt_ref)
        return carry

    lax.fori_loop(0, nchunk, body, 0)
    for ln in lanes:
        x_t = aft_ref[ln, :] + abt_ref[ln, :]
        xc = x_t - jnp.mean(x_t, axis=0, keepdims=True)
        y = (xc * lax.rsqrt(jnp.mean(xc * xc, axis=0, keepdims=True) + NORM_EPS)).T
        o_ref[0, :, ln] = (y * gn_ref[:, ln] * _sigmoid(mo_ref[0, :, ln].astype(F32))).astype(o_ref.dtype)


def mlstm(proj3, gates, gate_b, conv_w, conv_b, wq, wk, wv, gain, *, pad, mu_col0, mo_col0):
    b, lp, _ = proj3.shape
    hd = HEAD_DIM
    hps = HEADS_PER_STEP
    wd = hps * hd
    nchunk = lp // CHUNK
    gt = gates.reshape(b, lp, 4, ML_HEADS).transpose(0, 3, 2, 1).reshape(b, ML_HEADS, 4, nchunk, CHUNK)
    gb = jnp.broadcast_to(gate_b.T[:, :, None], (ML_HEADS, 4, hd)).astype(F32)
    blk = lambda off: pl.BlockSpec((1, lp, wd), lambda bi, hi: (bi, 0, off // hps + hi))
    per_h = lambda bi, hi: (hi, 0, 0)
    vec = pl.BlockSpec((1, wd), lambda bi, hi: (0, hi))
    sq = pl.BlockSpec((hps, hd, hd), per_h)
    tr = lambda w: jnp.swapaxes(w, 1, 2).astype(BF16)
    return pl.pallas_call(
        functools.partial(_mlstm_kernel, lp=lp, pad=pad, nchunk=nchunk),
        grid=(b, ML_HEADS // hps),
        in_specs=[blk(mu_col0), blk(mo_col0),
                  pl.BlockSpec((1, hps, 4, nchunk, CHUNK), lambda bi, hi: (bi, hi, 0, 0, 0)),
                  pl.BlockSpec((hps, 4, hd), per_h),
                  pl.BlockSpec((CONV_W, wd), lambda bi, hi: (0, hi)),
                  vec, sq, sq, sq, vec],
        out_specs=pl.BlockSpec((1, lp, wd), lambda bi, hi: (bi, 0, hi)),
        out_shape=jax.ShapeDtypeStruct((b, lp, ML_HEADS * hd), BF16),
        scratch_shapes=[pltpu.VMEM((lp, wd), BF16), pltpu.VMEM((wd, lp), BF16), pltpu.VMEM((wd, lp), BF16),
                        pltpu.VMEM((2 * hps, hd, lp), BF16),
                        pltpu.VMEM((8 * hps, nchunk, CHUNK), F32), pltpu.VMEM((lp, CHUNK), F32),
                        pltpu.VMEM((2 * hps, lp, CHUNK), F32),
                        pltpu.VMEM((wd, lp), F32), pltpu.VMEM((wd, lp), F32),
                        pltpu.VMEM((2 * hps, hd, hd), F32), pltpu.VMEM((2 * hps, 1, hd), F32),
                        pltpu.VMEM((2 * hps, 1, hd), F32)],
        compiler_params=_params("parallel", "parallel"),
        name="mlstm",
    )(proj3, proj3, gt, gb, conv_w.astype(F32), conv_b.reshape(1, -1).astype(F32),
      wk.astype(BF16), tr(wq), tr(wv), gain.reshape(1, -1).astype(F32))


def _rope_freqs(dim):
    return ROPE_THETA ** (-jnp.arange(dim // 2, dtype=F32) / (dim // 2))


def _axial_tables(n_tok, pad):
    rows = n_tok // GRID_W
    row = jnp.concatenate([jnp.zeros((pad,), F32), -jnp.ones((N_META,), F32),
                           jnp.repeat(jnp.arange(rows, dtype=F32), GRID_W)])
    col = jnp.concatenate([jnp.zeros((pad,), F32), jnp.arange(N_META, dtype=F32),
                           jnp.tile(jnp.arange(GRID_W, dtype=F32), rows)])
    f = _rope_freqs(HEAD_DIM // 2)
    ang = jnp.concatenate([row[:, None] * f[None, :]] * 2 + [col[:, None] * f[None, :]] * 2, axis=-1)
    first = (jnp.arange(HEAD_DIM) % 64) < 32
    sin = jnp.sin(ang)
    return jnp.cos(ang), jnp.where(first, -sin, 0.0), jnp.where(first, 0.0, sin)


def _linear_tables(l, pad):
    pos = jnp.concatenate([jnp.zeros((pad,), F32), jnp.arange(l, dtype=F32)])
    ang = pos[:, None] * _rope_freqs(HEAD_DIM)[None, :]
    ang = jnp.concatenate([ang, ang], axis=-1)
    sin = jnp.sin(ang)
    return jnp.cos(ang), jnp.where(jnp.arange(HEAD_DIM) < HEAD_DIM // 2, -sin, sin)


def _even_mixer_parts(hn, w_in_all, j, q_norm, k_norm, s5_params, glu_w, glu_b, tabs, *, b, lp, pad):
    att_w = ATT_HEADS * HEAD_DIM
    u0 = att_w + 2 * ATT_KV_HEADS * HEAD_DIM
    qkvu = matmul_wcast(hn, w_in_all, j, w_in_all.shape[2])
    qkvu3 = qkvu.reshape(b, lp, -1)
    att = attention(qkvu3, q_norm, k_norm, tabs, pad=pad).reshape(b * lp, att_w)
    y = s5_scan(qkvu3[:, :, u0:], _s5_matrices(*s5_params))
    ssm = s5_glu(y, glu_w.astype(BF16), glu_b)
    return [att, ssm]


def _odd_mixer_parts(hn, w_in_all, j, ret_log_decay, ret_norm, conv_w, conv_b, wq, wk, wv, gate_b, ml_norm,
                     tabs, *, b, lp, pad):
    ret_w = RET_HEADS * HEAD_DIM
    ml_w = ML_HEADS * HEAD_DIM
    main = 4 * ret_w + 2 * ml_w
    n_gate = w_in_all.shape[2] - main
    proj = matmul_wcast(hn, w_in_all, j, main, out_dtype=BF16)
    proj3 = proj.reshape(b, lp, main)
    w_gate = jnp.pad(w_in_all[j, :, main:].astype(BF16), ((0, 0), (0, CHUNK - n_gate)))
    gates = matmul(hn, w_gate)[:, :n_gate].reshape(b, lp, n_gate)
    log_gamma = -jnp.abs(ret_log_decay.astype(F32))
    ret = retention(proj3, log_gamma, ret_norm, *tabs)
    nblk = ret_w // HEAD_DIM
    hm = mlstm(proj3, gates, gate_b, conv_w, conv_b, wq, wk, wv, ml_norm,
               pad=pad, mu_col0=4 * nblk, mo_col0=4 * nblk + ml_w // HEAD_DIM)
    return [ret.reshape(b * lp, ret_w), hm.reshape(b * lp, ml_w)]


def kernel(x, meta_tokens, norm_gains, mlp_w1, mlp_w2, even_w_in, even_w_out, att_q_norm, att_k_norm, s5_lam_re, s5_lam_im, s5_log_dt, s5_b_re, s5_b_im, s5_c_re, s5_c_im, s5_d, s5_glu_w, s5_glu_b, odd_w_in, odd_w_out, ret_log_decay, ret_norm, ml_conv_w, ml_conv_b, ml_wq, ml_wk, ml_wv, ml_gate_b, ml_norm):
    b, n_tok, d_model = x.shape
    l = n_tok + N_META
    pad = (-l) % CHUNK
    lp = l + pad
    depth = norm_gains.shape[0]
    h = jnp.concatenate([jnp.zeros((b, pad, d_model), x.dtype),
                         jnp.broadcast_to(meta_tokens.astype(x.dtype)[None], (b, N_META, d_model)), x], axis=1)
    h = h.reshape(b * lp, d_model)
    axial = _axial_tables(n_tok, pad)
    linear = _linear_tables(l, pad)
    dims = dict(b=b, lp=lp, pad=pad)
    hn = rmsnorm(h, norm_gains[0, 0])
    w2_bf16 = mlp_w2.astype(BF16)
    even_out_bf16 = even_w_out.astype(BF16)
    odd_out_bf16 = odd_w_out.astype(BF16)
    for i in range(depth):
        j = i // 2
        if i % 2 == 0:
            s5_params = (s5_lam_re[j], s5_lam_im[j], s5_log_dt[j], s5_b_re[j], s5_b_im[j], s5_c_re[j],
                         s5_c_im[j], s5_d[j])
            parts = _even_mixer_parts(hn, even_w_in, j, att_q_norm[j], att_k_norm[j],
                                      s5_params, s5_glu_w[j], s5_glu_b[j], axial, **dims)
            w_out = even_out_bf16
        else:
            parts = _odd_mixer_parts(hn, odd_w_in, j, ret_log_decay[j], ret_norm[j],
                                     ml_conv_w[j], ml_conv_b[j], ml_wq[j], ml_wk[j], ml_wv[j], ml_gate_b[j],
                                     ml_norm[j], linear, **dims)
            w_out = odd_out_bf16
        h, hn = matmul_norm_res(parts, w_out, j, norm_gains[i, 1], h, norm_gains[i, 2], lp=lp, pad=pad)
        hid = matmul_wcast(hn, mlp_w1, i, mlp_w1.shape[2], relu2=True, out_dtype=BF16)
        last = i + 1 == depth
        next_gain = None if last else norm_gains[i + 1, 0]
        drop = pad + N_META if last and n_tok % CHUNK == 0 else 0
        h, hn = matmul_norm_res([hid], w2_bf16, i, norm_gains[i, 3], h, next_gain, lp=lp, pad=pad, drop_head=drop)
    if drop:
        return h.reshape(b, n_tok, d_model)
    return h.reshape(b, lp, d_model)[:, pad + N_META:]
```

```python
import functools
import math

import jax
import jax.numpy as jnp
from jax import lax
from jax.experimental import pallas as pl
from jax.experimental.pallas import tpu as pltpu

F32 = jnp.float32
BF16 = jnp.bfloat16

N_META = 16
GRID_W = 64
CHUNK = 128
HEAD_DIM = 128
NORM_EPS = 1e-6
ROPE_THETA = 10000.0
ATT_HEADS = 12
ATT_KV_HEADS = 4
ATT_GROUP = ATT_HEADS // ATT_KV_HEADS
S5_GROUP = 16
S5_GROUPS = 32
S5_STATE = 64
S5_T = 16
S5_COLS = S5_T * S5_GROUP
RET_HEADS = 8
ML_HEADS = 8
CONV_W = 5
NEG_GATE = -1e4
HEADS_PER_STEP = 4
BF16_ROWS = 16
VMEM_LIMIT_BYTES = 56 * 1024 * 1024


def _pick(n, cands):
    for c in cands:
        if n % c == 0:
            return c
    raise ValueError(f"no tile for {n} in {cands}")


def _params(*sem):
    return pltpu.CompilerParams(dimension_semantics=sem, vmem_limit_bytes=VMEM_LIMIT_BYTES)


def _dot(a, b):
    return jnp.dot(a, b, preferred_element_type=F32)


def _dot_nt(a, b):
    return lax.dot_general(a, b, (((1,), (1,)), ((), ())), preferred_element_type=F32)


def _dot_tn(a, b):
    return lax.dot_general(a, b, (((0,), (0,)), ((), ())), preferred_element_type=F32)


def _sigmoid(x):
    return 1.0 / (1.0 + jnp.exp(-x))


def _rmsnorm_kernel(x_ref, g_ref, o_ref):
    x = x_ref[...]
    ms = jnp.mean(x * x, axis=-1, keepdims=True)
    o_ref[...] = (x * lax.rsqrt(ms + NORM_EPS) * g_ref[...]).astype(o_ref.dtype)


def rmsnorm(x, gain):
    m, d = x.shape
    tm = _pick(m, (1024, 512, 384, 256, 128))
    return pl.pallas_call(
        _rmsnorm_kernel,
        grid=(m // tm,),
        in_specs=[pl.BlockSpec((tm, d), lambda i: (i, 0)), pl.BlockSpec((1, d), lambda i: (0, 0))],
        out_specs=pl.BlockSpec((tm, d), lambda i: (i, 0)),
        out_shape=jax.ShapeDtypeStruct((m, d), BF16),
        compiler_params=_params("parallel"),
        name="rmsnorm",
    )(x, gain.reshape(1, d).astype(F32))


def _matmul_kernel(a_ref, w_ref, o_ref, *, relu2):
    y = _dot(a_ref[...], w_ref[...])
    if relu2:
        y = jnp.square(jnp.maximum(y, 0.0))
    o_ref[...] = y.astype(o_ref.dtype)


def matmul(a, w, *, relu2=False, out_dtype=F32):
    m, k = a.shape
    n = w.shape[1]
    tm = _pick(m, (1024, 512, 384, 256, 128))
    tn = _pick(n, (2048, 1536, 1280, 1024, 512, 256, 128))
    return pl.pallas_call(
        functools.partial(_matmul_kernel, relu2=relu2),
        grid=(m // tm, n // tn),
        in_specs=[pl.BlockSpec((tm, k), lambda i, j: (i, 0)),
                  pl.BlockSpec((k, tn), lambda i, j: (0, j))],
        out_specs=pl.BlockSpec((tm, tn), lambda i, j: (i, j)),
        out_shape=jax.ShapeDtypeStruct((m, n), out_dtype),
        compiler_params=_params("parallel", "parallel"),
        name="matmul",
    )(a, w)


def _matmul_wcast_kernel(a_ref, w_ref, o_ref, wb_ref, *, relu2):
    @pl.when(pl.program_id(1) == 0)
    def _():
        wb_ref[...] = w_ref[...].astype(BF16)

    y = _dot(a_ref[...], wb_ref[...])
    if relu2:
        y = jnp.square(jnp.maximum(y, 0.0))
    o_ref[...] = y.astype(o_ref.dtype)


def matmul_wcast(a, w3, layer, n_cols, *, relu2=False, out_dtype=F32):
    m, k = a.shape
    tm = _pick(m, (1024, 512, 384, 256, 128))
    tn = _pick(n_cols, (1024, 512, 256, 128))
    return pl.pallas_call(
        functools.partial(_matmul_wcast_kernel, relu2=relu2),
        grid=(n_cols // tn, m // tm),
        in_specs=[pl.BlockSpec((tm, k), lambda j, i: (i, 0)),
                  pl.BlockSpec((None, k, tn), lambda j, i: (layer, 0, j))],
        out_specs=pl.BlockSpec((tm, tn), lambda j, i: (i, j)),
        out_shape=jax.ShapeDtypeStruct((m, n_cols), out_dtype),
        scratch_shapes=[pltpu.VMEM((k, tn), BF16)],
        compiler_params=_params("parallel", "arbitrary"),
        name="matmul_wcast",
    )(a, w3)


def _matmul_norm_res_kernel(*refs, widths, nk, tm, lp, pad, emit_next, mask_pad):
    na = len(widths)
    a_refs = refs[:na]
    w_ref, g_ref, h_ref = refs[na:na + 3]
    rest = refs[na + 3:]
    if emit_next:
        g2_ref, o_ref, n_ref = rest
    else:
        o_ref, = rest
    halves = [slice(0, tm // 2), slice(tm // 2, tm)]

    def finish(y, rows):
        ms = jnp.mean(y * y, axis=-1, keepdims=True)
        out = h_ref[rows, :] + y * lax.rsqrt(ms + NORM_EPS) * g_ref[...]
        if mask_pad:
            start = pl.program_id(0) * tm
            row = start + rows.start + lax.broadcasted_iota(jnp.int32, (rows.stop - rows.start, 1), 0)
            rel0 = row - (start // lp) * lp
            rel1 = row - ((start + tm - 1) // lp) * lp
            out = jnp.where(((rel0 >= 0) & (rel0 < pad)) | ((rel1 >= 0) & (rel1 < pad)), 0.0, out)
        o_ref[rows, :] = out
        if emit_next:
            ms2 = jnp.mean(out * out, axis=-1, keepdims=True)
            n_ref[rows, :] = (out * lax.rsqrt(ms2 + NORM_EPS) * g2_ref[...]).astype(n_ref.dtype)

    if nk == 1:
        for rows in halves:
            part = None
            off = 0
            for a_ref, wd in zip(a_refs, widths):
                d = _dot(a_ref[rows, :], w_ref[off:off + wd, :])
                part = d if part is None else part + d
                off += wd
            finish(part, rows)
    else:
        kk = pl.program_id(1)

        @pl.when(kk == 0)
        def _():
            o_ref[...] = _dot(a_refs[0][...], w_ref[...])

        @pl.when(kk > 0)
        def _():
            o_ref[...] += _dot(a_refs[0][...], w_ref[...])

        @pl.when(kk == nk - 1)
        def _():
            for rows in halves:
                finish(o_ref[rows, :], rows)


def matmul_norm_res(parts, w3, layer, gain, h, next_gain, *, lp, pad, drop_head=0):
    m, n = h.shape
    widths = tuple(p.shape[1] for p in parts)
    k = sum(widths)
    emit_next = next_gain is not None
    wide = bool(drop_head) and len(parts) == 1 and k > 2048 and (lp - drop_head) % 1024 == 0
    if len(parts) > 1 or k <= 2048:
        tk, nk = k, 1
    else:
        tk = 1024 if wide else 2048
        nk = k // tk
        widths = (tk,)
    vec = pl.BlockSpec((1, n), lambda i, j: (0, 0))
    if drop_head:
        assert not emit_next and pad <= drop_head
        keep = lp - drop_head
        tm = 1024 if wide else _pick(keep, (512, 384, 256, 128))
        per_b = keep // tm
        n_row_blocks = (m // lp) * per_b
        row0 = lambda i: pl.multiple_of((i // per_b) * lp + drop_head + (i % per_b) * tm, CHUNK)
        in_specs = [pl.BlockSpec((pl.Element(tm), pl.Element(wd)), functools.partial(
            lambda i, j, wd: (row0(i), j * wd), wd=wd)) for wd in widths]
        h_spec = pl.BlockSpec((pl.Element(tm), pl.Element(n)), lambda i, j: (row0(i), 0))
    else:
        tm = _pick(m, (512, 384, 256, 128))
        n_row_blocks = m // tm
        in_specs = [pl.BlockSpec((tm, wd), lambda i, j: (i, j)) for wd in widths]
        h_spec = pl.BlockSpec((tm, n), lambda i, j: (i, 0))
    assert tm <= lp
    row_blk = pl.BlockSpec((tm, n), lambda i, j: (i, 0))
    in_specs += [pl.BlockSpec((None, tk, n), lambda i, j: (layer, j, 0)), vec, h_spec]
    args = [*parts, w3, gain.reshape(1, n).astype(F32), h]
    out_specs = [row_blk]
    out_shape = [jax.ShapeDtypeStruct((n_row_blocks * tm, n), F32)]
    if emit_next:
        in_specs.append(vec)
        args.append(next_gain.reshape(1, n).astype(F32))
        out_specs.append(row_blk)
        out_shape.append(jax.ShapeDtypeStruct((n_row_blocks * tm, n), BF16))
    res = pl.pallas_call(
        functools.partial(_matmul_norm_res_kernel, widths=widths, nk=nk, tm=tm, lp=lp, pad=pad,
                          emit_next=emit_next, mask_pad=not drop_head),
        grid=(n_row_blocks, nk),
        in_specs=in_specs,
        out_specs=out_specs,
        out_shape=out_shape,
        compiler_params=_params("parallel", "arbitrary"),
        name="matmul_norm_res",
    )(*args)
    return (res[0], res[1]) if emit_next else (res[0], None)


def _rope_axial(x, c, sa, sb):
    return x * c + pltpu.roll(x, HEAD_DIM - 32, 1) * sa + pltpu.roll(x, 32, 1) * sb


def _attn_kernel(q_ref, k_ref, v_ref, qg_ref, kg_ref, c_ref, sa_ref, sb_ref, o_ref,
                 ks_ref, vt_ref, sta_ref, stb_ref, pa_ref, pb_ref, *, lp, pad, tq):
    k = k_ref[0]
    k = k * lax.rsqrt(jnp.mean(k * k, axis=-1, keepdims=True) + NORM_EPS) * kg_ref[...]
    ks_ref[...] = _rope_axial(k, c_ref[...], sa_ref[...], sb_ref[...]).astype(BF16)
    vt_ref[...] = v_ref[0].T.astype(BF16)
    for p_ref in (pa_ref, pb_ref):
        p_ref[0:pad, :] = jnp.zeros((pad, p_ref.shape[1]), BF16)
    scale = HEAD_DIM ** -0.5 * math.log2(math.e)

    def scores(r0, rows, st_ref):
        c = c_ref[pl.ds(r0, rows), :]
        sa = sa_ref[pl.ds(r0, rows), :]
        sb = sb_ref[pl.ds(r0, rows), :]
        qs = []
        for g in range(ATT_GROUP):
            q = q_ref[0, pl.ds(r0, rows), g * HEAD_DIM:(g + 1) * HEAD_DIM]
            q = q * lax.rsqrt(jnp.mean(q * q, axis=-1, keepdims=True) + NORM_EPS) * qg_ref[...]
            qs.append((_rope_axial(q, c, sa, sb) * scale).astype(BF16))
        qall = jnp.concatenate(qs, axis=0)
        st_ref[:, 0:ATT_GROUP * rows] = _dot_nt(ks_ref[pad:, :], qall)

    def attend(r0, rows, st_ref, p_ref):
        n = ATT_GROUP * rows
        st = st_ref[:, 0:n]
        p = jnp.exp2(st - jnp.max(st, axis=0, keepdims=True))
        l = jnp.sum(p, axis=0, keepdims=True)
        p_ref[pad:, 0:n] = p.astype(BF16)
        o = (_dot(vt_ref[...], p_ref[:, 0:n]) / l).T
        for g in range(ATT_GROUP):
            o_ref[0, pl.ds(r0, rows), g * HEAD_DIM:(g + 1) * HEAD_DIM] = (
                o[g * rows:(g + 1) * rows].astype(o_ref.dtype))

    bufs = ((sta_ref, pa_ref), (stb_ref, pb_ref))
    nbig = lp // tq
    blocks = [(i * tq, tq) for i in range(nbig)]
    if lp % tq:
        blocks.append((nbig * tq, lp % tq))
    npairs = max(0, (nbig - 1) // 2)
    scores(0, blocks[0][1], sta_ref)

    def body(u, carry):
        r, r1, r2 = (pl.multiple_of((2 * u + i) * tq, tq) for i in range(3))
        scores(r1, tq, stb_ref)
        attend(r, tq, sta_ref, pa_ref)
        scores(r2, tq, sta_ref)
        attend(r1, tq, stb_ref, pb_ref)
        return carry

    lax.fori_loop(0, npairs, body, 0)
    for i in range(2 * npairs, len(blocks)):
        if i + 1 < len(blocks):
            scores(*blocks[i + 1], bufs[(i + 1) % 2][0])
        attend(*blocks[i], *bufs[i % 2])


def attention(qkvu3, q_gain, k_gain, tabs, *, pad):
    b, lp, _ = qkvu3.shape
    tq = 2 * CHUNK
    gw = ATT_GROUP * HEAD_DIM
    k_col0 = ATT_HEADS
    v_col0 = ATT_HEADS + ATT_KV_HEADS
    full = lambda bi, hi: (0, 0)
    st = pltpu.VMEM((lp - pad, ATT_GROUP * tq), F32)
    pb = pltpu.VMEM((lp, ATT_GROUP * tq), BF16)
    return pl.pallas_call(
        functools.partial(_attn_kernel, lp=lp, pad=pad, tq=tq),
        grid=(b, ATT_KV_HEADS),
        in_specs=[pl.BlockSpec((1, lp, gw), lambda bi, hi: (bi, 0, hi)),
                  pl.BlockSpec((1, lp, HEAD_DIM), lambda bi, hi: (bi, 0, k_col0 + hi)),
                  pl.BlockSpec((1, lp, HEAD_DIM), lambda bi, hi: (bi, 0, v_col0 + hi)),
                  pl.BlockSpec((1, HEAD_DIM), full),
                  pl.BlockSpec((1, HEAD_DIM), full),
                  pl.BlockSpec((lp, HEAD_DIM), full),
                  pl.BlockSpec((lp, HEAD_DIM), full),
                  pl.BlockSpec((lp, HEAD_DIM), full)],
        out_specs=pl.BlockSpec((1, lp, gw), lambda bi, hi: (bi, 0, hi)),
        out_shape=jax.ShapeDtypeStruct((b, lp, ATT_HEADS * HEAD_DIM), BF16),
        scratch_shapes=[pltpu.VMEM((lp, HEAD_DIM), BF16), pltpu.VMEM((HEAD_DIM, lp), BF16), st, st, pb, pb],
        compiler_params=_params("parallel", "parallel"),
        name="attention",
    )(qkvu3, qkvu3, qkvu3, q_gain.reshape(1, HEAD_DIM).astype(F32), k_gain.reshape(1, HEAD_DIM).astype(F32),
      *tabs)


def _s5_kernel(u_ref, k_ref, w_ref, v_ref, at_ref, y_ref, s_ref, x_ref, *, nchunk, nb):
    u = u_ref[0]
    s_ref[...] = _dot(u, w_ref[0])
    at = at_ref[0]
    afr, afi, abr, abi = (at[:, i * CHUNK:(i + 1) * CHUNK] for i in range(4))

    def body(c, carry):
        xfr, xfi, xbr, xbi = carry
        rf = pl.multiple_of(c * nb, nb)
        rb = pl.multiple_of((nchunk - 1 - c) * nb, nb)
        x_ref[pl.ds(rf, nb), 0:CHUNK] = xfr
        x_ref[pl.ds(rf, nb), CHUNK:2 * CHUNK] = xfi
        x_ref[pl.ds(rb, nb), 2 * CHUNK:3 * CHUNK] = xbr
        x_ref[pl.ds(rb, nb), 3 * CHUNK:4 * CHUNK] = xbi
        sfr = s_ref[pl.ds(rf, nb), 0:CHUNK]
        sfi = s_ref[pl.ds(rf, nb), CHUNK:2 * CHUNK]
        sbr = s_ref[pl.ds(rb, nb), 2 * CHUNK:3 * CHUNK]
        sbi = s_ref[pl.ds(rb, nb), 3 * CHUNK:4 * CHUNK]
        return (afr * xfr - afi * xfi + sfr, afr * xfi + afi * xfr + sfi,
                abr * xbr - abi * xbi + sbr, abr * xbi + abi * xbr + sbi)

    z = jnp.zeros((nb, CHUNK), F32)
    lax.fori_loop(0, nchunk, body, (z, z, z, z))
    x = _dot(u, k_ref[0]) + _dot(x_ref[...].astype(BF16), v_ref[0])
    y = x * (0.5 * (1.0 + jnp.tanh(math.sqrt(2.0 / math.pi) * (x + 0.044715 * (x * x * x)))))
    y_ref[0] = y.astype(y_ref.dtype)


def _s5_matrices(lam_re, lam_im, log_dt, b_re, b_im, c_re, c_im, d_skip):
    hi = lax.Precision.HIGHEST
    t = S5_T
    lr = jnp.minimum(lam_re, -1e-4)
    li = lam_im
    dt = jnp.exp(log_dt)[..., None]
    er = jnp.exp(lr * dt)
    abar_re = er * jnp.cos(li * dt)
    abar_im = er * jnp.sin(li * dt)
    nr = abar_re - 1.0
    den = lr * lr + li * li
    coef_re = (nr * lr + abar_im * li) / den
    coef_im = (abar_im * lr - nr * li) / den
    bb_re = coef_re[..., None] * b_re - coef_im[..., None] * b_im
    bb_im = coef_re[..., None] * b_im + coef_im[..., None] * b_re
    kk = jnp.arange(t + 1, dtype=F32)[:, None, None, None]
    mag = jnp.exp(kk * (lr * dt)[None])
    pw_re = mag * jnp.cos(kk * (li * dt)[None])
    pw_im = mag * jnp.sin(kk * (li * dt)[None])
    g, hh = d_skip.shape
    ct_re = c_re.transpose(0, 1, 3, 2)
    ct_im = c_im.transpose(0, 1, 3, 2)
    flat = lambda a: a.reshape(2, g, S5_STATE, hh * hh)
    bc_re = flat(bb_re[..., :, None] * ct_re[..., None, :] - bb_im[..., :, None] * ct_im[..., None, :])
    bc_im = flat(bb_re[..., :, None] * ct_im[..., None, :] + bb_im[..., :, None] * ct_re[..., None, :])
    lagk = (jnp.einsum('kdgp,dgpn->dgkn', pw_re, bc_re, precision=hi)
            - jnp.einsum('kdgp,dgpn->dgkn', pw_im, bc_im, precision=hi))
    ti = jnp.arange(t)
    lag = ti[None, :] - ti[:, None]
    sel = lambda m: m[None, :, :, None]
    skip = (jnp.eye(hh, dtype=F32)[None] * d_skip[:, None, :]).reshape(g, 1, 1, hh * hh)
    ktot = (jnp.where(sel(lag >= 0), lagk[0][:, jnp.clip(lag, 0, t)], 0.0)
            + jnp.where(sel(lag <= 0), lagk[1][:, jnp.clip(-lag, 0, t)], 0.0)
            + jnp.where(sel(lag == 0), skip, 0.0))
    ktot = ktot.reshape(g, t, t, hh, hh).transpose(0, 1, 3, 2, 4).reshape(g, t * hh, t * hh)

    def bsum(pw_r, pw_i, d):
        wr = pw_r[:, :, :, None] * bb_re[d][None] - pw_i[:, :, :, None] * bb_im[d][None]
        wi = pw_r[:, :, :, None] * bb_im[d][None] + pw_i[:, :, :, None] * bb_re[d][None]
        tr = lambda a: a.transpose(1, 0, 3, 2).reshape(g, t * hh, S5_STATE)
        return tr(wr), tr(wi)

    wf_re, wf_im = bsum(pw_re[:t, 0][::-1], pw_im[:t, 0][::-1], 0)
    wb_re, wb_im = bsum(pw_re[:t, 1], pw_im[:t, 1], 1)
    padl = lambda a: jnp.pad(a, ((0, 0), (0, 0), (0, CHUNK - S5_STATE)))
    wtot = jnp.concatenate([padl(wf_re), padl(wf_im), padl(wb_re), padl(wb_im)], axis=-1)

    def vmat(pw_r, pw_i, d):
        vr = pw_r[:, :, None, :] * c_re[d][None] - pw_i[:, :, None, :] * c_im[d][None]
        vi = pw_r[:, :, None, :] * c_im[d][None] + pw_i[:, :, None, :] * c_re[d][None]
        tr = lambda a: a.transpose(1, 3, 0, 2).reshape(g, S5_STATE, t * hh)
        return tr(vr), tr(-vi)

    vf_re, vf_im = vmat(pw_re[1:, 0], pw_im[1:, 0], 0)
    vb_re, vb_im = vmat(pw_re[1:, 1][::-1], pw_im[1:, 1][::-1], 1)
    padr = lambda a: jnp.pad(a, ((0, 0), (0, CHUNK - S5_STATE), (0, 0)))
    vtot = jnp.concatenate([padr(vf_re), padr(vf_im), padr(vb_re), padr(vb_im)], axis=1)
    padv = lambda a: jnp.pad(a, ((0, 0), (0, CHUNK - S5_STATE)))
    at = jnp.concatenate([padv(pw_re[t, 0]), padv(pw_im[t, 0]), padv(pw_re[t, 1]), padv(pw_im[t, 1])], axis=-1)
    return ktot.astype(BF16), wtot.astype(BF16), vtot.astype(BF16), at[:, None, :]


def s5_scan(u, mats):
    b, lp, _ = u.shape
    nchunk = lp // S5_T
    ktot, wtot, vtot, at = mats
    ug = u.astype(BF16).reshape(b, nchunk, S5_T, S5_GROUPS, S5_GROUP).transpose(3, 1, 0, 2, 4)
    ug = ug.reshape(S5_GROUPS, nchunk * b, S5_COLS)
    rows = nchunk * b
    per_g = lambda g: (g, 0, 0)
    y = pl.pallas_call(
        functools.partial(_s5_kernel, nchunk=nchunk, nb=b),
        grid=(S5_GROUPS,),
        in_specs=[pl.BlockSpec((1, rows, S5_COLS), per_g),
                  pl.BlockSpec((1, S5_COLS, S5_COLS), per_g),
                  pl.BlockSpec((1, S5_COLS, 4 * CHUNK), per_g),
                  pl.BlockSpec((1, 4 * CHUNK, S5_COLS), per_g),
                  pl.BlockSpec((1, 1, 4 * CHUNK), per_g)],
        out_specs=pl.BlockSpec((1, rows, S5_COLS), per_g),
        out_shape=jax.ShapeDtypeStruct((S5_GROUPS, rows, S5_COLS), BF16),
        scratch_shapes=[pltpu.VMEM((rows, 4 * CHUNK), F32), pltpu.VMEM((rows, 4 * CHUNK), F32)],
        compiler_params=_params("parallel"),
        name="s5_scan",
    )(ug, ktot, wtot, vtot, at)
    y = y.reshape(S5_GROUPS, nchunk, b, S5_T, S5_GROUP).transpose(2, 1, 3, 0, 4)
    return y.reshape(b * lp, S5_GROUPS * S5_GROUP)


def _s5_glu_kernel(y_ref, w_ref, b_ref, o_ref):
    y = y_ref[...]
    z = _dot(y, w_ref[...]) + b_ref[...]
    o_ref[...] = (y.astype(F32) * _sigmoid(z)).astype(o_ref.dtype)


def s5_glu(y, w, bias):
    m, n = y.shape
    tm = _pick(m, (1024, 512, 384, 256, 128))
    return pl.pallas_call(
        _s5_glu_kernel,
        grid=(m // tm,),
        in_specs=[pl.BlockSpec((tm, n), lambda i: (i, 0)),
                  pl.BlockSpec((n, n), lambda i: (0, 0)),
                  pl.BlockSpec((1, n), lambda i: (0, 0))],
        out_specs=pl.BlockSpec((tm, n), lambda i: (i, 0)),
        out_shape=jax.ShapeDtypeStruct((m, n), BF16),
        compiler_params=_params("parallel"),
        name="s5_glu",
    )(y, w, bias.reshape(1, n).astype(F32))


def _head_norm(x, gain):
    xc = x - jnp.mean(x, axis=-1, keepdims=True)
    return xc * lax.rsqrt(jnp.mean(xc * xc, axis=-1, keepdims=True) + NORM_EPS) * gain


def _ret_kernel(q_ref, k_ref, v_ref, g_ref, c_ref, s_ref, lg_ref, gn_ref, o_ref,
                qs_ref, ks_ref, vs_ref, vt_ref, af_ref, ab_ref, st_ref, *, nchunk):
    c = c_ref[...]
    s = s_ref[...]
    ii = lax.broadcasted_iota(jnp.int32, (CHUNK, CHUNK), 0).astype(F32)
    jj = lax.broadcasted_iota(jnp.int32, (CHUNK, CHUNK), 1).astype(F32)
    lane = (lax.broadcasted_iota(jnp.int32, (1, nchunk * CHUNK), 1) & (CHUNK - 1)).astype(F32)
    diff = ii - jj
    lanes = [slice(hh * HEAD_DIM, (hh + 1) * HEAD_DIM) for hh in range(HEADS_PER_STEP)]
    consts = []
    for hh, ln in enumerate(lanes):
        q = q_ref[0, :, ln].astype(F32)
        qs_ref[:, ln] = ((q * c + pltpu.roll(q, HEAD_DIM // 2, 1) * s) * HEAD_DIM ** -0.5).astype(BF16)
        k = k_ref[0, :, ln].astype(F32)
        ks_ref[:, ln] = (k * c + pltpu.roll(k, HEAD_DIM // 2, 1) * s).astype(BF16)
        v = v_ref[0, :, ln].astype(F32)
        vs_ref[:, ln] = v.astype(BF16)
        v_t = v.T
        lgf = lg_ref[hh, 0:1, :]
        lgb = lg_ref[hh, 1:2, :]
        vt_ref[2 * hh] = (v_t * jnp.exp((CHUNK - 1 - lane) * lgf[:, 0:1])).astype(BF16)
        vt_ref[2 * hh + 1] = (v_t * jnp.exp(lane * lgb[:, 0:1])).astype(BF16)
        fwd = (jnp.where(diff >= 0, jnp.exp(jnp.where(diff >= 0, diff, 0.0) * lgf), 0.0),
               jnp.exp((ii + 1.0) * lgf), jnp.exp(CHUNK * lgf))
        bwd = (jnp.where(diff < 0, jnp.exp(jnp.where(diff < 0, -diff, 0.0) * lgb), 0.0),
               jnp.exp((CHUNK - ii) * lgb), jnp.exp(CHUNK * lgb))
        consts.append((fwd, bwd))
    st_ref[...] = jnp.zeros_like(st_ref)

    def chunk(r, ln, slot, cst, out_ref):
        dec, xi, gc = cst
        qc = qs_ref[pl.ds(r, CHUNK), ln]
        kc = ks_ref[pl.ds(r, CHUNK), ln]
        vc = vs_ref[pl.ds(r, CHUNK), ln]
        state_t = st_ref[slot]
        both = _dot_nt(qc, jnp.concatenate([kc, state_t.astype(BF16)], axis=0))
        sc = both[:, :CHUNK] * dec
        res = _dot(jnp.concatenate([sc.astype(BF16), vt_ref[slot, :, pl.ds(r, CHUNK)]], axis=0),
                   jnp.concatenate([vc, kc], axis=1))
        out_ref[pl.ds(r, CHUNK), ln] = res[:CHUNK, :HEAD_DIM] + both[:, CHUNK:] * xi
        st_ref[slot] = gc * state_t + res[CHUNK:, HEAD_DIM:]

    def body(t, carry):
        rf = pl.multiple_of(t * CHUNK, CHUNK)
        rb = pl.multiple_of((nchunk - 1 - t) * CHUNK, CHUNK)
        for hh, ln in enumerate(lanes):
            chunk(rf, ln, 2 * hh, consts[hh][0], af_ref)
            chunk(rb, ln, 2 * hh + 1, consts[hh][1], ab_ref)
        return carry

    lax.fori_loop(0, nchunk, body, 0)
    for ln in lanes:
        gate = g_ref[0, :, ln].astype(F32)
        y = _head_norm(af_ref[:, ln] + ab_ref[:, ln], gn_ref[:, ln])
        o_ref[0, :, ln] = (y * (gate * _sigmoid(gate))).astype(o_ref.dtype)


def retention(proj3, log_gamma, gain, cos_t, sin_t):
    b, lp, _ = proj3.shape
    hd = HEAD_DIM
    hps = HEADS_PER_STEP
    wd = hps * hd
    nblk = RET_HEADS // hps
    lg = jnp.broadcast_to(log_gamma.T[:, :, None], (RET_HEADS, 2, hd)).astype(F32)
    blk = lambda off: pl.BlockSpec((1, lp, wd), lambda bi, hi: (bi, 0, off + hi))
    full = lambda bi, hi: (0, 0)
    return pl.pallas_call(
        functools.partial(_ret_kernel, nchunk=lp // CHUNK),
        grid=(b, nblk),
        in_specs=[blk(0), blk(nblk), blk(2 * nblk), blk(3 * nblk),
                  pl.BlockSpec((lp, hd), full), pl.BlockSpec((lp, hd), full),
                  pl.BlockSpec((hps, 2, hd), lambda bi, hi: (hi, 0, 0)),
                  pl.BlockSpec((1, wd), lambda bi, hi: (0, hi))],
        out_specs=pl.BlockSpec((1, lp, wd), lambda bi, hi: (bi, 0, hi)),
        out_shape=jax.ShapeDtypeStruct((b, lp, RET_HEADS * hd), BF16),
        scratch_shapes=[pltpu.VMEM((lp, wd), BF16), pltpu.VMEM((lp, wd), BF16), pltpu.VMEM((lp, wd), BF16),
                        pltpu.VMEM((2 * hps, hd, lp), BF16), pltpu.VMEM((lp, wd), F32), pltpu.VMEM((lp, wd), F32),
                        pltpu.VMEM((2 * hps, hd, hd), F32)],
        compiler_params=_params("parallel", "parallel"),
        name="retention",
    )(proj3, proj3, proj3, proj3, cos_t, sin_t, lg, gain.reshape(1, RET_HEADS * hd).astype(F32))


def _mlstm_gate_tables(li, lf, tri_sum):
    bt = jnp.dot(lf, tri_sum, preferred_element_type=F32, precision=lax.Precision.HIGHEST)
    bt_last = jnp.sum(lf, axis=1, keepdims=True)
    a = bt_last - bt + li
    m_loc = jnp.max(a, axis=1, keepdims=True)
    return bt, jnp.exp(a - m_loc), m_loc, bt_last, li - bt


def _mlstm_chunk_t(kc, qtc, vtc, vwc, bt_row, w_row, m_loc, bt_last, colb, mask_t, ct_prev, n_prev, m_prev):
    d, c, pk = HEAD_DIM, CHUNK, BF16_ROWS
    dlog_t = jnp.where(mask_t, bt_row + colb, -jnp.inf)
    g_row = bt_row + m_prev
    m_t = jnp.maximum(g_row, jnp.max(dlog_t, axis=0, keepdims=True))
    r1 = _dot(jnp.concatenate([kc, ct_prev.astype(BF16), jnp.broadcast_to(n_prev, (pk, d)).astype(BF16)], axis=0),
              qtc)
    s_t = r1[:c] * jnp.exp(dlog_t - m_t)
    w_int = jnp.exp(g_row - m_t)
    r2 = _dot(jnp.concatenate([vtc, vwc, jnp.broadcast_to(w_row, (pk, c)).astype(BF16)], axis=0),
              jnp.concatenate([s_t.astype(BF16), kc], axis=1))
    num_t = r2[:d, :c] + w_int * r1[c:c + d]
    den_t = jnp.sum(s_t, axis=0, keepdims=True) + w_int * r1[c + d:c + d + 1]
    out_t = num_t / jnp.maximum(jnp.abs(den_t), jnp.exp(-m_t))
    m_new = jnp.maximum(bt_last + m_prev, m_loc)
    f_prev = jnp.exp(bt_last + m_prev - m_new)
    f_loc = jnp.exp(m_loc - m_new)
    c_new = f_prev * ct_prev + f_loc * r2[d:2 * d, c:]
    n_new = f_prev * n_prev + f_loc * r2[2 * d:2 * d + 1, c:]
    return out_t, c_new, n_new, m_new


def _mlstm_kernel(mu_ref, mo_ref, gt_ref, gb_ref, cw_ref, cb_ref, wk_ref, wqt_ref, wvt_ref, gn_ref, o_ref,
                  ks_ref, qt_ref, vt_ref, vw_ref, row_ref, d_ref, colb_ref, aft_ref, abt_ref,
                  cs_ref, ns_ref, ms_ref, *, lp, pad, nchunk):
    hd = HEAD_DIM
    lanes = [slice(hh * hd, (hh + 1) * hd) for hh in range(HEADS_PER_STEP)]
    valid_row = lax.broadcasted_iota(jnp.int32, (lp, 1), 0) >= pad
    valid_col = lax.broadcasted_iota(jnp.int32, (1, lp), 1) >= pad
    pos = (lax.broadcasted_iota(jnp.int32, (nchunk, CHUNK), 0) * CHUNK
           + lax.broadcasted_iota(jnp.int32, (nchunk, CHUNK), 1))
    valid_pos = pos >= pad
    ii = lax.broadcasted_iota(jnp.int32, (CHUNK, CHUNK), 0)
    jj = lax.broadcasted_iota(jnp.int32, (CHUNK, CHUNK), 1)
    eye = ii == jj
    upper = ii <= jj
    lower = ii >= jj
    ones = jnp.ones((CHUNK, CHUNK), BF16)
    for hh, ln in enumerate(lanes):
        mu = mu_ref[0, :, ln].astype(F32)
        conv = cb_ref[:, ln]
        for j in range(CONV_W):
            conv = conv + cw_ref[j:j + 1, ln] * pltpu.roll(mu, (CONV_W // 2 - j) % lp, 0)
        uc = (conv * _sigmoid(conv)).astype(BF16)
        ks_ref[:, ln] = jnp.where(valid_row, _dot(uc, wk_ref[hh]) * hd ** -0.5, 0.0).astype(BF16)
        qt_ref[ln, :] = jnp.where(valid_col, _dot_nt(wqt_ref[hh], uc), 0.0).astype(BF16)
        v_t = jnp.where(valid_col, _dot_nt(wvt_ref[hh], mu.astype(BF16)), 0.0)
        vt_ref[ln, :] = v_t.astype(BF16)

        for d, tri_sum in enumerate((upper, lower)):
            g_i = gt_ref[0, hh, 2 * d] + gb_ref[hh, 2 * d:2 * d + 1, 0:1]
            g_f = gt_ref[0, hh, 2 * d + 1] + gb_ref[hh, 2 * d + 1:2 * d + 2, 0:1]
            li = jnp.where(valid_pos, g_i, NEG_GATE)
            lf = jnp.where(valid_pos, jnp.minimum(g_f, 0.0) - jnp.log(1.0 + jnp.exp(-jnp.abs(g_f))), 0.0)
            bt, w, m_loc, bt_last, colv = _mlstm_gate_tables(li, lf, jnp.where(tri_sum, 1.0, 0.0))
            slot = 2 * hh + d
            row_ref[4 * slot + 0] = bt
            row_ref[4 * slot + 1] = w
            row_ref[4 * slot + 2] = jnp.broadcast_to(m_loc, (nchunk, CHUNK))
            row_ref[4 * slot + 3] = jnp.broadcast_to(bt_last, (nchunk, CHUNK))
            for n in range(nchunk):
                cols = slice(n * CHUNK, (n + 1) * CHUNK)
                vw_ref[slot, :, cols] = (v_t[:, cols] * w[n:n + 1, :]).astype(BF16)
                d_ref[cols, :] = jnp.where(eye, colv[n:n + 1, :], 0.0)
            diag = d_ref[...]
            d_hi = diag.astype(BF16)
            d_lo = (diag - d_hi.astype(F32)).astype(BF16)
            colb_ref[slot] = _dot(d_hi, ones) + _dot(d_lo, ones)
    cs_ref[...] = jnp.zeros_like(cs_ref)
    ns_ref[...] = jnp.zeros_like(ns_ref)
    ms_ref[...] = jnp.zeros_like(ms_ref)

    def run(n, r, ln, slot, mask_t, out_ref):
        row = lambda kind: row_ref[4 * slot + kind, pl.ds(n, 1), :]
        out_t, c_s, n_s, m_s = _mlstm_chunk_t(
            ks_ref[pl.ds(r, CHUNK), ln], qt_ref[ln, pl.ds(r, CHUNK)], vt_ref[ln, pl.ds(r, CHUNK)],
            vw_ref[slot, :, pl.ds(r, CHUNK)],
            row(0), row(1), row(2)[:, 0:1], row(3)[:, 0:1], colb_ref[slot, pl.ds(r, CHUNK), :], mask_t,
            cs_ref[slot], ns_ref[slot], ms_ref[slot][:, 0:1])
        out_ref[ln, pl.ds(r, CHUNK)] = out_t
        cs_ref[slot] = c_s
        ns_ref[slot] = n_s
        ms_ref[slot] = jnp.broadcast_to(m_s, (1, hd))

    def body(t, carry):
        tb = nchunk - 1 - t
        rf = pl.multiple_of(t * CHUNK, CHUNK)
        rb = pl.multiple_of(tb * CHUNK, CHUNK)
        for hh, ln in enumerate(lanes):
            run(t, rf, ln, 2 * hh, upper, aft_ref)
            run(tb, rb, ln, 2 * hh + 1, lower, abt_ref)
        return carry

    lax.fori_loop(0, nchunk, body, 0)
    for ln in lanes:
        x_t = aft_ref[ln, :] + abt_ref[ln, :]
        xc = x_t - jnp.mean(x_t, axis=0, keepdims=True)
        y = (xc * lax.rsqrt(jnp.mean(xc * xc, axis=0, keepdims=True) + NORM_EPS)).T
        o_ref[0, :, ln] = (y * gn_ref[:, ln] * _sigmoid(mo_ref[0, :, ln].astype(F32))).astype(o_ref.dtype)


def mlstm(proj3, gates, gate_b, conv_w, conv_b, wq, wk, wv, gain, *, pad, mu_col0, mo_col0):
    b, lp, _ = proj3.shape
    hd = HEAD_DIM
    hps = HEADS_PER_STEP
    wd = hps * hd
    nchunk = lp // CHUNK
    gt = gates.reshape(b, lp, 4, ML_HEADS).transpose(0, 3, 2, 1).reshape(b, ML_HEADS, 4, nchunk, CHUNK)
    gb = jnp.broadcast_to(gate_b.T[:, :, None], (ML_HEADS, 4, hd)).astype(F32)
    blk = lambda off: pl.BlockSpec((1, lp, wd), lambda bi, hi: (bi, 0, off // hps + hi))
    per_h = lambda bi, hi: (hi, 0, 0)
    vec = pl.BlockSpec((1, wd), lambda bi, hi: (0, hi))
    sq = pl.BlockSpec((hps, hd, hd), per_h)
    tr = lambda w: jnp.swapaxes(w, 1, 2).astype(BF16)
    return pl.pallas_call(
        functools.partial(_mlstm_kernel, lp=lp, pad=pad, nchunk=nchunk),
        grid=(b, ML_HEADS // hps),
        in_specs=[blk(mu_col0), blk(mo_col0),
                  pl.BlockSpec((1, hps, 4, nchunk, CHUNK), lambda bi, hi: (bi, hi, 0, 0, 0)),
                  pl.BlockSpec((hps, 4, hd), per_h),
                  pl.BlockSpec((CONV_W, wd), lambda bi, hi: (0, hi)),
                  vec, sq, sq, sq, vec],
        out_specs=pl.BlockSpec((1, lp, wd), lambda bi, hi: (bi, 0, hi)),
        out_shape=jax.ShapeDtypeStruct((b, lp, ML_HEADS * hd), BF16),
        scratch_shapes=[pltpu.VMEM((lp, wd), BF16), pltpu.VMEM((wd, lp), BF16), pltpu.VMEM((wd, lp), BF16),
                        pltpu.VMEM((2 * hps, hd, lp), BF16),
                        pltpu.VMEM((8 * hps, nchunk, CHUNK), F32), pltpu.VMEM((lp, CHUNK), F32),
                        pltpu.VMEM((2 * hps, lp, CHUNK), F32),
                        pltpu.VMEM((wd, lp), F32), pltpu.VMEM((wd, lp), F32),
                        pltpu.VMEM((2 * hps, hd, hd), F32), pltpu.VMEM((2 * hps, 1, hd), F32),
                        pltpu.VMEM((2 * hps, 1, hd), F32)],
        compiler_params=_params("parallel", "parallel"),
        name="mlstm",
    )(proj3, proj3, gt, gb, conv_w.astype(F32), conv_b.reshape(1, -1).astype(F32),
      wk.astype(BF16), tr(wq), tr(wv), gain.reshape(1, -1).astype(F32))


def _rope_freqs(dim):
    return ROPE_THETA ** (-jnp.arange(dim // 2, dtype=F32) / (dim // 2))


def _axial_tables(n_tok, pad):
    rows = n_tok // GRID_W
    row = jnp.concatenate([jnp.zeros((pad,), F32), -jnp.ones((N_META,), F32),
                           jnp.repeat(jnp.arange(rows, dtype=F32), GRID_W)])
    col = jnp.concatenate([jnp.zeros((pad,), F32), jnp.arange(N_META, dtype=F32),
                           jnp.tile(jnp.arange(GRID_W, dtype=F32), rows)])
    f = _rope_freqs(HEAD_DIM // 2)
    ang = jnp.concatenate([row[:, None] * f[None, :]] * 2 + [col[:, None] * f[None, :]] * 2, axis=-1)
    first = (jnp.arange(HEAD_DIM) % 64) < 32
    sin = jnp.sin(ang)
    return jnp.cos(ang), jnp.where(first, -sin, 0.0), jnp.where(first, 0.0, sin)


def _linear_tables(l, pad):
    pos = jnp.concatenate([jnp.zeros((pad,), F32), jnp.arange(l, dtype=F32)])
    ang = pos[:, None] * _rope_freqs(HEAD_DIM)[None, :]
    ang = jnp.concatenate([ang, ang], axis=-1)
    sin = jnp.sin(ang)
    return jnp.cos(ang), jnp.where(jnp.arange(HEAD_DIM) < HEAD_DIM // 2, -sin, sin)


def _even_mixer_parts(hn, w_in_all, j, q_norm, k_norm, s5_params, glu_w, glu_b, tabs, *, b, lp, pad):
    att_w = ATT_HEADS * HEAD_DIM
    u0 = att_w + 2 * ATT_KV_HEADS * HEAD_DIM
    qkvu = matmul_wcast(hn, w_in_all, j, w_in_all.shape[2])
    qkvu3 = qkvu.reshape(b, lp, -1)
    att = attention(qkvu3, q_norm, k_norm, tabs, pad=pad).reshape(b * lp, att_w)
    y = s5_scan(qkvu3[:, :, u0:], _s5_matrices(*s5_params))
    ssm = s5_glu(y, glu_w.astype(BF16), glu_b)
    return [att, ssm]


def _odd_mixer_parts(hn, w_in_all, j, ret_log_decay, ret_norm, conv_w, conv_b, wq, wk, wv, gate_b, ml_norm,
                     tabs, *, b, lp, pad):
    ret_w = RET_HEADS * HEAD_DIM
    ml_w = ML_HEADS * HEAD_DIM
    main = 4 * ret_w + 2 * ml_w
    n_gate = w_in_all.shape[2] - main
    proj = matmul_wcast(hn, w_in_all, j, main, out_dtype=BF16)
    proj3 = proj.reshape(b, lp, main)
    w_gate = jnp.pad(w_in_all[j, :, main:].astype(BF16), ((0, 0), (0, CHUNK - n_gate)))
    gates = matmul(hn, w_gate)[:, :n_gate].reshape(b, lp, n_gate)
    log_gamma = -jnp.abs(ret_log_decay.astype(F32))
    ret = retention(proj3, log_gamma, ret_norm, *tabs)
    nblk = ret_w // HEAD_DIM
    hm = mlstm(proj3, gates, gate_b, conv_w, conv_b, wq, wk, wv, ml_norm,
               pad=pad, mu_col0=4 * nblk, mo_col0=4 * nblk + ml_w // HEAD_DIM)
    return [ret.reshape(b * lp, ret_w), hm.reshape(b * lp, ml_w)]


def kernel(x, meta_tokens, norm_gains, mlp_w1, mlp_w2, even_w_in, even_w_out, att_q_norm, att_k_norm, s5_lam_re, s5_lam_im, s5_log_dt, s5_b_re, s5_b_im, s5_c_re, s5_c_im, s5_d, s5_glu_w, s5_glu_b, odd_w_in, odd_w_out, ret_log_decay, ret_norm, ml_conv_w, ml_conv_b, ml_wq, ml_wk, ml_wv, ml_gate_b, ml_norm):
    b, n_tok, d_model = x.shape
    l = n_tok + N_META
    pad = (-l) % CHUNK
    lp = l + pad
    depth = norm_gains.shape[0]
    h = jnp.concatenate([jnp.zeros((b, pad, d_model), x.dtype),
                         jnp.broadcast_to(meta_tokens.astype(x.dtype)[None], (b, N_META, d_model)), x], axis=1)
    h = h.reshape(b * lp, d_model)
    axial = _axial_tables(n_tok, pad)
    linear = _linear_tables(l, pad)
    dims = dict(b=b, lp=lp, pad=pad)
    hn = rmsnorm(h, norm_gains[0, 0])
    w2_bf16 = mlp_w2.astype(BF16)
    even_out_bf16 = even_w_out.astype(BF16)
    odd_out_bf16 = odd_w_out.astype(BF16)
    for i in range(depth):
        j = i // 2
        if i % 2 == 0:
            s5_params = (s5_lam_re[j], s5_lam_im[j], s5_log_dt[j], s5_b_re[j], s5_b_im[j], s5_c_re[j],
                         s5_c_im[j], s5_d[j])
            parts = _even_mixer_parts(hn, even_w_in, j, att_q_norm[j], att_k_norm[j],
                                      s5_params, s5_glu_w[j], s5_glu_b[j], axial, **dims)
            w_out = even_out_bf16
        else:
            parts = _odd_mixer_parts(hn, odd_w_in, j, ret_log_decay[j], ret_norm[j],
                                     ml_conv_w[j], ml_conv_b[j], ml_wq[j], ml_wk[j], ml_wv[j], ml_gate_b[j],
                                     ml_norm[j], linear, **dims)
            w_out = odd_out_bf16
        h, hn = matmul_norm_res(parts, w_out, j, norm_gains[i, 1], h, norm_gains[i, 2], lp=lp, pad=pad)
        hid = matmul_wcast(hn, mlp_w1, i, mlp_w1.shape[2], relu2=True, out_dtype=BF16)
        last = i + 1 == depth
        next_gain = None if last else norm_gains[i + 1, 0]
        drop = pad + N_META if last and n_tok % CHUNK == 0 else 0
        h, hn = matmul_norm_res([hid], w2_bf16, i, norm_gains[i, 3], h, next_gain, lp=lp, pad=pad, drop_head=drop)
    if drop:
        return h.reshape(b, n_tok, d_model)
    return h.reshape(b, lp, d_model)[:, pad + N_META:]
```

```python
import functools
import math

import jax
import jax.numpy as jnp
from jax import lax
from jax.experimental import pallas as pl
from jax.experimental.pallas import tpu as pltpu

F32 = jnp.float32
BF16 = jnp.bfloat16

N_META = 16
GRID_W = 64
CHUNK = 128
HEAD_DIM = 128
NORM_EPS = 1e-6
ROPE_THETA = 10000.0
ATT_HEADS = 12
ATT_KV_HEADS = 4
ATT_GROUP = ATT_HEADS // ATT_KV_HEADS
S5_GROUP = 16
S5_GROUPS = 32
S5_STATE = 64
S5_T = 16
S5_COLS = S5_T * S5_GROUP
RET_HEADS = 8
ML_HEADS = 8
CONV_W = 5
NEG_GATE = -1e4
HEADS_PER_STEP = 4
BF16_ROWS = 16
VMEM_LIMIT_BYTES = 56 * 1024 * 1024


def _pick(n, cands):
    for c in cands:
        if n % c == 0:
            return c
    raise ValueError(f"no tile for {n} in {cands}")


def _params(*sem):
    return pltpu.CompilerParams(dimension_semantics=sem, vmem_limit_bytes=VMEM_LIMIT_BYTES)


def _dot(a, b):
    return jnp.dot(a, b, preferred_element_type=F32)


def _dot_nt(a, b):
    return lax.dot_general(a, b, (((1,), (1,)), ((), ())), preferred_element_type=F32)


def _dot_tn(a, b):
    return lax.dot_general(a, b, (((0,), (0,)), ((), ())), preferred_element_type=F32)


def _sigmoid(x):
    return 1.0 / (1.0 + jnp.exp(-x))


def _rmsnorm_kernel(x_ref, g_ref, o_ref):
    x = x_ref[...]
    ms = jnp.mean(x * x, axis=-1, keepdims=True)
    o_ref[...] = (x * lax.rsqrt(ms + NORM_EPS) * g_ref[...]).astype(o_ref.dtype)


def rmsnorm(x, gain):
    m, d = x.shape
    tm = _pick(m, (1024, 512, 384, 256, 128))
    return pl.pallas_call(
        _rmsnorm_kernel,
        grid=(m // tm,),
        in_specs=[pl.BlockSpec((tm, d), lambda i: (i, 0)), pl.BlockSpec((1, d), lambda i: (0, 0))],
        out_specs=pl.BlockSpec((tm, d), lambda i: (i, 0)),
        out_shape=jax.ShapeDtypeStruct((m, d), BF16),
        compiler_params=_params("parallel"),
        name="rmsnorm",
    )(x, gain.reshape(1, d).astype(F32))


def _matmul_kernel(a_ref, w_ref, o_ref, *, relu2):
    y = _dot(a_ref[...], w_ref[...])
    if relu2:
        y = jnp.square(jnp.maximum(y, 0.0))
    o_ref[...] = y.astype(o_ref.dtype)


def matmul(a, w, *, relu2=False, out_dtype=F32):
    m, k = a.shape
    n = w.shape[1]
    tm = _pick(m, (1024, 512, 384, 256, 128))
    tn = _pick(n, (2048, 1536, 1280, 1024, 512, 256, 128))
    return pl.pallas_call(
        functools.partial(_matmul_kernel, relu2=relu2),
        grid=(m // tm, n // tn),
        in_specs=[pl.BlockSpec((tm, k), lambda i, j: (i, 0)),
                  pl.BlockSpec((k, tn), lambda i, j: (0, j))],
        out_specs=pl.BlockSpec((tm, tn), lambda i, j: (i, j)),
        out_shape=jax.ShapeDtypeStruct((m, n), out_dtype),
        compiler_params=_params("parallel", "parallel"),
        name="matmul",
    )(a, w)


def _matmul_wcast_kernel(a_ref, w_ref, o_ref, wb_ref, *, relu2):
    @pl.when(pl.program_id(1) == 0)
    def _():
        wb_ref[...] = w_ref[...].astype(BF16)

    y = _dot(a_ref[...], wb_ref[...])
    if relu2:
        y = jnp.square(jnp.maximum(y, 0.0))
    o_ref[...] = y.astype(o_ref.dtype)


def matmul_wcast(a, w3, layer, n_cols, *, relu2=False, out_dtype=F32):
    m, k = a.shape
    tm = _pick(m, (1024, 512, 384, 256, 128))
    tn = _pick(n_cols, (1024, 512, 256, 128))
    return pl.pallas_call(
        functools.partial(_matmul_wcast_kernel, relu2=relu2),
        grid=(n_cols // tn, m // tm),
        in_specs=[pl.BlockSpec((tm, k), lambda j, i: (i, 0)),
                  pl.BlockSpec((None, k, tn), lambda j, i: (layer, 0, j))],
        out_specs=pl.BlockSpec((tm, tn), lambda j, i: (i, j)),
        out_shape=jax.ShapeDtypeStruct((m, n_cols), out_dtype),
        scratch_shapes=[pltpu.VMEM((k, tn), BF16)],
        compiler_params=_params("parallel", "arbitrary"),
        name="matmul_wcast",
    )(a, w3)


def _matmul_norm_res_kernel(*refs, widths, nk, tm, lp, pad, emit_next, mask_pad):
    na = len(widths)
    a_refs = refs[:na]
    w_ref, g_ref, h_ref = refs[na:na + 3]
    rest = refs[na + 3:]
    if emit_next:
        g2_ref, o_ref, n_ref = rest
    else:
        o_ref, = rest
    halves = [slice(0, tm // 2), slice(tm // 2, tm)]

    def finish(y, rows):
        ms = jnp.mean(y * y, axis=-1, keepdims=True)
        out = h_ref[rows, :] + y * lax.rsqrt(ms + NORM_EPS) * g_ref[...]
        if mask_pad:
            start = pl.program_id(0) * tm
            row = start + rows.start + lax.broadcasted_iota(jnp.int32, (rows.stop - rows.start, 1), 0)
            rel0 = row - (start // lp) * lp
            rel1 = row - ((start + tm - 1) // lp) * lp
            out = jnp.where(((rel0 >= 0) & (rel0 < pad)) | ((rel1 >= 0) & (rel1 < pad)), 0.0, out)
        o_ref[rows, :] = out
        if emit_next:
            ms2 = jnp.mean(out * out, axis=-1, keepdims=True)
            n_ref[rows, :] = (out * lax.rsqrt(ms2 + NORM_EPS) * g2_ref[...]).astype(n_ref.dtype)

    if nk == 1:
        for rows in halves:
            part = None
            off = 0
            for a_ref, wd in zip(a_refs, widths):
                d = _dot(a_ref[rows, :], w_ref[off:off + wd, :])
                part = d if part is None else part + d
                off += wd
            finish(part, rows)
    else:
        kk = pl.program_id(1)

        @pl.when(kk == 0)
        def _():
            o_ref[...] = _dot(a_refs[0][...], w_ref[...])

        @pl.when(kk > 0)
        def _():
            o_ref[...] += _dot(a_refs[0][...], w_ref[...])

        @pl.when(kk == nk - 1)
        def _():
            for rows in halves:
                finish(o_ref[rows, :], rows)


def matmul_norm_res(parts, w3, layer, gain, h, next_gain, *, lp, pad, drop_head=0):
    m, n = h.shape
    widths = tuple(p.shape[1] for p in parts)
    k = sum(widths)
    emit_next = next_gain is not None
    wide = bool(drop_head) and len(parts) == 1 and k > 2048 and (lp - drop_head) % 1024 == 0
    if len(parts) > 1 or k <= 2048:
        tk, nk = k, 1
    else:
        tk = 1024 if wide else 2048
        nk = k // tk
        widths = (tk,)
    vec = pl.BlockSpec((1, n), lambda i, j: (0, 0))
    if drop_head:
        assert not emit_next and pad <= drop_head
        keep = lp - drop_head
        tm = 1024 if wide else _pick(keep, (512, 384, 256, 128))
        per_b = keep // tm
        n_row_blocks = (m // lp) * per_b
        row0 = lambda i: pl.multiple_of((i // per_b) * lp + drop_head + (i % per_b) * tm, CHUNK)
        in_specs = [pl.BlockSpec((pl.Element(tm), pl.Element(wd)), functools.partial(
            lambda i, j, wd: (row0(i), j * wd), wd=wd)) for wd in widths]
        h_spec = pl.BlockSpec((pl.Element(tm), pl.Element(n)), lambda i, j: (row0(i), 0))
    else:
        tm = _pick(m, (512, 384, 256, 128))
        n_row_blocks = m // tm
        in_specs = [pl.BlockSpec((tm, wd), lambda i, j: (i, j)) for wd in widths]
        h_spec = pl.BlockSpec((tm, n), lambda i, j: (i, 0))
    assert tm <= lp
    row_blk = pl.BlockSpec((tm, n), lambda i, j: (i, 0))
    in_specs += [pl.BlockSpec((None, tk, n), lambda i, j: (layer, j, 0)), vec, h_spec]
    args = [*parts, w3, gain.reshape(1, n).astype(F32), h]
    out_specs = [row_blk]
    out_shape = [jax.ShapeDtypeStruct((n_row_blocks * tm, n), F32)]
    if emit_next:
        in_specs.append(vec)
        args.append(next_gain.reshape(1, n).astype(F32))
        out_specs.append(row_blk)
        out_shape.append(jax.ShapeDtypeStruct((n_row_blocks * tm, n), BF16))
    res = pl.pallas_call(
        functools.partial(_matmul_norm_res_kernel, widths=widths, nk=nk, tm=tm, lp=lp, pad=pad,
                          emit_next=emit_next, mask_pad=not drop_head),
        grid=(n_row_blocks, nk),
        in_specs=in_specs,
        out_specs=out_specs,
        out_shape=out_shape,
        compiler_params=_params("parallel", "arbitrary"),
        name="matmul_norm_res",
    )(*args)
    return (res[0], res[1]) if emit_next else (res[0], None)


def _rope_axial(x, c, sa, sb):
    return x * c + pltpu.roll(x, HEAD_DIM - 32, 1) * sa + pltpu.roll(x, 32, 1) * sb


def _attn_kernel(q_ref, k_ref, v_ref, qg_ref, kg_ref, c_ref, sa_ref, sb_ref, o_ref,
                 ks_ref, vt_ref, sta_ref, stb_ref, pa_ref, pb_ref, *, lp, pad, tq):
    k = k_ref[0].astype(F32)
    k = k * lax.rsqrt(jnp.mean(k * k, axis=-1, keepdims=True) + NORM_EPS) * kg_ref[...]
    ks_ref[...] = _rope_axial(k, c_ref[...], sa_ref[...], sb_ref[...]).astype(BF16)
    vt_ref[...] = v_ref[0].astype(F32).T.astype(BF16)
    for p_ref in (pa_ref, pb_ref):
        p_ref[0:pad, :] = jnp.zeros((pad, p_ref.shape[1]), BF16)
    scale = HEAD_DIM ** -0.5 * math.log2(math.e)

    def scores(r0, rows, st_ref):
        c = c_ref[pl.ds(r0, rows), :]
        sa = sa_ref[pl.ds(r0, rows), :]
        sb = sb_ref[pl.ds(r0, rows), :]
        qs = []
        for g in range(ATT_GROUP):
            q = q_ref[0, pl.ds(r0, rows), g * HEAD_DIM:(g + 1) * HEAD_DIM].astype(F32)
            q = q * lax.rsqrt(jnp.mean(q * q, axis=-1, keepdims=True) + NORM_EPS) * qg_ref[...]
            qs.append((_rope_axial(q, c, sa, sb) * scale).astype(BF16))
        qall = jnp.concatenate(qs, axis=0)
        st_ref[:, 0:ATT_GROUP * rows] = _dot_nt(ks_ref[pad:, :], qall)

    def attend(r0, rows, st_ref, p_ref):
        n = ATT_GROUP * rows
        st = st_ref[:, 0:n]
        p = jnp.exp2(st - jnp.max(st, axis=0, keepdims=True))
        l = jnp.sum(p, axis=0, keepdims=True)
        p_ref[pad:, 0:n] = p.astype(BF16)
        o = (_dot(vt_ref[...], p_ref[:, 0:n]) / l).T
        for g in range(ATT_GROUP):
            o_ref[0, pl.ds(r0, rows), g * HEAD_DIM:(g + 1) * HEAD_DIM] = (
                o[g * rows:(g + 1) * rows].astype(o_ref.dtype))

    bufs = ((sta_ref, pa_ref), (stb_ref, pb_ref))
    nbig = lp // tq
    blocks = [(i * tq, tq) for i in range(nbig)]
    if lp % tq:
        blocks.append((nbig * tq, lp % tq))
    npairs = max(0, (nbig - 1) // 2)
    scores(0, blocks[0][1], sta_ref)

    def body(u, carry):
        r, r1, r2 = (pl.multiple_of((2 * u + i) * tq, tq) for i in range(3))
        scores(r1, tq, stb_ref)
        attend(r, tq, sta_ref, pa_ref)
        scores(r2, tq, sta_ref)
        attend(r1, tq, stb_ref, pb_ref)
        return carry

    lax.fori_loop(0, npairs, body, 0)
    for i in range(2 * npairs, len(blocks)):
        if i + 1 < len(blocks):
            scores(*blocks[i + 1], bufs[(i + 1) % 2][0])
        attend(*blocks[i], *bufs[i % 2])


def attention(qkvu3, q_gain, k_gain, tabs, *, pad):
    b, lp, _ = qkvu3.shape
    tq = 2 * CHUNK
    gw = ATT_GROUP * HEAD_DIM
    k_col0 = ATT_HEADS
    v_col0 = ATT_HEADS + ATT_KV_HEADS
    full = lambda bi, hi: (0, 0)
    st = pltpu.VMEM((lp - pad, ATT_GROUP * tq), F32)
    pb = pltpu.VMEM((lp, ATT_GROUP * tq), BF16)
    return pl.pallas_call(
        functools.partial(_attn_kernel, lp=lp, pad=pad, tq=tq),
        grid=(b, ATT_KV_HEADS),
        in_specs=[pl.BlockSpec((1, lp, gw), lambda bi, hi: (bi, 0, hi)),
                  pl.BlockSpec((1, lp, HEAD_DIM), lambda bi, hi: (bi, 0, k_col0 + hi)),
                  pl.BlockSpec((1, lp, HEAD_DIM), lambda bi, hi: (bi, 0, v_col0 + hi)),
                  pl.BlockSpec((1, HEAD_DIM), full),
                  pl.BlockSpec((1, HEAD_DIM), full),
                  pl.BlockSpec((lp, HEAD_DIM), full),
                  pl.BlockSpec((lp, HEAD_DIM), full),
                  pl.BlockSpec((lp, HEAD_DIM), full)],
        out_specs=pl.BlockSpec((1, lp, gw), lambda bi, hi: (bi, 0, hi)),
        out_shape=jax.ShapeDtypeStruct((b, lp, ATT_HEADS * HEAD_DIM), BF16),
        scratch_shapes=[pltpu.VMEM((lp, HEAD_DIM), BF16), pltpu.VMEM((HEAD_DIM, lp), BF16), st, st, pb, pb],
        compiler_params=_params("parallel", "parallel"),
        name="attention",
    )(qkvu3, qkvu3, qkvu3, q_gain.reshape(1, HEAD_DIM).astype(F32), k_gain.reshape(1, HEAD_DIM).astype(F32),
      *tabs)


def _s5_kernel(u_ref, k_ref, w_ref, v_ref, at_ref, y_ref, s_ref, x_ref, *, nchunk, nb):
    u = u_ref[0]
    s_ref[...] = _dot(u, w_ref[0])
    at = at_ref[0]
    afr, afi, abr, abi = (at[:, i * CHUNK:(i + 1) * CHUNK] for i in range(4))

    def body(c, carry):
        xfr, xfi, xbr, xbi = carry
        rf = pl.multiple_of(c * nb, nb)
        rb = pl.multiple_of((nchunk - 1 - c) * nb, nb)
        x_ref[pl.ds(rf, nb), 0:CHUNK] = xfr
        x_ref[pl.ds(rf, nb), CHUNK:2 * CHUNK] = xfi
        x_ref[pl.ds(rb, nb), 2 * CHUNK:3 * CHUNK] = xbr
        x_ref[pl.ds(rb, nb), 3 * CHUNK:4 * CHUNK] = xbi
        sfr = s_ref[pl.ds(rf, nb), 0:CHUNK]
        sfi = s_ref[pl.ds(rf, nb), CHUNK:2 * CHUNK]
        sbr = s_ref[pl.ds(rb, nb), 2 * CHUNK:3 * CHUNK]
        sbi = s_ref[pl.ds(rb, nb), 3 * CHUNK:4 * CHUNK]
        return (afr * xfr - afi * xfi + sfr, afr * xfi + afi * xfr + sfi,
                abr * xbr - abi * xbi + sbr, abr * xbi + abi * xbr + sbi)

    z = jnp.zeros((nb, CHUNK), F32)
    lax.fori_loop(0, nchunk, body, (z, z, z, z))
    x = _dot(u, k_ref[0]) + _dot(x_ref[...].astype(BF16), v_ref[0])
    y = x * (0.5 * (1.0 + jnp.tanh(math.sqrt(2.0 / math.pi) * (x + 0.044715 * (x * x * x)))))
    y_ref[0] = y.astype(y_ref.dtype)


def _s5_matrices(lam_re, lam_im, log_dt, b_re, b_im, c_re, c_im, d_skip):
    hi = lax.Precision.HIGHEST
    t = S5_T
    lr = jnp.minimum(lam_re, -1e-4)
    li = lam_im
    dt = jnp.exp(log_dt)[..., None]
    er = jnp.exp(lr * dt)
    abar_re = er * jnp.cos(li * dt)
    abar_im = er * jnp.sin(li * dt)
    nr = abar_re - 1.0
    den = lr * lr + li * li
    coef_re = (nr * lr + abar_im * li) / den
    coef_im = (abar_im * lr - nr * li) / den
    bb_re = coef_re[..., None] * b_re - coef_im[..., None] * b_im
    bb_im = coef_re[..., None] * b_im + coef_im[..., None] * b_re
    kk = jnp.arange(t + 1, dtype=F32)[:, None, None, None]
    mag = jnp.exp(kk * (lr * dt)[None])
    pw_re = mag * jnp.cos(kk * (li * dt)[None])
    pw_im = mag * jnp.sin(kk * (li * dt)[None])
    g, hh = d_skip.shape
    ct_re = c_re.transpose(0, 1, 3, 2)
    ct_im = c_im.transpose(0, 1, 3, 2)
    flat = lambda a: a.reshape(2, g, S5_STATE, hh * hh)
    bc_re = flat(bb_re[..., :, None] * ct_re[..., None, :] - bb_im[..., :, None] * ct_im[..., None, :])
    bc_im = flat(bb_re[..., :, None] * ct_im[..., None, :] + bb_im[..., :, None] * ct_re[..., None, :])
    lagk = (jnp.einsum('kdgp,dgpn->dgkn', pw_re, bc_re, precision=hi)
            - jnp.einsum('kdgp,dgpn->dgkn', pw_im, bc_im, precision=hi))
    ti = jnp.arange(t)
    lag = ti[None, :] - ti[:, None]
    sel = lambda m: m[None, :, :, None]
    skip = (jnp.eye(hh, dtype=F32)[None] * d_skip[:, None, :]).reshape(g, 1, 1, hh * hh)
    ktot = (jnp.where(sel(lag >= 0), lagk[0][:, jnp.clip(lag, 0, t)], 0.0)
            + jnp.where(sel(lag <= 0), lagk[1][:, jnp.clip(-lag, 0, t)], 0.0)
            + jnp.where(sel(lag == 0), skip, 0.0))
    ktot = ktot.reshape(g, t, t, hh, hh).transpose(0, 1, 3, 2, 4).reshape(g, t * hh, t * hh)

    def bsum(pw_r, pw_i, d):
        wr = pw_r[:, :, :, None] * bb_re[d][None] - pw_i[:, :, :, None] * bb_im[d][None]
        wi = pw_r[:, :, :, None] * bb_im[d][None] + pw_i[:, :, :, None] * bb_re[d][None]
        tr = lambda a: a.transpose(1, 0, 3, 2).reshape(g, t * hh, S5_STATE)
        return tr(wr), tr(wi)

    wf_re, wf_im = bsum(pw_re[:t, 0][::-1], pw_im[:t, 0][::-1], 0)
    wb_re, wb_im = bsum(pw_re[:t, 1], pw_im[:t, 1], 1)
    padl = lambda a: jnp.pad(a, ((0, 0), (0, 0), (0, CHUNK - S5_STATE)))
    wtot = jnp.concatenate([padl(wf_re), padl(wf_im), padl(wb_re), padl(wb_im)], axis=-1)

    def vmat(pw_r, pw_i, d):
        vr = pw_r[:, :, None, :] * c_re[d][None] - pw_i[:, :, None, :] * c_im[d][None]
        vi = pw_r[:, :, None, :] * c_im[d][None] + pw_i[:, :, None, :] * c_re[d][None]
        tr = lambda a: a.transpose(1, 3, 0, 2).reshape(g, S5_STATE, t * hh)
        return tr(vr), tr(-vi)

    vf_re, vf_im = vmat(pw_re[1:, 0], pw_im[1:, 0], 0)
    vb_re, vb_im = vmat(pw_re[1:, 1][::-1], pw_im[1:, 1][::-1], 1)
    padr = lambda a: jnp.pad(a, ((0, 0), (0, CHUNK - S5_STATE), (0, 0)))
    vtot = jnp.concatenate([padr(vf_re), padr(vf_im), padr(vb_re), padr(vb_im)], axis=1)
    padv = lambda a: jnp.pad(a, ((0, 0), (0, CHUNK - S5_STATE)))
    at = jnp.concatenate([padv(pw_re[t, 0]), padv(pw_im[t, 0]), padv(pw_re[t, 1]), padv(pw_im[t, 1])], axis=-1)
    return ktot.astype(BF16), wtot.astype(BF16), vtot.astype(BF16), at[:, None, :]


def s5_scan(u, mats):
    b, lp, _ = u.shape
    nchunk = lp // S5_T
    ktot, wtot, vtot, at = mats
    ug = u.astype(BF16).reshape(b, nchunk, S5_T, S5_GROUPS, S5_GROUP).transpose(3, 1, 0, 2, 4)
    ug = ug.reshape(S5_GROUPS, nchunk * b, S5_COLS)
    rows = nchunk * b
    per_g = lambda g: (g, 0, 0)
    y = pl.pallas_call(
        functools.partial(_s5_kernel, nchunk=nchunk, nb=b),
        grid=(S5_GROUPS,),
        in_specs=[pl.BlockSpec((1, rows, S5_COLS), per_g),
                  pl.BlockSpec((1, S5_COLS, S5_COLS), per_g),
                  pl.BlockSpec((1, S5_COLS, 4 * CHUNK), per_g),
                  pl.BlockSpec((1, 4 * CHUNK, S5_COLS), per_g),
                  pl.BlockSpec((1, 1, 4 * CHUNK), per_g)],
        out_specs=pl.BlockSpec((1, rows, S5_COLS), per_g),
        out_shape=jax.ShapeDtypeStruct((S5_GROUPS, rows, S5_COLS), BF16),
        scratch_shapes=[pltpu.VMEM((rows, 4 * CHUNK), F32), pltpu.VMEM((rows, 4 * CHUNK), F32)],
        compiler_params=_params("parallel"),
        name="s5_scan",
    )(ug, ktot, wtot, vtot, at)
    y = y.reshape(S5_GROUPS, nchunk, b, S5_T, S5_GROUP).transpose(2, 1, 3, 0, 4)
    return y.reshape(b * lp, S5_GROUPS * S5_GROUP)


def _s5_glu_kernel(y_ref, w_ref, b_ref, o_ref):
    y = y_ref[...]
    z = _dot(y, w_ref[...]) + b_ref[...]
    o_ref[...] = (y.astype(F32) * _sigmoid(z)).astype(o_ref.dtype)


def s5_glu(y, w, bias):
    m, n = y.shape
    tm = _pick(m, (1024, 512, 384, 256, 128))
    return pl.pallas_call(
        _s5_glu_kernel,
        grid=(m // tm,),
        in_specs=[pl.BlockSpec((tm, n), lambda i: (i, 0)),
                  pl.BlockSpec((n, n), lambda i: (0, 0)),
                  pl.BlockSpec((1, n), lambda i: (0, 0))],
        out_specs=pl.BlockSpec((tm, n), lambda i: (i, 0)),
        out_shape=jax.ShapeDtypeStruct((m, n), BF16),
        compiler_params=_params("parallel"),
        name="s5_glu",
    )(y, w, bias.reshape(1, n).astype(F32))


def _head_norm(x, gain):
    xc = x - jnp.mean(x, axis=-1, keepdims=True)
    return xc * lax.rsqrt(jnp.mean(xc * xc, axis=-1, keepdims=True) + NORM_EPS) * gain


def _ret_kernel(q_ref, k_ref, v_ref, g_ref, c_ref, s_ref, lg_ref, gn_ref, o_ref,
                qs_ref, ks_ref, vs_ref, vt_ref, af_ref, ab_ref, st_ref, *, nchunk):
    c = c_ref[...]
    s = s_ref[...]
    ii = lax.broadcasted_iota(jnp.int32, (CHUNK, CHUNK), 0).astype(F32)
    jj = lax.broadcasted_iota(jnp.int32, (CHUNK, CHUNK), 1).astype(F32)
    lane = (lax.broadcasted_iota(jnp.int32, (1, nchunk * CHUNK), 1) & (CHUNK - 1)).astype(F32)
    diff = ii - jj
    lanes = [slice(hh * HEAD_DIM, (hh + 1) * HEAD_DIM) for hh in range(HEADS_PER_STEP)]
    consts = []
    for hh, ln in enumerate(lanes):
        q = q_ref[0, :, ln].astype(F32)
        qs_ref[:, ln] = ((q * c + pltpu.roll(q, HEAD_DIM // 2, 1) * s) * HEAD_DIM ** -0.5).astype(BF16)
        k = k_ref[0, :, ln].astype(F32)
        ks_ref[:, ln] = (k * c + pltpu.roll(k, HEAD_DIM // 2, 1) * s).astype(BF16)
        v = v_ref[0, :, ln].astype(F32)
        vs_ref[:, ln] = v.astype(BF16)
        v_t = v.T
        lgf = lg_ref[hh, 0:1, :]
        lgb = lg_ref[hh, 1:2, :]
        vt_ref[2 * hh] = (v_t * jnp.exp((CHUNK - 1 - lane) * lgf[:, 0:1])).astype(BF16)
        vt_ref[2 * hh + 1] = (v_t * jnp.exp(lane * lgb[:, 0:1])).astype(BF16)
        fwd = (jnp.where(diff >= 0, jnp.exp(jnp.where(diff >= 0, diff, 0.0) * lgf), 0.0),
               jnp.exp((ii + 1.0) * lgf), jnp.exp(CHUNK * lgf))
        bwd = (jnp.where(diff < 0, jnp.exp(jnp.where(diff < 0, -diff, 0.0) * lgb), 0.0),
               jnp.exp((CHUNK - ii) * lgb), jnp.exp(CHUNK * lgb))
        consts.append((fwd, bwd))
    st_ref[...] = jnp.zeros_like(st_ref)

    def chunk(r, ln, slot, cst, out_ref):
        dec, xi, gc = cst
        qc = qs_ref[pl.ds(r, CHUNK), ln]
        kc = ks_ref[pl.ds(r, CHUNK), ln]
        vc = vs_ref[pl.ds(r, CHUNK), ln]
        state_t = st_ref[slot]
        both = _dot_nt(qc, jnp.concatenate([kc, state_t.astype(BF16)], axis=0))
        sc = both[:, :CHUNK] * dec
        res = _dot(jnp.concatenate([sc.astype(BF16), vt_ref[slot, :, pl.ds(r, CHUNK)]], axis=0),
                   jnp.concatenate([vc, kc], axis=1))
        out_ref[pl.ds(r, CHUNK), ln] = res[:CHUNK, :HEAD_DIM] + both[:, CHUNK:] * xi
        st_ref[slot] = gc * state_t + res[CHUNK:, HEAD_DIM:]

    def body(t, carry):
        rf = pl.multiple_of(t * CHUNK, CHUNK)
        rb = pl.multiple_of((nchunk - 1 - t) * CHUNK, CHUNK)
        for hh, ln in enumerate(lanes):
            chunk(rf, ln, 2 * hh, consts[hh][0], af_ref)
            chunk(rb, ln, 2 * hh + 1, consts[hh][1], ab_ref)
        return carry

    lax.fori_loop(0, nchunk, body, 0)
    for ln in lanes:
        gate = g_ref[0, :, ln].astype(F32)
        y = _head_norm(af_ref[:, ln] + ab_ref[:, ln], gn_ref[:, ln])
        o_ref[0, :, ln] = (y * (gate * _sigmoid(gate))).astype(o_ref.dtype)


def retention(proj3, log_gamma, gain, cos_t, sin_t):
    b, lp, _ = proj3.shape
    hd = HEAD_DIM
    hps = HEADS_PER_STEP
    wd = hps * hd
    nblk = RET_HEADS // hps
    lg = jnp.broadcast_to(log_gamma.T[:, :, None], (RET_HEADS, 2, hd)).astype(F32)
    blk = lambda off: pl.BlockSpec((1, lp, wd), lambda bi, hi: (bi, 0, off + hi))
    full = lambda bi, hi: (0, 0)
    return pl.pallas_call(
        functools.partial(_ret_kernel, nchunk=lp // CHUNK),
        grid=(b, nblk),
        in_specs=[blk(0), blk(nblk), blk(2 * nblk), blk(3 * nblk),
                  pl.BlockSpec((lp, hd), full), pl.BlockSpec((lp, hd), full),
                  pl.BlockSpec((hps, 2, hd), lambda bi, hi: (hi, 0, 0)),
                  pl.BlockSpec((1, wd), lambda bi, hi: (0, hi))],
        out_specs=pl.BlockSpec((1, lp, wd), lambda bi, hi: (bi, 0, hi)),
        out_shape=jax.ShapeDtypeStruct((b, lp, RET_HEADS * hd), BF16),
        scratch_shapes=[pltpu.VMEM((lp, wd), BF16), pltpu.VMEM((lp, wd), BF16), pltpu.VMEM((lp, wd), BF16),
                        pltpu.VMEM((2 * hps, hd, lp), BF16), pltpu.VMEM((lp, wd), F32), pltpu.VMEM((lp, wd), F32),
                        pltpu.VMEM((2 * hps, hd, hd), F32)],
        compiler_params=_params("parallel", "parallel"),
        name="retention",
    )(proj3, proj3, proj3, proj3, cos_t, sin_t, lg, gain.reshape(1, RET_HEADS * hd).astype(F32))


def _mlstm_gate_tables(li, lf, tri_sum):
    bt = jnp.dot(lf, tri_sum, preferred_element_type=F32, precision=lax.Precision.HIGHEST)
    bt_last = jnp.sum(lf, axis=1, keepdims=True)
    a = bt_last - bt + li
    m_loc = jnp.max(a, axis=1, keepdims=True)
    return bt, jnp.exp(a - m_loc), m_loc, bt_last, li - bt


def _mlstm_chunk_t(kc, qtc, vtc, vwc, bt_row, w_row, m_loc, bt_last, colb, mask_t, ct_prev, n_prev, m_prev):
    d, c, pk = HEAD_DIM, CHUNK, BF16_ROWS
    dlog_t = jnp.where(mask_t, bt_row + colb, -jnp.inf)
    g_row = bt_row + m_prev
    m_t = jnp.maximum(g_row, jnp.max(dlog_t, axis=0, keepdims=True))
    r1 = _dot(jnp.concatenate([kc, ct_prev.astype(BF16), jnp.broadcast_to(n_prev, (pk, d)).astype(BF16)], axis=0),
              qtc)
    s_t = r1[:c] * jnp.exp(dlog_t - m_t)
    w_int = jnp.exp(g_row - m_t)
    r2 = _dot(jnp.concatenate([vtc, vwc, jnp.broadcast_to(w_row, (pk, c)).astype(BF16)], axis=0),
              jnp.concatenate([s_t.astype(BF16), kc], axis=1))
    num_t = r2[:d, :c] + w_int * r1[c:c + d]
    den_t = jnp.sum(s_t, axis=0, keepdims=True) + w_int * r1[c + d:c + d + 1]
    out_t = num_t / jnp.maximum(jnp.abs(den_t), jnp.exp(-m_t))
    m_new = jnp.maximum(bt_last + m_prev, m_loc)
    f_prev = jnp.exp(bt_last + m_prev - m_new)
    f_loc = jnp.exp(m_loc - m_new)
    c_new = f_prev * ct_prev + f_loc * r2[d:2 * d, c:]
    n_new = f_prev * n_prev + f_loc * r2[2 * d:2 * d + 1, c:]
    return out_t, c_new, n_new, m_new


def _mlstm_kernel(mu_ref, mo_ref, gt_ref, gb_ref, cw_ref, cb_ref, wk_ref, wqt_ref, wvt_ref, gn_ref, o_ref,
                  ks_ref, qt_ref, vt_ref, vw_ref, row_ref, d_ref, colb_ref, aft_ref, abt_ref,
                  cs_ref, ns_ref, ms_ref, *, lp, pad, nchunk):
    hd = HEAD_DIM
    lanes = [slice(hh * hd, (hh + 1) * hd) for hh in range(HEADS_PER_STEP)]
    valid_row = lax.broadcasted_iota(jnp.int32, (lp, 1), 0) >= pad
    valid_col = lax.broadcasted_iota(jnp.int32, (1, lp), 1) >= pad
    pos = (lax.broadcasted_iota(jnp.int32, (nchunk, CHUNK), 0) * CHUNK
           + lax.broadcasted_iota(jnp.int32, (nchunk, CHUNK), 1))
    valid_pos = pos >= pad
    ii = lax.broadcasted_iota(jnp.int32, (CHUNK, CHUNK), 0)
    jj = lax.broadcasted_iota(jnp.int32, (CHUNK, CHUNK), 1)
    eye = ii == jj
    upper = ii <= jj
    lower = ii >= jj
    ones = jnp.ones((CHUNK, CHUNK), BF16)
    for hh, ln in enumerate(lanes):
        mu = mu_ref[0, :, ln].astype(F32)
        conv = cb_ref[:, ln]
        for j in range(CONV_W):
            conv = conv + cw_ref[j:j + 1, ln] * pltpu.roll(mu, (CONV_W // 2 - j) % lp, 0)
        uc = (conv * _sigmoid(conv)).astype(BF16)
        ks_ref[:, ln] = jnp.where(valid_row, _dot(uc, wk_ref[hh]) * hd ** -0.5, 0.0).astype(BF16)
        qt_ref[ln, :] = jnp.where(valid_col, _dot_nt(wqt_ref[hh], uc), 0.0).astype(BF16)
        v_t = jnp.where(valid_col, _dot_nt(wvt_ref[hh], mu.astype(BF16)), 0.0)
        vt_ref[ln, :] = v_t.astype(BF16)

        for d, tri_sum in enumerate((upper, lower)):
            g_i = gt_ref[0, hh, 2 * d] + gb_ref[hh, 2 * d:2 * d + 1, 0:1]
            g_f = gt_ref[0, hh, 2 * d + 1] + gb_ref[hh, 2 * d + 1:2 * d + 2, 0:1]
            li = jnp.where(valid_pos, g_i, NEG_GATE)
            lf = jnp.where(valid_pos, jnp.minimum(g_f, 0.0) - jnp.log(1.0 + jnp.exp(-jnp.abs(g_f))), 0.0)
            bt, w, m_loc, bt_last, colv = _mlstm_gate_tables(li, lf, jnp.where(tri_sum, 1.0, 0.0))
            slot = 2 * hh + d
            row_ref[4 * slot + 0] = bt
            row_ref[4 * slot + 1] = w
            row_ref[4 * slot + 2] = jnp.broadcast_to(m_loc, (nchunk, CHUNK))
            row_ref[4 * slot + 3] = jnp.broadcast_to(bt_last, (nchunk, CHUNK))
            for n in range(nchunk):
                cols = slice(n * CHUNK, (n + 1) * CHUNK)
                vw_ref[slot, :, cols] = (v_t[:, cols] * w[n:n + 1, :]).astype(BF16)
                d_ref[cols, :] = jnp.where(eye, colv[n:n + 1, :], 0.0)
            diag = d_ref[...]
            d_hi = diag.astype(BF16)
            d_lo = (diag - d_hi.astype(F32)).astype(BF16)
            colb_ref[slot] = _dot(d_hi, ones) + _dot(d_lo, ones)
    cs_ref[...] = jnp.zeros_like(cs_ref)
    ns_ref[...] = jnp.zeros_like(ns_ref)
    ms_ref[...] = jnp.zeros_like(ms_ref)

    def run(n, r, ln, slot, mask_t, out_ref):
        row = lambda kind: row_ref[4 * slot + kind, pl.ds(n, 1), :]
        out_t, c_s, n_s, m_s = _mlstm_chunk_t(
            ks_ref[pl.ds(r, CHUNK), ln], qt_ref[ln, pl.ds(r, CHUNK)], vt_ref[ln, pl.ds(r, CHUNK)],
            vw_ref[slot, :, pl.ds(r, CHUNK)],
            row(0), row(1), row(2)[:, 0:1], row(3)[:, 0:1], colb_ref[slot, pl.ds(r, CHUNK), :], mask_t,
            cs_ref[slot], ns_ref[slot], ms_ref[slot][:, 0:1])
        out_ref[ln, pl.ds(r, CHUNK)] = out_t
        cs_ref[slot] = c_s
        ns_ref[slot] = n_s
        ms_ref[slot] = jnp.broadcast_to(m_s, (1, hd))

    def body(t, carry):
        tb = nchunk - 1 - t
        rf = pl.multiple_of(t * CHUNK, CHUNK)
        rb = pl.multiple_of(tb * CHUNK, CHUNK)
        for hh, ln in enumerate(lanes):
            run(t, rf, ln, 2 * hh, upper, aft_ref)
            run(tb, rb, ln, 2 * hh + 1, lower, abt_ref)
        return carry

    lax.fori_loop(0, nchunk, body, 0)
    for ln in lanes:
        x_t = aft_ref[ln, :] + abt_ref[ln, :]
        xc = x_t - jnp.mean(x_t, axis=0, keepdims=True)
        y = (xc * lax.rsqrt(jnp.mean(xc * xc, axis=0, keepdims=True) + NORM_EPS)).T
        o_ref[0, :, ln] = (y * gn_ref[:, ln] * _sigmoid(mo_ref[0, :, ln].astype(F32))).astype(o_ref.dtype)


def mlstm(proj3, gates, gate_b, conv_w, conv_b, wq, wk, wv, gain, *, pad, mu_col0, mo_col0):
    b, lp, _ = proj3.shape
    hd = HEAD_DIM
    hps = HEADS_PER_STEP
    wd = hps * hd
    nchunk = lp // CHUNK
    gt = gates.reshape(b, lp, 4, ML_HEADS).transpose(0, 3, 2, 1).reshape(b, ML_HEADS, 4, nchunk, CHUNK)
    gb = jnp.broadcast_to(gate_b.T[:, :, None], (ML_HEADS, 4, hd)).astype(F32)
    blk = lambda off: pl.BlockSpec((1, lp, wd), lambda bi, hi: (bi, 0, off // hps + hi))
    per_h = lambda bi, hi: (hi, 0, 0)
    vec = pl.BlockSpec((1, wd), lambda bi, hi: (0, hi))
    sq = pl.BlockSpec((hps, hd, hd), per_h)
    tr = lambda w: jnp.swapaxes(w, 1, 2).astype(BF16)
    return pl.pallas_call(
        functools.partial(_mlstm_kernel, lp=lp, pad=pad, nchunk=nchunk),
        grid=(b, ML_HEADS // hps),
        in_specs=[blk(mu_col0), blk(mo_col0),
                  pl.BlockSpec((1, hps, 4, nchunk, CHUNK), lambda bi, hi: (bi, hi, 0, 0, 0)),
                  pl.BlockSpec((hps, 4, hd), per_h),
                  pl.BlockSpec((CONV_W, wd), lambda bi, hi: (0, hi)),
                  vec, sq, sq, sq, vec],
        out_specs=pl.BlockSpec((1, lp, wd), lambda bi, hi: (bi, 0, hi)),
        out_shape=jax.ShapeDtypeStruct((b, lp, ML_HEADS * hd), BF16),
        scratch_shapes=[pltpu.VMEM((lp, wd), BF16), pltpu.VMEM((wd, lp), BF16), pltpu.VMEM((wd, lp), BF16),
                        pltpu.VMEM((2 * hps, hd, lp), BF16),
                        pltpu.VMEM((8 * hps, nchunk, CHUNK), F32), pltpu.VMEM((lp, CHUNK), F32),
                        pltpu.VMEM((2 * hps, lp, CHUNK), F32),
                        pltpu.VMEM((wd, lp), F32), pltpu.VMEM((wd, lp), F32),
                        pltpu.VMEM((2 * hps, hd, hd), F32), pltpu.VMEM((2 * hps, 1, hd), F32),
                        pltpu.VMEM((2 * hps, 1, hd), F32)],
        compiler_params=_params("parallel", "parallel"),
        name="mlstm",
    )(proj3, proj3, gt, gb, conv_w.astype(F32), conv_b.reshape(1, -1).astype(F32),
      wk.astype(BF16), tr(wq), tr(wv), gain.reshape(1, -1).astype(F32))


def _rope_freqs(dim):
    return ROPE_THETA ** (-jnp.arange(dim // 2, dtype=F32) / (dim // 2))


def _axial_tables(n_tok, pad):
    rows = n_tok // GRID_W
    row = jnp.concatenate([jnp.zeros((pad,), F32), -jnp.ones((N_META,), F32),
                           jnp.repeat(jnp.arange(rows, dtype=F32), GRID_W)])
    col = jnp.concatenate([jnp.zeros((pad,), F32), jnp.arange(N_META, dtype=F32),
                           jnp.tile(jnp.arange(GRID_W, dtype=F32), rows)])
    f = _rope_freqs(HEAD_DIM // 2)
    ang = jnp.concatenate([row[:, None] * f[None, :]] * 2 + [col[:, None] * f[None, :]] * 2, axis=-1)
    first = (jnp.arange(HEAD_DIM) % 64) < 32
    sin = jnp.sin(ang)
    return jnp.cos(ang), jnp.where(first, -sin, 0.0), jnp.where(first, 0.0, sin)


def _linear_tables(l, pad):
    pos = jnp.concatenate([jnp.zeros((pad,), F32), jnp.arange(l, dtype=F32)])
    ang = pos[:, None] * _rope_freqs(HEAD_DIM)[None, :]
    ang = jnp.concatenate([ang, ang], axis=-1)
    sin = jnp.sin(ang)
    return jnp.cos(ang), jnp.where(jnp.arange(HEAD_DIM) < HEAD_DIM // 2, -sin, sin)


def _even_mixer_parts(hn, w_in_all, j, q_norm, k_norm, s5_params, glu_w, glu_b, tabs, *, b, lp, pad):
    att_w = ATT_HEADS * HEAD_DIM
    u0 = att_w + 2 * ATT_KV_HEADS * HEAD_DIM
    qkvu = matmul_wcast(hn, w_in_all, j, w_in_all.shape[2], out_dtype=BF16)
    qkvu3 = qkvu.reshape(b, lp, -1)
    att = attention(qkvu3, q_norm, k_norm, tabs, pad=pad).reshape(b * lp, att_w)
    y = s5_scan(qkvu3[:, :, u0:], _s5_matrices(*s5_params))
    ssm = s5_glu(y, glu_w.astype(BF16), glu_b)
    return [att, ssm]


def _odd_mixer_parts(hn, w_in_all, j, ret_log_decay, ret_norm, conv_w, conv_b, wq, wk, wv, gate_b, ml_norm,
                     tabs, *, b, lp, pad):
    ret_w = RET_HEADS * HEAD_DIM
    ml_w = ML_HEADS * HEAD_DIM
    main = 4 * ret_w + 2 * ml_w
    n_gate = w_in_all.shape[2] - main
    proj = matmul_wcast(hn, w_in_all, j, main, out_dtype=BF16)
    proj3 = proj.reshape(b, lp, main)
    w_gate = jnp.pad(w_in_all[j, :, main:].astype(BF16), ((0, 0), (0, CHUNK - n_gate)))
    gates = matmul(hn, w_gate)[:, :n_gate].reshape(b, lp, n_gate)
    log_gamma = -jnp.abs(ret_log_decay.astype(F32))
    ret = retention(proj3, log_gamma, ret_norm, *tabs)
    nblk = ret_w // HEAD_DIM
    hm = mlstm(proj3, gates, gate_b, conv_w, conv_b, wq, wk, wv, ml_norm,
               pad=pad, mu_col0=4 * nblk, mo_col0=4 * nblk + ml_w // HEAD_DIM)
    return [ret.reshape(b * lp, ret_w), hm.reshape(b * lp, ml_w)]


def kernel(x, meta_tokens, norm_gains, mlp_w1, mlp_w2, even_w_in, even_w_out, att_q_norm, att_k_norm, s5_lam_re, s5_lam_im, s5_log_dt, s5_b_re, s5_b_im, s5_c_re, s5_c_im, s5_d, s5_glu_w, s5_glu_b, odd_w_in, odd_w_out, ret_log_decay, ret_norm, ml_conv_w, ml_conv_b, ml_wq, ml_wk, ml_wv, ml_gate_b, ml_norm):
    b, n_tok, d_model = x.shape
    l = n_tok + N_META
    pad = (-l) % CHUNK
    lp = l + pad
    depth = norm_gains.shape[0]
    h = jnp.concatenate([jnp.zeros((b, pad, d_model), x.dtype),
                         jnp.broadcast_to(meta_tokens.astype(x.dtype)[None], (b, N_META, d_model)), x], axis=1)
    h = h.reshape(b * lp, d_model)
    axial = _axial_tables(n_tok, pad)
    linear = _linear_tables(l, pad)
    dims = dict(b=b, lp=lp, pad=pad)
    hn = rmsnorm(h, norm_gains[0, 0])
    w2_bf16 = mlp_w2.astype(BF16)
    even_out_bf16 = even_w_out.astype(BF16)
    odd_out_bf16 = odd_w_out.astype(BF16)
    for i in range(depth):
        j = i // 2
        if i % 2 == 0:
            s5_params = (s5_lam_re[j], s5_lam_im[j], s5_log_dt[j], s5_b_re[j], s5_b_im[j], s5_c_re[j],
                         s5_c_im[j], s5_d[j])
            parts = _even_mixer_parts(hn, even_w_in, j, att_q_norm[j], att_k_norm[j],
                                      s5_params, s5_glu_w[j], s5_glu_b[j], axial, **dims)
            w_out = even_out_bf16
        else:
            parts = _odd_mixer_parts(hn, odd_w_in, j, ret_log_decay[j], ret_norm[j],
                                     ml_conv_w[j], ml_conv_b[j], ml_wq[j], ml_wk[j], ml_wv[j], ml_gate_b[j],
                                     ml_norm[j], linear, **dims)
            w_out = odd_out_bf16
        h, hn = matmul_norm_res(parts, w_out, j, norm_gains[i, 1], h, norm_gains[i, 2], lp=lp, pad=pad)
        hid = matmul_wcast(hn, mlp_w1, i, mlp_w1.shape[2], relu2=True, out_dtype=BF16)
        last = i + 1 == depth
        next_gain = None if last else norm_gains[i + 1, 0]
        drop = pad + N_META if last and n_tok % CHUNK == 0 else 0
        h, hn = matmul_norm_res([hid], w2_bf16, i, norm_gains[i, 3], h, next_gain, lp=lp, pad=pad, drop_head=drop)
    if drop:
        return h.reshape(b, n_tok, d_model)
    return h.reshape(b, lp, d_model)[:, pad + N_META:]
```
